```python
import jax, jax.numpy as jnp
from jax import lax
import numpy as np

D_MODEL = 1024
BATCH = 2
SEQ = 16384
DEPTH = 1
DEC_BATCH = 32
DEC_SEQ = 32
PAST_LEN = 2048

CHUNK = 64
LEFT_CHUNKS = 8
BAND = (LEFT_CHUNKS + 1) * CHUNK
HEAD_DIM = 64
ATTN_WIDTH = D_MODEL // 2
ATTN_HEADS = ATTN_WIDTH // HEAD_DIM
LRU_WIDTH = D_MODEL // 2
LRU_BLOCKS = 8
LRU_BLOCK = LRU_WIDTH // LRU_BLOCKS
CONV_WIDTH = 4
LRU_C = 8.0
REL_CLIP = 128
REL_SIZE = REL_CLIP + CHUNK
MIX_WIDTH = ATTN_WIDTH + LRU_WIDTH
IN_WIDTH = 3 * ATTN_WIDTH + 2 * LRU_WIDTH
D_FF = -(-8 * D_MODEL // (3 * 256)) * 256
PLE_DIM = 256
EPS = 1e-6
NEG = -1e30
SCALE = HEAD_DIM ** -0.5

kernel_name = 'hymba_rglru_chunkband_stream_step'


def rmsnorm(x, g):
    xf = x.astype(jnp.float32)
    y = xf * lax.rsqrt(jnp.mean(xf * xf, axis=-1, keepdims=True) + EPS)
    return (y * g.astype(jnp.float32)).astype(x.dtype)


def rel_bias(rel_table, rel):
    idx = jnp.clip(rel, -REL_CLIP, CHUNK - 1) + REL_CLIP
    return rel_table.astype(jnp.float32)[:, idx]


def softmax_attend(q, k, v, bias, mask=None):
    s = jnp.einsum('bqhd,bkhd->bhqk', q.astype(jnp.float32), k.astype(jnp.float32)) * SCALE + bias
    if mask is not None:
        s = jnp.where(mask, s, NEG)
    p = jax.nn.softmax(s, axis=-1)
    return jnp.einsum('bhqk,bkhd->bqhd', p, v.astype(jnp.float32)).astype(q.dtype)


def chunk_band_attention(q, k, v, rel_table):
    b_, s_ = q.shape[:2]
    n_chunks = s_ // CHUNK
    pad = LEFT_CHUNKS * CHUNK
    kp = jnp.pad(k, ((0, 0), (pad, 0), (0, 0), (0, 0)))
    vp = jnp.pad(v, ((0, 0), (pad, 0), (0, 0), (0, 0)))
    kpos = jnp.arange(BAND) - pad
    qpos = jnp.arange(CHUNK)
    bias = rel_bias(rel_table, kpos[None, :] - qpos[:, None])

    def one_chunk(c):
        start = c * CHUNK
        qb = lax.dynamic_slice_in_dim(q, start, CHUNK, axis=1)
        kb = lax.dynamic_slice_in_dim(kp, start, BAND, axis=1)
        vb = lax.dynamic_slice_in_dim(vp, start, BAND, axis=1)
        valid = (start + kpos) >= pad
        return softmax_attend(qb, kb, vb, bias, valid[None, None, None, :])

    out = lax.map(one_chunk, jnp.arange(n_chunks))
    return jnp.moveaxis(out, 0, 1).reshape(b_, s_, ATTN_WIDTH)


def cached_band_attention(q, k, v, cache_k, cache_v, rel_table):
    b_, t = q.shape[:2]
    n_cache = cache_k.shape[1]
    keys = jnp.concatenate([cache_k.astype(k.dtype), k], axis=1)
    vals = jnp.concatenate([cache_v.astype(v.dtype), v], axis=1)
    kpos = jnp.concatenate([jnp.arange(n_cache) - n_cache, jnp.arange(t)])
    bias = rel_bias(rel_table, kpos[None, :] - jnp.arange(t)[:, None])
    return softmax_attend(q, keys, vals, bias).reshape(b_, t, ATTN_WIDTH)


def causal_conv(xb, state, conv_w, conv_b):
    t = xb.shape[1]
    xp = jnp.concatenate([state.astype(xb.dtype), xb], axis=1)
    y = conv_b + sum(xp[:, j:j + t] * conv_w[j] for j in range(CONV_WIDTH))
    return y, xp[:, -(CONV_WIDTH - 1):]


def rglru(xc, h0, w_rgate, b_rgate, w_igate, b_igate, lru_lambda):
    b_, t = xc.shape[:2]
    xf = xc.astype(jnp.float32)
    xblk = xf.reshape(b_, t, LRU_BLOCKS, LRU_BLOCK)
    r = jax.nn.sigmoid(jnp.einsum('btnk,nkj->btnj', xblk, w_rgate.astype(jnp.float32)).reshape(b_, t, LRU_WIDTH) + b_rgate.astype(jnp.float32))
    ig = jax.nn.sigmoid(jnp.einsum('btnk,nkj->btnj', xblk, w_igate.astype(jnp.float32)).reshape(b_, t, LRU_WIDTH) + b_igate.astype(jnp.float32))
    log_a = -LRU_C * r * jax.nn.softplus(-lru_lambda.astype(jnp.float32))
    a = jnp.exp(log_a)
    u = jnp.sqrt(-jnp.expm1(2.0 * log_a)) * (ig * xf)
    u = u.at[:, 0].add(a[:, 0] * h0.astype(jnp.float32))

    def combine(left, right):
        a_l, u_l = left
        a_r, u_r = right
        return a_l * a_r, a_r * u_l + u_r

    _, hs = lax.associative_scan(combine, (a, u), axis=1)
    return hs.astype(xc.dtype), hs[:, -1]


def trunk_layer(h, p, attend, conv_state, lru_state, g_mix, w_in, conv_w, conv_b, w_rgate, b_rgate,
                w_igate, b_igate, lru_lambda, g_attn_out, g_lru_out, w_out, g_ffn, w_ffn_gate,
                w_ffn_up, w_ffn_down, g_ple, w_ple_gate, w_ple_proj):
    b_, t = h.shape[:2]
    xn = rmsnorm(h, g_mix)
    proj = xn @ w_in
    q, k, v, xr, gr = jnp.split(proj, [ATTN_WIDTH, 2 * ATTN_WIDTH, 3 * ATTN_WIDTH, 3 * ATTN_WIDTH + LRU_WIDTH], axis=-1)
    hd_shape = (b_, t, ATTN_HEADS, HEAD_DIM)
    attn_out, k_new, v_new = attend(q.reshape(hd_shape), k.reshape(hd_shape), v.reshape(hd_shape))
    xc, conv_new = causal_conv(xr, conv_state, conv_w, conv_b)
    hs, lru_new = rglru(xc, lru_state, w_rgate, b_rgate, w_igate, b_igate, lru_lambda)
    lru_out = hs * jax.nn.gelu(gr, approximate=True)
    mixed = jnp.concatenate([rmsnorm(attn_out, g_attn_out), rmsnorm(lru_out, g_lru_out)], axis=-1)
    h = h + mixed @ w_out
    hn = rmsnorm(h, g_ffn)
    h = h + (jax.nn.silu(hn @ w_ffn_gate) * (hn @ w_ffn_up)) @ w_ffn_down
    gate = jax.nn.sigmoid(rmsnorm(h, g_ple) @ w_ple_gate)
    h = h + (p.astype(h.dtype) @ w_ple_proj) * gate
    return h, k_new, v_new, conv_new, lru_new


def setup_inputs(seed: int = 0) -> dict:
    key = jax.random.key(seed)
    ks = jax.random.split(key, 32)
    f32 = jnp.float32

    def nrm(k, shape, scale=1.0):
        return jax.random.normal(k, shape, f32) * scale

    kv_len = min(LEFT_CHUNKS * CHUNK, PAST_LEN)
    a_c = jax.random.uniform(ks[20], (DEPTH, LRU_WIDTH), f32, 0.9, 0.999)
    base = a_c ** (1.0 / LRU_C)
    lru_lambda = jnp.log(base) - jnp.log1p(-base)
    return {
        'x_prompt': nrm(ks[0], (BATCH, SEQ, D_MODEL)),
        'x_sample': nrm(ks[1], (DEC_BATCH, DEC_SEQ, D_MODEL)),
        'p_prompt': nrm(ks[2], (DEPTH, BATCH, SEQ, PLE_DIM)),
        'p_sample': nrm(ks[3], (DEPTH, DEC_BATCH, DEC_SEQ, PLE_DIM)),
        'cache_k': nrm(ks[4], (DEPTH, DEC_BATCH, kv_len, ATTN_HEADS, HEAD_DIM)),
        'cache_v': nrm(ks[5], (DEPTH, DEC_BATCH, kv_len, ATTN_HEADS, HEAD_DIM)),
        'state_conv': nrm(ks[6], (DEPTH, DEC_BATCH, CONV_WIDTH - 1, LRU_WIDTH)),
        'state_h': nrm(ks[7], (DEPTH, DEC_BATCH, LRU_WIDTH), 0.5),
        'g_mix': 1.0 + nrm(ks[8], (DEPTH, D_MODEL), 0.02),
        'w_in': nrm(ks[9], (DEPTH, D_MODEL, IN_WIDTH), D_MODEL ** -0.5),
        'conv_w': nrm(ks[10], (DEPTH, CONV_WIDTH, LRU_WIDTH), CONV_WIDTH ** -0.5),
        'conv_b': nrm(ks[11], (DEPTH, LRU_WIDTH), 0.01),
        'w_rgate': nrm(ks[12], (DEPTH, LRU_BLOCKS, LRU_BLOCK, LRU_BLOCK), LRU_BLOCK ** -0.5),
        'b_rgate': nrm(ks[13], (DEPTH, LRU_WIDTH), 0.01),
        'w_igate': nrm(ks[14], (DEPTH, LRU_BLOCKS, LRU_BLOCK, LRU_BLOCK), LRU_BLOCK ** -0.5),
        'b_igate': nrm(ks[15], (DEPTH, LRU_WIDTH), 0.01),
        'lru_lambda': lru_lambda,
        'rel_bias_table': nrm(ks[16], (DEPTH, ATTN_HEADS, REL_SIZE), 0.1),
        'g_attn_out': 1.0 + nrm(ks[17], (DEPTH, ATTN_WIDTH), 0.02),
        'g_lru_out': 1.0 + nrm(ks[18], (DEPTH, LRU_WIDTH), 0.02),
        'w_out': nrm(ks[19], (DEPTH, MIX_WIDTH, D_MODEL), MIX_WIDTH ** -0.5),
        'g_ffn': 1.0 + nrm(ks[21], (DEPTH, D_MODEL), 0.02),
        'w_ffn_gate': nrm(ks[22], (DEPTH, D_MODEL, D_FF), D_MODEL ** -0.5),
        'w_ffn_up': nrm(ks[23], (DEPTH, D_MODEL, D_FF), D_MODEL ** -0.5),
        'w_ffn_down': nrm(ks[24], (DEPTH, D_FF, D_MODEL), D_FF ** -0.5),
        'g_ple': 1.0 + nrm(ks[25], (DEPTH, D_MODEL), 0.02),
        'w_ple_gate': nrm(ks[26], (DEPTH, D_MODEL, D_MODEL), D_MODEL ** -0.5),
        'w_ple_proj': nrm(ks[27], (DEPTH, PLE_DIM, D_MODEL), PLE_DIM ** -0.5),
        'g_final': 1.0 + nrm(ks[28], (D_MODEL,), 0.02),
    }


def reference(x_prompt, x_sample, p_prompt, p_sample, cache_k, cache_v, state_conv, state_h,
              g_mix, w_in, conv_w, conv_b, w_rgate, b_rgate, w_igate, b_igate, lru_lambda,
              rel_bias_table, g_attn_out, g_lru_out, w_out, g_ffn, w_ffn_gate, w_ffn_up,
              w_ffn_down, g_ple, w_ple_gate, w_ple_proj, g_final):
    hp = x_prompt
    hsmp = x_sample
    n_keep = min(LEFT_CHUNKS * CHUNK, x_prompt.shape[1])
    kp_l, vp_l, cp_l, hp_l = [], [], [], []
    ks_l, vs_l, cs_l, hs_l = [], [], [], []
    for i in range(DEPTH):
        lw = (g_mix[i], w_in[i], conv_w[i], conv_b[i], w_rgate[i], b_rgate[i], w_igate[i], b_igate[i],
              lru_lambda[i], g_attn_out[i], g_lru_out[i], w_out[i], g_ffn[i], w_ffn_gate[i],
              w_ffn_up[i], w_ffn_down[i], g_ple[i], w_ple_gate[i], w_ple_proj[i])
        table = rel_bias_table[i]

        def attend_prompt(q, k, v, table=table):
            return chunk_band_attention(q, k, v, table), k[:, -n_keep:], v[:, -n_keep:]

        def attend_sample(q, k, v, table=table, ck=cache_k[i], cv=cache_v[i]):
            return cached_band_attention(q, k, v, ck, cv, table), k, v

        zero_conv = jnp.zeros((hp.shape[0], CONV_WIDTH - 1, LRU_WIDTH), hp.dtype)
        zero_h = jnp.zeros((hp.shape[0], LRU_WIDTH), jnp.float32)
        hp, k1, v1, c1, r1 = trunk_layer(hp, p_prompt[i], attend_prompt, zero_conv, zero_h, *lw)
        hsmp, k2, v2, c2, r2 = trunk_layer(hsmp, p_sample[i], attend_sample, state_conv[i], state_h[i], *lw)
        kp_l.append(k1); vp_l.append(v1); cp_l.append(c1); hp_l.append(r1)
        ks_l.append(k2); vs_l.append(v2); cs_l.append(c2); hs_l.append(r2)
    y_prompt = rmsnorm(hp, g_final)
    y_sample = rmsnorm(hsmp, g_final)
    return (y_prompt, y_sample,
            jnp.stack(kp_l), jnp.stack(vp_l), jnp.stack(cp_l), jnp.stack(hp_l),
            jnp.stack(ks_l), jnp.stack(vs_l), jnp.stack(cs_l), jnp.stack(hs_l))
```

```python
import functools

import jax
import jax.numpy as jnp
from jax import lax
from jax.experimental import pallas as pl
from jax.experimental.pallas import tpu as pltpu

CHUNK = 64
LEFT_CHUNKS = 8
PAD = LEFT_CHUNKS * CHUNK
BAND = PAD + CHUNK
HEAD_DIM = 64
CONV_WIDTH = 4
LRU_C = 8.0
REL_CLIP = 128
EPS = 1e-6
NEG = -1e30
SCALE = HEAD_DIM ** -0.5
LANES = 128
SUBLANES = 8
EXT = 640
VMEM_LIMIT = 60 * 1024 * 1024

F32 = jnp.float32
BF16 = jnp.bfloat16


def _dot(a, b):
    return jnp.dot(a, b, preferred_element_type=F32)


def _dot_nt(a, b):
    return lax.dot_general(a, b, (((1,), (1,)), ((), ())), preferred_element_type=F32)


def _rms(x, g):
    ms = jnp.mean(x * x, axis=-1, keepdims=True)
    return (x * lax.rsqrt(ms + EPS)) * g


def _gelu_tanh(x):
    c = 0.7978845608028654
    return 0.5 * x * (1.0 + jnp.tanh(c * (x + 0.044715 * (x * x * x))))


def _shift_rows(x, d, fill, seg):
    rolled = pltpu.roll(x, d, axis=0)
    row = lax.broadcasted_iota(jnp.int32, x.shape, 0)
    return jnp.where((row & (seg - 1)) >= d, rolled, fill)


def _linear_scan(a, u, seg):
    d = 1
    while d < seg:
        u = u + a * _shift_rows(u, d, 0.0, seg)
        if 2 * d < seg:
            a = a * _shift_rows(a, d, 1.0, seg)
        d *= 2
    return u


def _rglru(xc, gr, h0_rows, seg, w_r, b_r, w_i, b_i, lam):
    xcb = xc.astype(BF16)
    r = jax.nn.sigmoid(_dot(xcb, w_r) + b_r)
    ig = jax.nn.sigmoid(_dot(xcb, w_i) + b_i)
    z = -lam
    softplus = jnp.maximum(z, 0.0) + jnp.log1p(jnp.exp(-jnp.abs(z)))
    log_a = (-LRU_C) * r * softplus
    a = jnp.exp(log_a)
    u = jnp.sqrt(1.0 - a * a) * (ig * xc)
    row = lax.broadcasted_iota(jnp.int32, xc.shape, 0)
    u = u + jnp.where((row & (seg - 1)) == 0, a * h0_rows, 0.0)
    hs = _linear_scan(a, u, seg)
    return hs, hs * _gelu_tanh(gr)


def _dense_tail(h, attn, lru_out, p, w, final_norm):
    mixed = jnp.concatenate(
        [_rms(attn, w["g_attn_out"][...]), _rms(lru_out, w["g_lru_out"][...])], axis=-1)
    h = h + _dot(mixed.astype(BF16), w["w_out"][...])
    hn = _rms(h, w["g_ffn"][...]).astype(BF16)
    d_ff = w["w_ffn_gate"].shape[1]
    n_split = 2 if d_ff % (2 * LANES) == 0 else 1
    step = d_ff // n_split
    acc = h
    for c in range(n_split):
        cols = slice(c * step, (c + 1) * step)
        gate = _dot(hn, w["w_ffn_gate"][:, cols])
        up = _dot(hn, w["w_ffn_up"][:, cols])
        act = (gate * jax.nn.sigmoid(gate)) * up
        acc = acc + _dot(act.astype(BF16), w["w_ffn_down"][cols, :])
    h = acc
    gate = jax.nn.sigmoid(_dot(_rms(h, w["g_ple"][...]).astype(BF16), w["w_ple_gate"][...]))
    h = h + _dot(p.astype(BF16), w["w_ple_proj"][...]) * gate
    if final_norm:
        h = _rms(h, w["g_final"][...])
    return h


def _softmax_pv(scores, values):
    m = functools.reduce(jnp.maximum, [jnp.max(s, axis=-1, keepdims=True) for s in scores])
    num = None
    den = None
    for s, v in zip(scores, values):
        e = jnp.exp(s - m)
        l = jnp.sum(e, axis=-1, keepdims=True)
        o = _dot(e.astype(BF16), v)
        num = o if num is None else num + o
        den = l if den is None else den + l
    return num / den


WEIGHT_NAMES = (
    "g_mix", "w_in", "conv_w", "conv_b", "w_r", "b_r", "w_i", "b_i", "lam",
    "g_attn_out", "g_lru_out", "w_out", "g_ffn", "w_ffn_gate", "w_ffn_up", "w_ffn_down",
    "g_ple", "w_ple_gate", "w_ple_proj", "g_final")


def _project(x, w):
    xn = _rms(x, w["g_mix"][...]).astype(BF16)
    proj = _dot(xn, w["w_in"][...])
    aw = w["g_attn_out"].shape[1]
    lw = w["g_lru_out"].shape[1]
    q = proj[:, :aw] * SCALE
    k = proj[:, aw:2 * aw]
    v = proj[:, 2 * aw:3 * aw]
    xr = proj[:, 3 * aw:3 * aw + lw]
    gr = proj[:, 3 * aw + lw:]
    return q, k, v, xr, gr


def _conv(xr_buf, base, rows, w):
    cw = w["conv_w"]
    xc = w["conv_b"][...] + xr_buf[base:base + rows, :] * cw[CONV_WIDTH - 1:CONV_WIDTH, :]
    for j in range(CONV_WIDTH - 1):
        off = base - (CONV_WIDTH - 1) + j
        xc = xc + xr_buf[off:off + rows, :] * cw[j:j + 1, :]
    return xc


def _bias_kernel(ext_ref, bias_ref):
    n_heads = ext_ref.shape[0]
    for h in range(n_heads):
        rows = jnp.broadcast_to(ext_ref[h], (CHUNK, EXT))
        rolled = pltpu.roll(rows, EXT - (CHUNK - 1), axis=1, stride=1, stride_axis=0)
        bias_ref[h] = rolled[:, :BAND]


def _bias_table(rel_table):
    n_heads, rel_size = rel_table.shape
    assert rel_size == REL_CLIP + CHUNK
    left = (BAND - 1) - REL_CLIP
    ext = jnp.pad(rel_table.astype(F32), ((0, 0), (left, EXT - left - rel_size)), mode="edge")
    return pl.pallas_call(
        _bias_kernel,
        out_shape=jax.ShapeDtypeStruct((n_heads, CHUNK, BAND), F32),
        name="rel_bias_expand",
    )(ext.reshape(n_heads, 1, EXT))


def _prompt_kernel(*refs, tile, seq, final_norm):
    x_ref, p_ref, bias_ref = refs[:3]
    nw = len(WEIGHT_NAMES)
    w = dict(zip(WEIGHT_NAMES, refs[3:3 + nw]))
    y_ref, ko_ref, vo_ref, convo_ref, ho_ref = refs[3 + nw:8 + nw]
    kbuf, vbuf, q_scr, attn_scr, xr_buf, h_scr = refs[8 + nw:]

    t = pl.program_id(1)
    n_pairs = kbuf.shape[1] // LANES

    @pl.when(t == 0)
    def _():
        kbuf[0:PAD, :] = jnp.zeros((PAD, kbuf.shape[1]), BF16)
        vbuf[0:PAD, :] = jnp.zeros((PAD, vbuf.shape[1]), BF16)
        xr_buf[0:SUBLANES, :] = jnp.zeros((SUBLANES, xr_buf.shape[1]), F32)
        h_scr[...] = jnp.zeros(h_scr.shape, F32)

    x = x_ref[0]
    q, k, v, xr, gr = _project(x, w)

    q_scr[...] = q.astype(BF16)
    kbuf[PAD:PAD + tile, :] = k.astype(BF16)
    vbuf[PAD:PAD + tile, :] = v.astype(BF16)

    keep_from = seq - PAD

    @pl.when(t * tile >= keep_from)
    def _():
        off = pl.multiple_of(t * tile - keep_from, tile)
        ko_ref[0, pl.ds(off, tile), :] = k
        vo_ref[0, pl.ds(off, tile), :] = v

    lane = lax.broadcasted_iota(jnp.int32, (CHUNK, LANES), 1)
    low_half = lane < HEAD_DIM
    kidx = lax.broadcasted_iota(jnp.int32, (1, BAND), 1)

    for j in range(tile // CHUNK):
        r0 = j * CHUNK
        valid = (t * tile + j * CHUNK + kidx - PAD) >= PAD
        for pr in range(n_pairs):
            lanes = slice(pr * LANES, (pr + 1) * LANES)
            q2 = q_scr[pl.ds(r0, CHUNK), lanes]
            kb = kbuf[pl.ds(r0, BAND), lanes]
            vb = vbuf[pl.ds(r0, BAND), lanes]
            outs = []
            for e in range(2):
                qm = jnp.where(low_half if e == 0 else jnp.logical_not(low_half), q2,
                               jnp.zeros_like(q2))
                s = _dot_nt(qm, kb) + bias_ref[2 * pr + e]
                s = jnp.where(valid, s, NEG)
                outs.append(_softmax_pv([s], [vb]))
            attn_scr[pl.ds(r0, CHUNK), lanes] = jnp.where(low_half, outs[0], outs[1])

    for r in range(0, PAD, tile):
        n = min(tile, PAD - r)
        kbuf[r:r + n, :] = kbuf[r + tile:r + tile + n, :]
        vbuf[r:r + n, :] = vbuf[r + tile:r + tile + n, :]

    xr_buf[SUBLANES:SUBLANES + tile, :] = xr
    xc = _conv(xr_buf, SUBLANES, tile, w)
    h0_rows = jnp.broadcast_to(h_scr[...], xc.shape)
    hs, lru_out = _rglru(xc, gr, h0_rows, tile, w["w_r"][...], w["b_r"][...],
                         w["w_i"][...], w["b_i"][...], w["lam"][...])
    h_scr[...] = hs[tile - 1:tile, :]

    @pl.when(t == pl.num_programs(1) - 1)
    def _():
        convo_ref[0] = xr_buf[SUBLANES + tile - (CONV_WIDTH - 1):SUBLANES + tile, :]
        ho_ref[0] = hs[tile - 1:tile, :]

    xr_buf[0:SUBLANES, :] = xr_buf[tile:tile + SUBLANES, :]

    y_ref[0] = _dense_tail(x, attn_scr[...], lru_out, p_ref[0], w, final_norm)


def _const_spec(arr):
    nd = arr.ndim
    return pl.BlockSpec(arr.shape, lambda *_: (0,) * nd, pipeline_mode=pl.Buffered(1))


def _prompt_layer(x, p, bias, weights, final_norm, tile):
    b, seq, d = x.shape
    aw = weights["g_attn_out"].shape[1]
    lw = weights["g_lru_out"].shape[1]
    assert seq % tile == 0 and PAD % tile == 0 and seq >= PAD and tile % CHUNK == 0
    assert tile & (tile - 1) == 0 and aw % LANES == 0
    wlist = [weights[n] for n in WEIGHT_NAMES]
    kern = functools.partial(_prompt_kernel, tile=tile, seq=seq, final_norm=final_norm)
    out_shape = (
        jax.ShapeDtypeStruct((b, seq, d), F32),
        jax.ShapeDtypeStruct((b, PAD, aw), F32),
        jax.ShapeDtypeStruct((b, PAD, aw), F32),
        jax.ShapeDtypeStruct((b, CONV_WIDTH - 1, lw), F32),
        jax.ShapeDtypeStruct((b, 1, lw), F32),
    )
    in_specs = [
        pl.BlockSpec((1, tile, d), lambda i, j: (i, j, 0)),
        pl.BlockSpec((1, tile, p.shape[-1]), lambda i, j: (i, j, 0)),
        _const_spec(bias),
    ] + [_const_spec(a) for a in wlist]
    out_specs = (
        pl.BlockSpec((1, tile, d), lambda i, j: (i, j, 0)),
        pl.BlockSpec((1, PAD, aw), lambda i, j: (i, 0, 0)),
        pl.BlockSpec((1, PAD, aw), lambda i, j: (i, 0, 0)),
        pl.BlockSpec((1, CONV_WIDTH - 1, lw), lambda i, j: (i, 0, 0)),
        pl.BlockSpec((1, 1, lw), lambda i, j: (i, 0, 0)),
    )
    scratch = [
        pltpu.VMEM((PAD + tile, aw), BF16),
        pltpu.VMEM((PAD + tile, aw), BF16),
        pltpu.VMEM((tile, aw), BF16),
        pltpu.VMEM((tile, aw), F32),
        pltpu.VMEM((SUBLANES + tile, lw), F32),
        pltpu.VMEM((1, lw), F32),
    ]
    return pl.pallas_call(
        kern,
        out_shape=out_shape,
        grid=(b, seq // tile),
        in_specs=in_specs,
        out_specs=out_specs,
        scratch_shapes=scratch,
        compiler_params=pltpu.CompilerParams(
            dimension_semantics=("arbitrary", "arbitrary"), vmem_limit_bytes=VMEM_LIMIT),
        name="prompt_layer",
    )(x, p, bias, *wlist)


def _sample_kernel(*refs, final_norm):
    x_ref, p_ref, ck_ref, cv_ref, sconv_ref, sh_ref, bias_ref = refs[:7]
    nw = len(WEIGHT_NAMES)
    w = dict(zip(WEIGHT_NAMES, refs[7:7 + nw]))
    y_ref, ko_ref, vo_ref, convo_ref, ho_ref = refs[7 + nw:12 + nw]
    attn_scr, xr_buf = refs[12 + nw:]

    nb, t, d = x_ref.shape
    n_cache = ck_ref.shape[1]
    aw = ck_ref.shape[2]
    n_pairs = aw // LANES
    rows = nb * t

    x = x_ref[...].reshape(rows, d)
    q, k, v, xr, gr = _project(x, w)
    ko_ref[...] = k.reshape(nb, t, aw)
    vo_ref[...] = v.reshape(nb, t, aw)
    qb = q.astype(BF16)
    kb_new = k.astype(BF16)
    vb_new = v.astype(BF16)

    lane = lax.broadcasted_iota(jnp.int32, (t, LANES), 1)
    low_half = lane < HEAD_DIM
    for b in range(nb):
        rs = slice(b * t, (b + 1) * t)
        for pr in range(n_pairs):
            lanes = slice(pr * LANES, (pr + 1) * LANES)
            q2 = qb[rs, lanes]
            kc = ck_ref[b, :, lanes].astype(BF16)
            vc = cv_ref[b, :, lanes].astype(BF16)
            kn = kb_new[rs, lanes]
            vn = vb_new[rs, lanes]
            outs = []
            for e in range(2):
                h = 2 * pr + e
                qm = jnp.where(low_half if e == 0 else jnp.logical_not(low_half), q2,
                               jnp.zeros_like(q2))
                s_c = _dot_nt(qm, kc) + bias_ref[h, 0:t, PAD - n_cache:PAD]
                s_n = _dot_nt(qm, kn) + bias_ref[h, 0:t, PAD:PAD + t]
                outs.append(_softmax_pv([s_c, s_n], [vc, vn]))
            attn_scr[rs, lanes] = jnp.where(low_half, outs[0], outs[1])

    seg_rows = SUBLANES + t
    xcs = []
    for b in range(nb):
        base = b * seg_rows
        xr_buf[base + SUBLANES - (CONV_WIDTH - 1):base + SUBLANES, :] = sconv_ref[b]
        xr_buf[base + SUBLANES:base + seg_rows, :] = xr[b * t:(b + 1) * t, :]
        xcs.append(_conv(xr_buf, base + SUBLANES, t, w))
        convo_ref[b] = xr_buf[base + seg_rows - (CONV_WIDTH - 1):base + seg_rows, :]
    xc = jnp.concatenate(xcs, axis=0)
    h0_rows = jnp.concatenate(
        [jnp.broadcast_to(sh_ref[b], (t, sh_ref.shape[2])) for b in range(nb)], axis=0)
    hs, lru_out = _rglru(xc, gr, h0_rows, t, w["w_r"][...], w["b_r"][...],
                         w["w_i"][...], w["b_i"][...], w["lam"][...])
    for b in range(nb):
        ho_ref[b] = hs[(b + 1) * t - 1:(b + 1) * t, :]

    y = _dense_tail(x, attn_scr[...], lru_out, p_ref[...].reshape(rows, p_ref.shape[2]), w,
                    final_norm)
    y_ref[...] = y.reshape(nb, t, d)


def _sample_layer(x, p, cache_k, cache_v, state_conv, state_h, bias, weights, final_norm, group):
    nbatch, t, d = x.shape
    n_cache, aw = cache_k.shape[1:]
    lw = weights["g_lru_out"].shape[1]
    assert nbatch % group == 0 and t % SUBLANES == 0 and t & (t - 1) == 0
    assert CONV_WIDTH - 1 <= t <= CHUNK and n_cache <= PAD and aw % LANES == 0
    wlist = [weights[n] for n in WEIGHT_NAMES]
    kern = functools.partial(_sample_kernel, final_norm=final_norm)
    state_h3 = state_h.reshape(nbatch, 1, lw)

    def grp(shape):
        nd = len(shape)
        return pl.BlockSpec((group,) + tuple(shape[1:]), lambda i: (i,) + (0,) * (nd - 1))

    out_shape = (
        jax.ShapeDtypeStruct((nbatch, t, d), F32),
        jax.ShapeDtypeStruct((nbatch, t, aw), F32),
        jax.ShapeDtypeStruct((nbatch, t, aw), F32),
        jax.ShapeDtypeStruct((nbatch, CONV_WIDTH - 1, lw), F32),
        jax.ShapeDtypeStruct((nbatch, 1, lw), F32),
    )
    ins = [x, p, cache_k, cache_v, state_conv, state_h3]
    in_specs = [grp(a.shape) for a in ins] + [_const_spec(bias)] + [_const_spec(a) for a in wlist]
    out_specs = tuple(grp(s.shape) for s in out_shape)
    scratch = [
        pltpu.VMEM((group * t, aw), F32),
        pltpu.VMEM((group * (SUBLANES + t), lw), F32),
    ]
    return pl.pallas_call(
        kern,
        out_shape=out_shape,
        grid=(nbatch // group,),
        in_specs=in_specs,
        out_specs=out_specs,
        scratch_shapes=scratch,
        compiler_params=pltpu.CompilerParams(
            dimension_semantics=("arbitrary",), vmem_limit_bytes=VMEM_LIMIT),
        name="sample_layer",
    )(*ins, bias, *wlist)


def _block_diag(wb):
    n, kk, jj = wb.shape
    eye = jnp.eye(n, dtype=wb.dtype)
    return (wb[:, :, None, :] * eye[:, None, :, None]).reshape(n * kk, n * jj)


PROMPT_TILE = 256
SAMPLE_GROUP = 4


def kernel(x_prompt, x_sample, p_prompt, p_sample, cache_k, cache_v, state_conv, state_h, g_mix, w_in, conv_w, conv_b, w_rgate, b_rgate, w_igate, b_igate, lru_lambda, rel_bias_table, g_attn_out, g_lru_out, w_out, g_ffn, w_ffn_gate, w_ffn_up, w_ffn_down, g_ple, w_ple_gate, w_ple_proj, g_final):
    depth = w_in.shape[0]
    hp, hs = x_prompt, x_sample
    outs = [[] for _ in range(8)]
    for i in range(depth):
        row = lambda a: a[i].reshape(1, -1).astype(F32)
        weights = {
            "g_mix": row(g_mix), "w_in": w_in[i].astype(BF16),
            "conv_w": conv_w[i].astype(F32), "conv_b": row(conv_b),
            "w_r": _block_diag(w_rgate[i]).astype(BF16), "b_r": row(b_rgate),
            "w_i": _block_diag(w_igate[i]).astype(BF16), "b_i": row(b_igate),
            "lam": row(lru_lambda),
            "g_attn_out": row(g_attn_out), "g_lru_out": row(g_lru_out),
            "w_out": w_out[i].astype(BF16), "g_ffn": row(g_ffn),
            "w_ffn_gate": w_ffn_gate[i].astype(BF16), "w_ffn_up": w_ffn_up[i].astype(BF16),
            "w_ffn_down": w_ffn_down[i].astype(BF16), "g_ple": row(g_ple),
            "w_ple_gate": w_ple_gate[i].astype(BF16), "w_ple_proj": w_ple_proj[i].astype(BF16),
            "g_final": g_final.reshape(1, -1).astype(F32),
        }
        final = i == depth - 1
        bias = _bias_table(rel_bias_table[i])
        n_heads = rel_bias_table.shape[1]
        aw = n_heads * HEAD_DIM

        hp, k1, v1, c1, r1 = _prompt_layer(hp, p_prompt[i], bias, weights, final, PROMPT_TILE)
        nb, nc = cache_k.shape[1:3]
        hs, k2, v2, c2, r2 = _sample_layer(
            hs, p_sample[i], cache_k[i].reshape(nb, nc, aw), cache_v[i].reshape(nb, nc, aw),
            state_conv[i], state_h[i], bias, weights, final, SAMPLE_GROUP)
        bp, keep = k1.shape[:2]
        ts = k2.shape[1]
        for lst, val in zip(outs, (
                k1.reshape(bp, keep, n_heads, HEAD_DIM), v1.reshape(bp, keep, n_heads, HEAD_DIM),
                c1, r1.reshape(bp, -1),
                k2.reshape(nb, ts, n_heads, HEAD_DIM), v2.reshape(nb, ts, n_heads, HEAD_DIM),
                c2, r2.reshape(nb, -1))):
            lst.append(val)
    return (hp, hs) + tuple(jnp.stack(l) for l in outs)
```

```python
import functools

import jax
import jax.numpy as jnp
from jax import lax
from jax.experimental import pallas as pl
from jax.experimental.pallas import tpu as pltpu

CHUNK = 64
LEFT_CHUNKS = 8
PAD = LEFT_CHUNKS * CHUNK
BAND = PAD + CHUNK
HEAD_DIM = 64
CONV_WIDTH = 4
LRU_C = 8.0
REL_CLIP = 128
EPS = 1e-6
NEG = -1e30
SCALE = HEAD_DIM ** -0.5
LANES = 128
SUBLANES = 8
EXT = 640
VMEM_LIMIT = 60 * 1024 * 1024

F32 = jnp.float32
BF16 = jnp.bfloat16


def _dot(a, b):
    return jnp.dot(a, b, preferred_element_type=F32)


def _dot_nt(a, b):
    return lax.dot_general(a, b, (((1,), (1,)), ((), ())), preferred_element_type=F32)


def _rms(x, g):
    ms = jnp.mean(x * x, axis=-1, keepdims=True)
    return (x * lax.rsqrt(ms + EPS)) * g


def _gelu_tanh(x):
    c = 0.7978845608028654
    return 0.5 * x * (1.0 + jnp.tanh(c * (x + 0.044715 * (x * x * x))))


def _shift_rows(x, d, fill, seg):
    rolled = pltpu.roll(x, d, axis=0)
    row = lax.broadcasted_iota(jnp.int32, x.shape, 0)
    return jnp.where((row & (seg - 1)) >= d, rolled, fill)


def _linear_scan(a, u, seg):
    d = 1
    while d < seg:
        u = u + a * _shift_rows(u, d, 0.0, seg)
        if 2 * d < seg:
            a = a * _shift_rows(a, d, 1.0, seg)
        d *= 2
    return u


def _rglru(xc, gr, h0_rows, seg, w_r, b_r, w_i, b_i, lam):
    xcb = xc.astype(BF16)
    r = jax.nn.sigmoid(_dot(xcb, w_r) + b_r)
    ig = jax.nn.sigmoid(_dot(xcb, w_i) + b_i)
    z = -lam
    softplus = jnp.maximum(z, 0.0) + jnp.log1p(jnp.exp(-jnp.abs(z)))
    log_a = (-LRU_C) * r * softplus
    a = jnp.exp(log_a)
    u = jnp.sqrt(1.0 - a * a) * (ig * xc)
    row = lax.broadcasted_iota(jnp.int32, xc.shape, 0)
    u = u + jnp.where((row & (seg - 1)) == 0, a * h0_rows, 0.0)
    hs = _linear_scan(a, u, seg)
    return hs, hs * _gelu_tanh(gr)


def _dense_tail(h, attn, lru_out, p, w, final_norm):
    mixed = jnp.concatenate(
        [_rms(attn, w["g_attn_out"][...]), _rms(lru_out, w["g_lru_out"][...])], axis=-1)
    h = h + _dot(mixed.astype(BF16), w["w_out"][...])
    hn = _rms(h, w["g_ffn"][...]).astype(BF16)
    d_ff = w["w_ffn_gate"].shape[1]
    n_split = 2 if d_ff % (2 * LANES) == 0 else 1
    step = d_ff // n_split
    acc = h
    for c in range(n_split):
        cols = slice(c * step, (c + 1) * step)
        gate = _dot(hn, w["w_ffn_gate"][:, cols])
        up = _dot(hn, w["w_ffn_up"][:, cols])
        act = (gate * jax.nn.sigmoid(gate)) * up
        acc = acc + _dot(act.astype(BF16), w["w_ffn_down"][cols, :])
    h = acc
    gate = jax.nn.sigmoid(_dot(_rms(h, w["g_ple"][...]).astype(BF16), w["w_ple_gate"][...]))
    h = h + _dot(p.astype(BF16), w["w_ple_proj"][...]) * gate
    if final_norm:
        h = _rms(h, w["g_final"][...])
    return h


def _softmax_pv(scores, values):
    m = functools.reduce(jnp.maximum, [jnp.max(s, axis=-1, keepdims=True) for s in scores])
    num = None
    den = None
    for s, v in zip(scores, values):
        e = jnp.exp(s - m)
        l = jnp.sum(e, axis=-1, keepdims=True)
        o = _dot(e.astype(BF16), v)
        num = o if num is None else num + o
        den = l if den is None else den + l
    return num / den


WEIGHT_NAMES = (
    "g_mix", "w_in", "conv_w", "conv_b", "w_r", "b_r", "w_i", "b_i", "lam",
    "g_attn_out", "g_lru_out", "w_out", "g_ffn", "w_ffn_gate", "w_ffn_up", "w_ffn_down",
    "g_ple", "w_ple_gate", "w_ple_proj", "g_final")


def _project(x, w):
    xn = _rms(x, w["g_mix"][...]).astype(BF16)
    proj = _dot(xn, w["w_in"][...])
    aw = w["g_attn_out"].shape[1]
    lw = w["g_lru_out"].shape[1]
    q = proj[:, :aw] * SCALE
    k = proj[:, aw:2 * aw]
    v = proj[:, 2 * aw:3 * aw]
    xr = proj[:, 3 * aw:3 * aw + lw]
    gr = proj[:, 3 * aw + lw:]
    return q, k, v, xr, gr


def _conv(xr_buf, base, rows, w):
    cw = w["conv_w"]
    xc = w["conv_b"][...] + xr_buf[base:base + rows, :] * cw[CONV_WIDTH - 1:CONV_WIDTH, :]
    for j in range(CONV_WIDTH - 1):
        off = base - (CONV_WIDTH - 1) + j
        xc = xc + xr_buf[off:off + rows, :] * cw[j:j + 1, :]
    return xc


def _bias_kernel(ext_ref, bias_ref):
    n_heads = ext_ref.shape[0]
    for h in range(n_heads):
        rows = jnp.broadcast_to(ext_ref[h], (CHUNK, EXT))
        rolled = pltpu.roll(rows, EXT - (CHUNK - 1), axis=1, stride=1, stride_axis=0)
        bias_ref[h] = rolled[:, :BAND]


def _bias_table(rel_table):
    n_heads, rel_size = rel_table.shape
    assert rel_size == REL_CLIP + CHUNK
    left = (BAND - 1) - REL_CLIP
    ext = jnp.pad(rel_table.astype(F32), ((0, 0), (left, EXT - left - rel_size)), mode="edge")
    return pl.pallas_call(
        _bias_kernel,
        out_shape=jax.ShapeDtypeStruct((n_heads, CHUNK, BAND), F32),
        name="rel_bias_expand",
    )(ext.reshape(n_heads, 1, EXT))


def _prompt_kernel(*refs, tile, seq, final_norm):
    x_ref, p_ref, bias_ref = refs[:3]
    nw = len(WEIGHT_NAMES)
    w = dict(zip(WEIGHT_NAMES, refs[3:3 + nw]))
    y_ref, ko_ref, vo_ref, convo_ref, ho_ref = refs[3 + nw:8 + nw]
    kbuf, vbuf, q_scr, attn_scr, xr_buf, h_scr = refs[8 + nw:]

    t = pl.program_id(1)
    n_pairs = kbuf.shape[1] // LANES

    @pl.when(t == 0)
    def _():
        kbuf[0:PAD, :] = jnp.zeros((PAD, kbuf.shape[1]), BF16)
        vbuf[0:PAD, :] = jnp.zeros((PAD, vbuf.shape[1]), BF16)
        xr_buf[0:SUBLANES, :] = jnp.zeros((SUBLANES, xr_buf.shape[1]), F32)
        h_scr[...] = jnp.zeros(h_scr.shape, F32)

    x = x_ref[0]
    q, k, v, xr, gr = _project(x, w)

    q_scr[...] = q.astype(BF16)
    kbuf[PAD:PAD + tile, :] = k.astype(BF16)
    vbuf[PAD:PAD + tile, :] = v.astype(BF16)

    keep_from = seq - PAD

    @pl.when(t * tile >= keep_from)
    def _():
        off = pl.multiple_of(t * tile - keep_from, tile)
        ko_ref[0, pl.ds(off, tile), :] = k
        vo_ref[0, pl.ds(off, tile), :] = v

    lane = lax.broadcasted_iota(jnp.int32, (CHUNK, LANES), 1)
    low_half = lane < HEAD_DIM
    kidx = lax.broadcasted_iota(jnp.int32, (1, BAND), 1)
    units = [(j, pr) for j in range(tile // CHUNK) for pr in range(n_pairs)]

    def scores(j, pr):
        lanes = slice(pr * LANES, (pr + 1) * LANES)
        q2 = q_scr[j * CHUNK:(j + 1) * CHUNK, lanes]
        zero = jnp.zeros_like(q2)
        qs = jnp.concatenate([jnp.where(low_half, q2, zero), jnp.where(low_half, zero, q2)],
                             axis=0)
        s = _dot_nt(qs, kbuf[j * CHUNK:j * CHUNK + BAND, lanes]) + bias_ref[pr]
        valid = (t * tile + j * CHUNK + kidx - PAD) >= PAD
        return jnp.where(valid, s, NEG)

    def attend(j, pr, s):
        lanes = slice(pr * LANES, (pr + 1) * LANES)
        o = _softmax_pv([s], [vbuf[j * CHUNK:j * CHUNK + BAND, lanes]])
        attn_scr[j * CHUNK:(j + 1) * CHUNK, lanes] = jnp.where(low_half, o[:CHUNK], o[CHUNK:])

    s_cur = scores(*units[0])
    for i, unit in enumerate(units):
        s_next = scores(*units[i + 1]) if i + 1 < len(units) else None
        attend(*unit, s_cur)
        s_cur = s_next

    for r in range(0, PAD, tile):
        n = min(tile, PAD - r)
        kbuf[r:r + n, :] = kbuf[r + tile:r + tile + n, :]
        vbuf[r:r + n, :] = vbuf[r + tile:r + tile + n, :]

    xr_buf[SUBLANES:SUBLANES + tile, :] = xr
    xc = _conv(xr_buf, SUBLANES, tile, w)
    h0_rows = jnp.broadcast_to(h_scr[...], xc.shape)
    hs, lru_out = _rglru(xc, gr, h0_rows, tile, w["w_r"][...], w["b_r"][...],
                         w["w_i"][...], w["b_i"][...], w["lam"][...])
    h_scr[...] = hs[tile - 1:tile, :]

    @pl.when(t == pl.num_programs(1) - 1)
    def _():
        convo_ref[0] = xr_buf[SUBLANES + tile - (CONV_WIDTH - 1):SUBLANES + tile, :]
        ho_ref[0] = hs[tile - 1:tile, :]

    xr_buf[0:SUBLANES, :] = xr_buf[tile:tile + SUBLANES, :]

    y_ref[0] = _dense_tail(x, attn_scr[...], lru_out, p_ref[0], w, final_norm)


def _const_spec(arr):
    nd = arr.ndim
    return pl.BlockSpec(arr.shape, lambda *_: (0,) * nd, pipeline_mode=pl.Buffered(1))


def _prompt_layer(x, p, bias, weights, final_norm, tile):
    b, seq, d = x.shape
    aw = weights["g_attn_out"].shape[1]
    lw = weights["g_lru_out"].shape[1]
    assert seq % tile == 0 and PAD % tile == 0 and seq >= PAD and tile % CHUNK == 0
    assert tile & (tile - 1) == 0 and aw % LANES == 0
    wlist = [weights[n] for n in WEIGHT_NAMES]
    kern = functools.partial(_prompt_kernel, tile=tile, seq=seq, final_norm=final_norm)
    out_shape = (
        jax.ShapeDtypeStruct((b, seq, d), F32),
        jax.ShapeDtypeStruct((b, PAD, aw), F32),
        jax.ShapeDtypeStruct((b, PAD, aw), F32),
        jax.ShapeDtypeStruct((b, CONV_WIDTH - 1, lw), F32),
        jax.ShapeDtypeStruct((b, 1, lw), F32),
    )
    in_specs = [
        pl.BlockSpec((1, tile, d), lambda i, j: (i, j, 0)),
        pl.BlockSpec((1, tile, p.shape[-1]), lambda i, j: (i, j, 0)),
        _const_spec(bias),
    ] + [_const_spec(a) for a in wlist]
    out_specs = (
        pl.BlockSpec((1, tile, d), lambda i, j: (i, j, 0)),
        pl.BlockSpec((1, PAD, aw), lambda i, j: (i, 0, 0)),
        pl.BlockSpec((1, PAD, aw), lambda i, j: (i, 0, 0)),
        pl.BlockSpec((1, CONV_WIDTH - 1, lw), lambda i, j: (i, 0, 0)),
        pl.BlockSpec((1, 1, lw), lambda i, j: (i, 0, 0)),
    )
    scratch = [
        pltpu.VMEM((PAD + tile, aw), BF16),
        pltpu.VMEM((PAD + tile, aw), BF16),
        pltpu.VMEM((tile, aw), BF16),
        pltpu.VMEM((tile, aw), F32),
        pltpu.VMEM((SUBLANES + tile, lw), F32),
        pltpu.VMEM((1, lw), F32),
    ]
    return pl.pallas_call(
        kern,
        out_shape=out_shape,
        grid=(b, seq // tile),
        in_specs=in_specs,
        out_specs=out_specs,
        scratch_shapes=scratch,
        compiler_params=pltpu.CompilerParams(
            dimension_semantics=("arbitrary", "arbitrary"), vmem_limit_bytes=VMEM_LIMIT),
        name="prompt_layer",
    )(x, p, bias, *wlist)


def _sample_kernel(*refs, final_norm):
    x_ref, p_ref, ck_ref, cv_ref, sconv_ref, sh_ref, bias_ref = refs[:7]
    nw = len(WEIGHT_NAMES)
    w = dict(zip(WEIGHT_NAMES, refs[7:7 + nw]))
    y_ref, ko_ref, vo_ref, convo_ref, ho_ref = refs[7 + nw:12 + nw]
    attn_scr, xr_buf = refs[12 + nw:]

    nb, t, d = x_ref.shape
    n_cache = ck_ref.shape[1]
    aw = ck_ref.shape[2]
    n_pairs = aw // LANES
    rows = nb * t

    x = x_ref[...].reshape(rows, d)
    q, k, v, xr, gr = _project(x, w)
    ko_ref[...] = k.reshape(nb, t, aw)
    vo_ref[...] = v.reshape(nb, t, aw)
    qb = q.astype(BF16)
    kb_new = k.astype(BF16)
    vb_new = v.astype(BF16)

    lane = lax.broadcasted_iota(jnp.int32, (t, LANES), 1)
    low_half = lane < HEAD_DIM
    for b in range(nb):
        rs = slice(b * t, (b + 1) * t)
        for pr in range(n_pairs):
            lanes = slice(pr * LANES, (pr + 1) * LANES)
            q2 = qb[rs, lanes]
            kc = ck_ref[b, :, lanes].astype(BF16)
            vc = cv_ref[b, :, lanes].astype(BF16)
            kn = kb_new[rs, lanes]
            vn = vb_new[rs, lanes]
            outs = []
            for e in range(2):
                h = 2 * pr + e
                qm = jnp.where(low_half if e == 0 else jnp.logical_not(low_half), q2,
                               jnp.zeros_like(q2))
                s_c = _dot_nt(qm, kc) + bias_ref[h, 0:t, PAD - n_cache:PAD]
                s_n = _dot_nt(qm, kn) + bias_ref[h, 0:t, PAD:PAD + t]
                outs.append(_softmax_pv([s_c, s_n], [vc, vn]))
            attn_scr[rs, lanes] = jnp.where(low_half, outs[0], outs[1])

    seg_rows = SUBLANES + t
    xcs = []
    for b in range(nb):
        base = b * seg_rows
        xr_buf[base + SUBLANES - (CONV_WIDTH - 1):base + SUBLANES, :] = sconv_ref[b]
        xr_buf[base + SUBLANES:base + seg_rows, :] = xr[b * t:(b + 1) * t, :]
        xcs.append(_conv(xr_buf, base + SUBLANES, t, w))
        convo_ref[b] = xr_buf[base + seg_rows - (CONV_WIDTH - 1):base + seg_rows, :]
    xc = jnp.concatenate(xcs, axis=0)
    h0_rows = jnp.concatenate(
        [jnp.broadcast_to(sh_ref[b], (t, sh_ref.shape[2])) for b in range(nb)], axis=0)
    hs, lru_out = _rglru(xc, gr, h0_rows, t, w["w_r"][...], w["b_r"][...],
                         w["w_i"][...], w["b_i"][...], w["lam"][...])
    for b in range(nb):
        ho_ref[b] = hs[(b + 1) * t - 1:(b + 1) * t, :]

    y = _dense_tail(x, attn_scr[...], lru_out, p_ref[...].reshape(rows, p_ref.shape[2]), w,
                    final_norm)
    y_ref[...] = y.reshape(nb, t, d)


def _sample_layer(x, p, cache_k, cache_v, state_conv, state_h, bias, weights, final_norm, group):
    nbatch, t, d = x.shape
    n_cache, aw = cache_k.shape[1:]
    lw = weights["g_lru_out"].shape[1]
    assert nbatch % group == 0 and t % SUBLANES == 0 and t & (t - 1) == 0
    assert CONV_WIDTH - 1 <= t <= CHUNK and n_cache <= PAD and aw % LANES == 0
    wlist = [weights[n] for n in WEIGHT_NAMES]
    kern = functools.partial(_sample_kernel, final_norm=final_norm)
    state_h3 = state_h.reshape(nbatch, 1, lw)

    def grp(shape):
        nd = len(shape)
        return pl.BlockSpec((group,) + tuple(shape[1:]), lambda i: (i,) + (0,) * (nd - 1))

    out_shape = (
        jax.ShapeDtypeStruct((nbatch, t, d), F32),
        jax.ShapeDtypeStruct((nbatch, t, aw), F32),
        jax.ShapeDtypeStruct((nbatch, t, aw), F32),
        jax.ShapeDtypeStruct((nbatch, CONV_WIDTH - 1, lw), F32),
        jax.ShapeDtypeStruct((nbatch, 1, lw), F32),
    )
    ins = [x, p, cache_k, cache_v, state_conv, state_h3]
    in_specs = [grp(a.shape) for a in ins] + [_const_spec(bias)] + [_const_spec(a) for a in wlist]
    out_specs = tuple(grp(s.shape) for s in out_shape)
    scratch = [
        pltpu.VMEM((group * t, aw), F32),
        pltpu.VMEM((group * (SUBLANES + t), lw), F32),
    ]
    return pl.pallas_call(
        kern,
        out_shape=out_shape,
        grid=(nbatch // group,),
        in_specs=in_specs,
        out_specs=out_specs,
        scratch_shapes=scratch,
        compiler_params=pltpu.CompilerParams(
            dimension_semantics=("arbitrary",), vmem_limit_bytes=VMEM_LIMIT),
        name="sample_layer",
    )(*ins, bias, *wlist)


def _block_diag(wb):
    n, kk, jj = wb.shape
    eye = jnp.eye(n, dtype=wb.dtype)
    return (wb[:, :, None, :] * eye[:, None, :, None]).reshape(n * kk, n * jj)


PROMPT_TILE = 256
SAMPLE_GROUP = 4


def kernel(x_prompt, x_sample, p_prompt, p_sample, cache_k, cache_v, state_conv, state_h, g_mix, w_in, conv_w, conv_b, w_rgate, b_rgate, w_igate, b_igate, lru_lambda, rel_bias_table, g_attn_out, g_lru_out, w_out, g_ffn, w_ffn_gate, w_ffn_up, w_ffn_down, g_ple, w_ple_gate, w_ple_proj, g_final):
    depth = w_in.shape[0]
    hp, hs = x_prompt, x_sample
    outs = [[] for _ in range(8)]
    for i in range(depth):
        row = lambda a: a[i].reshape(1, -1).astype(F32)
        weights = {
            "g_mix": row(g_mix), "w_in": w_in[i].astype(BF16),
            "conv_w": conv_w[i].astype(F32), "conv_b": row(conv_b),
            "w_r": _block_diag(w_rgate[i]).astype(BF16), "b_r": row(b_rgate),
            "w_i": _block_diag(w_igate[i]).astype(BF16), "b_i": row(b_igate),
            "lam": row(lru_lambda),
            "g_attn_out": row(g_attn_out), "g_lru_out": row(g_lru_out),
            "w_out": w_out[i].astype(BF16), "g_ffn": row(g_ffn),
            "w_ffn_gate": w_ffn_gate[i].astype(BF16), "w_ffn_up": w_ffn_up[i].astype(BF16),
            "w_ffn_down": w_ffn_down[i].astype(BF16), "g_ple": row(g_ple),
            "w_ple_gate": w_ple_gate[i].astype(BF16), "w_ple_proj": w_ple_proj[i].astype(BF16),
            "g_final": g_final.reshape(1, -1).astype(F32),
        }
        final = i == depth - 1
        bias = _bias_table(rel_bias_table[i])
        n_heads = rel_bias_table.shape[1]
        aw = n_heads * HEAD_DIM

        hp, k1, v1, c1, r1 = _prompt_layer(
            hp, p_prompt[i], bias.reshape(n_heads // 2, 2 * CHUNK, BAND), weights, final,
            PROMPT_TILE)
        nb, nc = cache_k.shape[1:3]
        hs, k2, v2, c2, r2 = _sample_layer(
            hs, p_sample[i], cache_k[i].reshape(nb, nc, aw), cache_v[i].reshape(nb, nc, aw),
            state_conv[i], state_h[i], bias, weights, final, SAMPLE_GROUP)
        bp, keep = k1.shape[:2]
        ts = k2.shape[1]
        for lst, val in zip(outs, (
                k1.reshape(bp, keep, n_heads, HEAD_DIM), v1.reshape(bp, keep, n_heads, HEAD_DIM),
                c1, r1.reshape(bp, -1),
                k2.reshape(nb, ts, n_heads, HEAD_DIM), v2.reshape(nb, ts, n_heads, HEAD_DIM),
                c2, r2.reshape(nb, -1))):
            lst.append(val)
    return (hp, hs) + tuple(jnp.stack(l) for l in outs)
```

```python
import functools

import jax
import jax.numpy as jnp
from jax import lax
from jax.experimental import pallas as pl
from jax.experimental.pallas import tpu as pltpu

CHUNK = 64
LEFT_CHUNKS = 8
PAD = LEFT_CHUNKS * CHUNK
BAND = PAD + CHUNK
HEAD_DIM = 64
CONV_WIDTH = 4
LRU_C = 8.0
REL_CLIP = 128
EPS = 1e-6
NEG = -1e30
SCALE = HEAD_DIM ** -0.5
LOG2E = 1.4426950408889634
LANES = 128
SUBLANES = 8
MXU_DIM = 256
EXT = 640
VMEM_LIMIT = 60 * 1024 * 1024
FFN_CHUNK = 2 * MXU_DIM

F32 = jnp.float32
BF16 = jnp.bfloat16


def _dot(a, b):
    return jnp.dot(a, b, preferred_element_type=F32)


def _dot_nt(a, b):
    return lax.dot_general(a, b, (((1,), (1,)), ((), ())), preferred_element_type=F32)


def _rms(x, g):
    ms = jnp.mean(x * x, axis=-1, keepdims=True)
    return (x * lax.rsqrt(ms + EPS)) * g


def _sigmoid(x):
    return 1.0 / (1.0 + jnp.exp2(x * (-LOG2E)))


def _gelu_tanh(x):
    c = 0.7978845608028654
    return 0.5 * x * (1.0 + jnp.tanh(c * (x + 0.044715 * (x * x * x))))


def _shift_rows(x, d, fill, seg):
    rows = x.shape[0]
    if seg == rows and d % SUBLANES == 0:
        return jnp.concatenate([jnp.full((d, x.shape[1]), fill, x.dtype), x[:rows - d]], axis=0)
    rolled = pltpu.roll(x, d, axis=0)
    row = lax.broadcasted_iota(jnp.int32, x.shape, 0)
    return jnp.where((row & (seg - 1)) >= d, rolled, fill)


def _scan_distances(seg):
    return [1 << b for b in range(seg.bit_length() - 1)]


def _scan_step(a, u, d, seg):
    u = u + a * _shift_rows(u, d, 0.0, seg)
    if 2 * d < seg:
        a = a * _shift_rows(a, d, 1.0, seg)
    return a, u


def _sqrt_nonneg(y):
    return jnp.where(y > 0.0, y * lax.rsqrt(y), 0.0)


def _lru_inputs(xc, r_pre, i_pre, h0_rows, seg, w):
    r = _sigmoid(r_pre + w["b_r"][...])
    ig = _sigmoid(i_pre + w["b_i"][...])
    z = -w["lam"][...]
    softplus = jnp.maximum(z, 0.0) + jnp.log1p(jnp.exp(-jnp.abs(z)))
    log_a = (-LRU_C) * r * softplus
    a = jnp.exp(log_a)
    u = _sqrt_nonneg(1.0 - a * a) * (ig * xc)
    if seg == xc.shape[0]:
        row = lax.broadcasted_iota(jnp.int32, (SUBLANES, xc.shape[1]), 0)
        head = u[:SUBLANES] + jnp.where(row == 0, a[:SUBLANES] * h0_rows[:SUBLANES], 0.0)
        u = jnp.concatenate([head, u[SUBLANES:]], axis=0)
    else:
        row = lax.broadcasted_iota(jnp.int32, xc.shape, 0)
        u = u + jnp.where((row & (seg - 1)) == 0, a * h0_rows, 0.0)
    return a, u


def _rglru(xc, gr, h0_rows, seg, w):
    xcb = xc.astype(BF16)
    a, u = _lru_inputs(xc, _dot(xcb, w["w_r"][...]), _dot(xcb, w["w_i"][...]), h0_rows, seg, w)
    for d in _scan_distances(seg):
        a, u = _scan_step(a, u, d, seg)
    return u, u * _gelu_tanh(gr)


def _mix_out(h, attn, lru_out, w):
    mixed = jnp.concatenate(
        [_rms(attn, w["g_attn_out"][...]), _rms(lru_out, w["g_lru_out"][...])], axis=-1)
    return h + _dot(mixed.astype(BF16), w["w_out"][...])


def _ffn_piece(hn, acc, w, c0, c1):
    gate = _dot(hn, w["w_ffn_gate"][:, c0:c1])
    up = _dot(hn, w["w_ffn_up"][:, c0:c1])
    act = (gate * _sigmoid(gate)) * up
    return acc + _dot(act.astype(BF16), w["w_ffn_down"][c0:c1, :])


def _ffn_bounds(d_ff, step):
    return [(c, min(c + step, d_ff)) for c in range(0, d_ff, step)]


def _ple_out(h, p, w, final_norm):
    gate = _sigmoid(_dot(_rms(h, w["g_ple"][...]).astype(BF16), w["w_ple_gate"][...]))
    h = h + _dot(p.astype(BF16), w["w_ple_proj"][...]) * gate
    if final_norm:
        h = _rms(h, w["g_final"][...])
    return h


def _dense_tail(h, attn, lru_out, p, w, final_norm):
    h = _mix_out(h, attn, lru_out, w)
    hn = _rms(h, w["g_ffn"][...]).astype(BF16)
    d_ff = w["w_ffn_gate"].shape[1]
    for c0, c1 in _ffn_bounds(d_ff, -(-d_ff // (2 * LANES)) * LANES):
        h = _ffn_piece(hn, h, w, c0, c1)
    return _ple_out(h, p, w, final_norm)


def _softmax_pv(scores, values):
    m = functools.reduce(jnp.maximum, [jnp.max(s, axis=-1, keepdims=True) for s in scores])
    num = None
    den = None
    for s, v in zip(scores, values):
        e = jnp.exp2(s - m)
        l = jnp.sum(e, axis=-1, keepdims=True)
        o = _dot(e.astype(BF16), v)
        num = o if num is None else num + o
        den = l if den is None else den + l
    return num / den


WEIGHT_NAMES = (
    "g_mix", "w_in", "conv_w", "conv_b", "w_r", "b_r", "w_i", "b_i", "lam",
    "g_attn_out", "g_lru_out", "w_out", "g_ffn", "w_ffn_gate", "w_ffn_up", "w_ffn_down",
    "g_ple", "w_ple_gate", "w_ple_proj", "g_final")


def _project(x, w):
    xn = _rms(x, w["g_mix"][...]).astype(BF16)
    proj = _dot(xn, w["w_in"][...])
    aw = w["g_attn_out"].shape[1]
    lw = w["g_lru_out"].shape[1]
    q = proj[:, :aw] * (SCALE * LOG2E)
    k = proj[:, aw:2 * aw]
    v = proj[:, 2 * aw:3 * aw]
    xr = proj[:, 3 * aw:3 * aw + lw]
    gr = proj[:, 3 * aw + lw:]
    return q, k, v, xr, gr


def _conv(xr_buf, base, rows, w):
    cw = w["conv_w"]
    ext = xr_buf[base - SUBLANES:base + rows, :]
    xc = w["conv_b"][...] + ext[SUBLANES:] * cw[CONV_WIDTH - 1:CONV_WIDTH, :]
    for back in range(1, CONV_WIDTH):
        j = CONV_WIDTH - 1 - back
        xc = xc + pltpu.roll(ext, back, axis=0)[SUBLANES:] * cw[j:j + 1, :]
    return xc


def _bias_kernel(ext_ref, bias_ref):
    n_heads = ext_ref.shape[0]
    for h in range(n_heads):
        rows = jnp.broadcast_to(ext_ref[h], (CHUNK, EXT))
        rolled = pltpu.roll(rows, EXT - (CHUNK - 1), axis=1, stride=1, stride_axis=0)
        bias_ref[h] = rolled[:, :BAND] * LOG2E


def _bias_table(rel_table):
    n_heads, rel_size = rel_table.shape
    assert rel_size == REL_CLIP + CHUNK
    left = (BAND - 1) - REL_CLIP
    ext = jnp.pad(rel_table.astype(F32), ((0, 0), (left, EXT - left - rel_size)), mode="edge")
    return pl.pallas_call(
        _bias_kernel,
        out_shape=jax.ShapeDtypeStruct((n_heads, CHUNK, BAND), F32),
        name="rel_bias_expand",
    )(ext.reshape(n_heads, 1, EXT))


def _prompt_kernel(*refs, tile, seq, final_norm):
    x_ref, xprev_ref, pprev_ref, bias_ref = refs[:4]
    nw = len(WEIGHT_NAMES)
    w = dict(zip(WEIGHT_NAMES, refs[4:4 + nw]))
    y_ref, ko_ref, vo_ref, convo_ref, ho_ref = refs[4 + nw:9 + nw]
    kbuf, vbuf, q_scr, attn_scr, lru_scr, xr_buf, h_scr = refs[9 + nw:]

    t = pl.program_id(1)
    n_tiles = seq // tile
    n_pairs = kbuf.shape[1] // LANES

    @pl.when(t == 0)
    def _():
        kbuf[0:PAD, :] = jnp.zeros((PAD, kbuf.shape[1]), BF16)
        vbuf[0:PAD, :] = jnp.zeros((PAD, vbuf.shape[1]), BF16)
        xr_buf[0:SUBLANES, :] = jnp.zeros((SUBLANES, xr_buf.shape[1]), F32)
        h_scr[...] = jnp.zeros(h_scr.shape, F32)
        attn_scr[...] = jnp.zeros(attn_scr.shape, F32)
        lru_scr[...] = jnp.zeros(lru_scr.shape, F32)

    @pl.when(t == n_tiles)
    def _():
        convo_ref[0] = xr_buf[SUBLANES - (CONV_WIDTH - 1):SUBLANES, :]
        ho_ref[0] = h_scr[...]

    st = {}

    st["h"] = _mix_out(xprev_ref[0], attn_scr[...], lru_scr[...], w)

    q, k, v, xr, gr = _project(x_ref[0], w)
    q_scr[...] = q.astype(BF16)
    kbuf[PAD:PAD + tile, :] = k.astype(BF16)
    vbuf[PAD:PAD + tile, :] = v.astype(BF16)

    keep_from = seq - PAD

    @pl.when(jnp.logical_and(t < n_tiles, t * tile >= keep_from))
    def _():
        off = pl.multiple_of(t * tile - keep_from, tile)
        ko_ref[0, pl.ds(off, tile), :] = k
        vo_ref[0, pl.ds(off, tile), :] = v

    rows_w = tile // 16
    dense = []

    xr_buf[SUBLANES:SUBLANES + tile, :] = xr
    xc = _conv(xr_buf, SUBLANES, tile, w)
    xcb = xc.astype(BF16)
    r_pre = _dot(xcb, w["w_r"][...])
    i_pre = _dot(xcb, w["w_i"][...])

    def lru_inputs():
        st["a"], st["u"] = _lru_inputs(xc, r_pre, i_pre, jnp.broadcast_to(h_scr[...], xc.shape),
                                       tile, w)

    def lru_step(d):
        st["a"], st["u"] = _scan_step(st["a"], st["u"], d, tile)

    def lru_store():
        hs = st.pop("u")
        h_scr[...] = hs[tile - 1:tile, :]
        lru_scr[...] = hs * _gelu_tanh(gr)

    vector_pieces = [lru_inputs] + [functools.partial(lru_step, d) for d in _scan_distances(tile)]
    vector_pieces += [lru_store]

    def ffn_norm():
        st["hn"] = _rms(st["h"], w["g_ffn"][...]).astype(BF16)

    def ffn_gate(c0, c1):
        st["g"] = _dot(st["hn"], w["w_ffn_gate"][:, c0:c1])

    def ffn_up(c0, c1):
        g = st.pop("g")
        st["act"] = ((g * _sigmoid(g)) * _dot(st["hn"], w["w_ffn_up"][:, c0:c1])).astype(BF16)

    def ffn_down(c0, c1):
        st["h"] = st["h"] + _dot(st.pop("act"), w["w_ffn_down"][c0:c1, :])

    d_model = x_ref.shape[2]
    d_ff = w["w_ffn_gate"].shape[1]
    dense += [(0, ffn_norm)]
    for c0, c1 in _ffn_bounds(d_ff, FFN_CHUNK):
        wt = rows_w * (d_model // MXU_DIM) * -(-(c1 - c0) // MXU_DIM)
        dense += [(wt, functools.partial(ffn_gate, c0, c1)), (wt, functools.partial(ffn_up, c0, c1)),
                  (wt, functools.partial(ffn_down, c0, c1))]

    def ple_norm():
        st["hpn"] = _rms(st["h"], w["g_ple"][...]).astype(BF16)
        st["pg"] = []

    def ple_gate(c0, c1):
        st["pg"].append(_sigmoid(_dot(st["hpn"], w["w_ple_gate"][:, c0:c1])))

    def ple_out():
        h = st["h"] + _dot(pprev_ref[0].astype(BF16), w["w_ple_proj"][...]) * jnp.concatenate(
            st.pop("pg"), axis=-1)
        if final_norm:
            h = _rms(h, w["g_final"][...])
        y_ref[0] = h

    dense += [(0, ple_norm)]
    for c0, c1 in _ffn_bounds(d_model, MXU_DIM):
        dense += [(rows_w * (d_model // MXU_DIM), functools.partial(ple_gate, c0, c1))]
    dense += [(rows_w * (d_model // MXU_DIM), ple_out)]

    lane = lax.broadcasted_iota(jnp.int32, (CHUNK, LANES), 1)
    low_half = lane < HEAD_DIM
    kidx = lax.broadcasted_iota(jnp.int32, (1, BAND), 1)
    units = [(j, pr) for j in range(tile // CHUNK) for pr in range(n_pairs)]

    def scores(j, pr):
        lanes = slice(pr * LANES, (pr + 1) * LANES)
        q2 = q_scr[j * CHUNK:(j + 1) * CHUNK, lanes]
        zero = jnp.zeros_like(q2)
        qs = jnp.concatenate([jnp.where(low_half, q2, zero), jnp.where(low_half, zero, q2)],
                             axis=0)
        s = _dot_nt(qs, kbuf[j * CHUNK:j * CHUNK + BAND, lanes]) + bias_ref[pr]
        valid = (t * tile + j * CHUNK + kidx - PAD) >= PAD
        return jnp.where(valid, s, NEG)

    def attend(j, pr, s):
        lanes = slice(pr * LANES, (pr + 1) * LANES)
        o = _softmax_pv([s], [vbuf[j * CHUNK:j * CHUNK + BAND, lanes]])
        attn_scr[j * CHUNK:(j + 1) * CHUNK, lanes] = jnp.where(low_half, o[:CHUNK], o[CHUNK:])

    total_w = sum(wt for wt, _ in dense)
    done_w = 0
    pending = list(dense)
    s_cur = scores(*units[0])
    for i, unit in enumerate(units):
        s_next = scores(*units[i + 1]) if i + 1 < len(units) else None
        while pending and (done_w < total_w * (i + 1) // len(units) or i + 1 == len(units)):
            wt, piece = pending.pop(0)
            piece()
            done_w += wt
        attend(*unit, s_cur)
        s_cur = s_next
        while vector_pieces and (len(vector_pieces) > len(units) - 1 - i):
            vector_pieces.pop(0)()
        if vector_pieces:
            vector_pieces.pop(0)()

    for r in range(0, PAD, tile):
        n = min(tile, PAD - r)
        kbuf[r:r + n, :] = kbuf[r + tile:r + tile + n, :]
        vbuf[r:r + n, :] = vbuf[r + tile:r + tile + n, :]
    xr_buf[0:SUBLANES, :] = xr_buf[tile:tile + SUBLANES, :]


def _const_spec(arr):
    nd = arr.ndim
    return pl.BlockSpec(arr.shape, lambda *_: (0,) * nd, pipeline_mode=pl.Buffered(1))


def _prompt_layer(x, p, bias, weights, final_norm, tile):
    b, seq, d = x.shape
    aw = weights["g_attn_out"].shape[1]
    lw = weights["g_lru_out"].shape[1]
    assert seq % tile == 0 and PAD % tile == 0 and seq >= PAD and tile % CHUNK == 0
    assert tile & (tile - 1) == 0 and aw % LANES == 0
    n_tiles = seq // tile
    wlist = [weights[n] for n in WEIGHT_NAMES]
    kern = functools.partial(_prompt_kernel, tile=tile, seq=seq, final_norm=final_norm)
    out_shape = (
        jax.ShapeDtypeStruct((b, seq, d), F32),
        jax.ShapeDtypeStruct((b, PAD, aw), F32),
        jax.ShapeDtypeStruct((b, PAD, aw), F32),
        jax.ShapeDtypeStruct((b, CONV_WIDTH - 1, lw), F32),
        jax.ShapeDtypeStruct((b, 1, lw), F32),
    )
    cur = lambda i, j: (i, jnp.minimum(j, n_tiles - 1), 0)
    prev = lambda i, j: (i, jnp.maximum(j - 1, 0), 0)
    in_specs = [
        pl.BlockSpec((1, tile, d), cur),
        pl.BlockSpec((1, tile, d), prev),
        pl.BlockSpec((1, tile, p.shape[-1]), prev),
        _const_spec(bias),
    ] + [_const_spec(a) for a in wlist]
    out_specs = (
        pl.BlockSpec((1, tile, d), prev),
        pl.BlockSpec((1, PAD, aw), lambda i, j: (i, 0, 0)),
        pl.BlockSpec((1, PAD, aw), lambda i, j: (i, 0, 0)),
        pl.BlockSpec((1, CONV_WIDTH - 1, lw), lambda i, j: (i, 0, 0)),
        pl.BlockSpec((1, 1, lw), lambda i, j: (i, 0, 0)),
    )
    scratch = [
        pltpu.VMEM((PAD + tile, aw), BF16),
        pltpu.VMEM((PAD + tile, aw), BF16),
        pltpu.VMEM((tile, aw), BF16),
        pltpu.VMEM((tile, aw), F32),
        pltpu.VMEM((tile, lw), F32),
        pltpu.VMEM((SUBLANES + tile, lw), F32),
        pltpu.VMEM((1, lw), F32),
    ]
    return pl.pallas_call(
        kern,
        out_shape=out_shape,
        grid=(b, n_tiles + 1),
        in_specs=in_specs,
        out_specs=out_specs,
        scratch_shapes=scratch,
        compiler_params=pltpu.CompilerParams(
            dimension_semantics=("arbitrary", "arbitrary"), vmem_limit_bytes=VMEM_LIMIT),
        name="prompt_layer",
    )(x, x, p, bias, *wlist)


def _sample_kernel(*refs, final_norm):
    x_ref, p_ref, ck_ref, cv_ref, sconv_ref, sh_ref, bias_ref = refs[:7]
    nw = len(WEIGHT_NAMES)
    w = dict(zip(WEIGHT_NAMES, refs[7:7 + nw]))
    y_ref, ko_ref, vo_ref, convo_ref, ho_ref = refs[7 + nw:12 + nw]
    attn_scr, xr_buf = refs[12 + nw:]

    nb, t, d = x_ref.shape
    n_cache = ck_ref.shape[1]
    aw = ck_ref.shape[2]
    n_pairs = aw // LANES
    rows = nb * t

    x = x_ref[...].reshape(rows, d)
    q, k, v, xr, gr = _project(x, w)
    ko_ref[...] = k.reshape(nb, t, aw)
    vo_ref[...] = v.reshape(nb, t, aw)
    qb = q.astype(BF16)
    kb_new = k.astype(BF16)
    vb_new = v.astype(BF16)

    lane = lax.broadcasted_iota(jnp.int32, (t, LANES), 1)
    low_half = lane < HEAD_DIM
    for b in range(nb):
        rs = slice(b * t, (b + 1) * t)
        for pr in range(n_pairs):
            lanes = slice(pr * LANES, (pr + 1) * LANES)
            q2 = qb[rs, lanes]
            kc = ck_ref[b, :, lanes].astype(BF16)
            vc = cv_ref[b, :, lanes].astype(BF16)
            kn = kb_new[rs, lanes]
            vn = vb_new[rs, lanes]
            outs = []
            for e in range(2):
                h = 2 * pr + e
                qm = jnp.where(low_half if e == 0 else jnp.logical_not(low_half), q2,
                               jnp.zeros_like(q2))
                s_c = _dot_nt(qm, kc) + bias_ref[h, 0:t, PAD - n_cache:PAD]
                s_n = _dot_nt(qm, kn) + bias_ref[h, 0:t, PAD:PAD + t]
                outs.append(_softmax_pv([s_c, s_n], [vc, vn]))
            attn_scr[rs, lanes] = jnp.where(low_half, outs[0], outs[1])

    seg_rows = SUBLANES + t
    xcs = []
    for b in range(nb):
        base = b * seg_rows
        xr_buf[base + SUBLANES - (CONV_WIDTH - 1):base + SUBLANES, :] = sconv_ref[b]
        xr_buf[base + SUBLANES:base + seg_rows, :] = xr[b * t:(b + 1) * t, :]
        xcs.append(_conv(xr_buf, base + SUBLANES, t, w))
        convo_ref[b] = xr_buf[base + seg_rows - (CONV_WIDTH - 1):base + seg_rows, :]
    xc = jnp.concatenate(xcs, axis=0)
    h0_rows = jnp.concatenate(
        [jnp.broadcast_to(sh_ref[b], (t, sh_ref.shape[2])) for b in range(nb)], axis=0)
    hs, lru_out = _rglru(xc, gr, h0_rows, t, w)
    for b in range(nb):
        ho_ref[b] = hs[(b + 1) * t - 1:(b + 1) * t, :]

    y = _dense_tail(x, attn_scr[...], lru_out, p_ref[...].reshape(rows, p_ref.shape[2]), w,
                    final_norm)
    y_ref[...] = y.reshape(nb, t, d)


def _sample_layer(x, p, cache_k, cache_v, state_conv, state_h, bias, weights, final_norm, group):
    nbatch, t, d = x.shape
    n_cache, aw = cache_k.shape[1:]
    lw = weights["g_lru_out"].shape[1]
    assert nbatch % group == 0 and t % SUBLANES == 0 and t & (t - 1) == 0
    assert CONV_WIDTH - 1 <= t <= CHUNK and n_cache <= PAD and aw % LANES == 0
    wlist = [weights[n] for n in WEIGHT_NAMES]
    kern = functools.partial(_sample_kernel, final_norm=final_norm)
    state_h3 = state_h.reshape(nbatch, 1, lw)

    def grp(shape):
        nd = len(shape)
        return pl.BlockSpec((group,) + tuple(shape[1:]), lambda i: (i,) + (0,) * (nd - 1))

    out_shape = (
        jax.ShapeDtypeStruct((nbatch, t, d), F32),
        jax.ShapeDtypeStruct((nbatch, t, aw), F32),
        jax.ShapeDtypeStruct((nbatch, t, aw), F32),
        jax.ShapeDtypeStruct((nbatch, CONV_WIDTH - 1, lw), F32),
        jax.ShapeDtypeStruct((nbatch, 1, lw), F32),
    )
    ins = [x, p, cache_k, cache_v, state_conv, state_h3]
    in_specs = [grp(a.shape) for a in ins] + [_const_spec(bias)] + [_const_spec(a) for a in wlist]
    out_specs = tuple(grp(s.shape) for s in out_shape)
    scratch = [
        pltpu.VMEM((group * t, aw), F32),
        pltpu.VMEM((group * (SUBLANES + t), lw), F32),
    ]
    return pl.pallas_call(
        kern,
        out_shape=out_shape,
        grid=(nbatch // group,),
        in_specs=in_specs,
        out_specs=out_specs,
        scratch_shapes=scratch,
        compiler_params=pltpu.CompilerParams(
            dimension_semantics=("arbitrary",), vmem_limit_bytes=VMEM_LIMIT),
        name="sample_layer",
    )(*ins, bias, *wlist)


def _block_diag(wb):
    n, kk, jj = wb.shape
    eye = jnp.eye(n, dtype=wb.dtype)
    return (wb[:, :, None, :] * eye[:, None, :, None]).reshape(n * kk, n * jj)


PROMPT_TILE = 256
SAMPLE_GROUP = 4


def kernel(x_prompt, x_sample, p_prompt, p_sample, cache_k, cache_v, state_conv, state_h, g_mix, w_in, conv_w, conv_b, w_rgate, b_rgate, w_igate, b_igate, lru_lambda, rel_bias_table, g_attn_out, g_lru_out, w_out, g_ffn, w_ffn_gate, w_ffn_up, w_ffn_down, g_ple, w_ple_gate, w_ple_proj, g_final):
    depth = w_in.shape[0]
    hp, hs = x_prompt, x_sample
    outs = [[] for _ in range(8)]
    for i in range(depth):
        row = lambda a: a[i].reshape(1, -1).astype(F32)
        weights = {
            "g_mix": row(g_mix), "w_in": w_in[i].astype(BF16),
            "conv_w": conv_w[i].astype(F32), "conv_b": row(conv_b),
            "w_r": _block_diag(w_rgate[i]).astype(BF16), "b_r": row(b_rgate),
            "w_i": _block_diag(w_igate[i]).astype(BF16), "b_i": row(b_igate),
            "lam": row(lru_lambda),
            "g_attn_out": row(g_attn_out), "g_lru_out": row(g_lru_out),
            "w_out": w_out[i].astype(BF16), "g_ffn": row(g_ffn),
            "w_ffn_gate": w_ffn_gate[i].astype(BF16), "w_ffn_up": w_ffn_up[i].astype(BF16),
            "w_ffn_down": w_ffn_down[i].astype(BF16), "g_ple": row(g_ple),
            "w_ple_gate": w_ple_gate[i].astype(BF16), "w_ple_proj": w_ple_proj[i].astype(BF16),
            "g_final": g_final.reshape(1, -1).astype(F32),
        }
        final = i == depth - 1
        bias = _bias_table(rel_bias_table[i])
        n_heads = rel_bias_table.shape[1]
        aw = n_heads * HEAD_DIM

        hp, k1, v1, c1, r1 = _prompt_layer(
            hp, p_prompt[i], bias.reshape(n_heads // 2, 2 * CHUNK, BAND), weights, final,
            PROMPT_TILE)
        nb, nc = cache_k.shape[1:3]
        hs, k2, v2, c2, r2 = _sample_layer(
            hs, p_sample[i], cache_k[i].reshape(nb, nc, aw), cache_v[i].reshape(nb, nc, aw),
            state_conv[i], state_h[i], bias, weights, final, SAMPLE_GROUP)
        bp, keep = k1.shape[:2]
        ts = k2.shape[1]
        for lst, val in zip(outs, (
                k1.reshape(bp, keep, n_heads, HEAD_DIM), v1.reshape(bp, keep, n_heads, HEAD_DIM),
                c1, r1.reshape(bp, -1),
                k2.reshape(nb, ts, n_heads, HEAD_DIM), v2.reshape(nb, ts, n_heads, HEAD_DIM),
                c2, r2.reshape(nb, -1))):
            lst.append(val)
    return (hp, hs) + tuple(jnp.stack(l) for l in outs)
```

```python
import functools

import jax
import jax.numpy as jnp
from jax import lax
from jax.experimental import pallas as pl
from jax.experimental.pallas import tpu as pltpu

CHUNK = 64
LEFT_CHUNKS = 8
PAD = LEFT_CHUNKS * CHUNK
BAND = PAD + CHUNK
HEAD_DIM = 64
CONV_WIDTH = 4
LRU_C = 8.0
REL_CLIP = 128
EPS = 1e-6
NEG = -1e30
SCALE = HEAD_DIM ** -0.5
LOG2E = 1.4426950408889634
LANES = 128
SUBLANES = 8
MXU_DIM = 256
EXT = 640
VMEM_LIMIT = 60 * 1024 * 1024
FFN_CHUNK = 2 * MXU_DIM

F32 = jnp.float32
BF16 = jnp.bfloat16


def _dot(a, b):
    return jnp.dot(a, b, preferred_element_type=F32)


def _dot_nt(a, b):
    return lax.dot_general(a, b, (((1,), (1,)), ((), ())), preferred_element_type=F32)


def _rms(x, g):
    ms = jnp.mean(x * x, axis=-1, keepdims=True)
    return (x * lax.rsqrt(ms + EPS)) * g


def _sigmoid(x):
    return 1.0 / (1.0 + jnp.exp2(x * (-LOG2E)))


def _gelu_tanh(x):
    c = 0.7978845608028654
    return 0.5 * x * (1.0 + jnp.tanh(c * (x + 0.044715 * (x * x * x))))


def _shift_rows(x, d, fill, seg):
    rows = x.shape[0]
    if seg == rows and d % SUBLANES == 0:
        return jnp.concatenate([jnp.full((d, x.shape[1]), fill, x.dtype), x[:rows - d]], axis=0)
    rolled = pltpu.roll(x, d, axis=0)
    row = lax.broadcasted_iota(jnp.int32, x.shape, 0)
    return jnp.where((row & (seg - 1)) >= d, rolled, fill)


def _scan_distances(seg):
    return [1 << b for b in range(seg.bit_length() - 1)]


def _scan_step(a, u, d, seg):
    u = u + a * _shift_rows(u, d, 0.0, seg)
    if 2 * d < seg:
        a = a * _shift_rows(a, d, 1.0, seg)
    return a, u


def _sqrt_nonneg(y):
    return jnp.where(y > 0.0, y * lax.rsqrt(y), 0.0)


def _lru_inputs(xc, r_pre, i_pre, w):
    r = _sigmoid(r_pre + w["b_r"][...])
    ig = _sigmoid(i_pre + w["b_i"][...])
    z = -w["lam"][...]
    softplus = jnp.maximum(z, 0.0) + jnp.log1p(jnp.exp(-jnp.abs(z)))
    a = jnp.exp2(r * ((-LRU_C * LOG2E) * softplus))
    u = _sqrt_nonneg(1.0 - a * a) * (ig * xc)
    return a, u


def _rglru(xc, gr, h0_rows, seg, w):
    xcb = xc.astype(BF16)
    a, u = _lru_inputs(xc, _dot(xcb, w["w_r"][...]), _dot(xcb, w["w_i"][...]), w)
    row = lax.broadcasted_iota(jnp.int32, xc.shape, 0)
    u = u + jnp.where((row & (seg - 1)) == 0, a * h0_rows, 0.0)
    for d in _scan_distances(seg):
        a, u = _scan_step(a, u, d, seg)
    return u, u * _gelu_tanh(gr)


def _scan_blocked(a, u, h0):
    for d in _scan_distances(SUBLANES):
        u = u + a * _shift_rows(u, d, 0.0, SUBLANES)
        a = a * _shift_rows(a, d, 1.0, SUBLANES)
    carry = jnp.broadcast_to(h0, (SUBLANES, a.shape[1]))
    groups = []
    for g in range(a.shape[0] // SUBLANES):
        rows = slice(g * SUBLANES, (g + 1) * SUBLANES)
        h = u[rows] + a[rows] * carry
        groups.append(h)
        carry = jnp.broadcast_to(h[SUBLANES - 1:SUBLANES], h.shape)
    return jnp.concatenate(groups, axis=0)


def _mix_out(h, attn, lru_out, w):
    mixed = jnp.concatenate(
        [_rms(attn, w["g_attn_out"][...]), _rms(lru_out, w["g_lru_out"][...])], axis=-1)
    return h + _dot(mixed.astype(BF16), w["w_out"][...])


def _ffn_piece(hn, acc, w, c0, c1):
    gate = _dot(hn, w["w_ffn_gate"][:, c0:c1])
    up = _dot(hn, w["w_ffn_up"][:, c0:c1])
    act = (gate * _sigmoid(gate)) * up
    return acc + _dot(act.astype(BF16), w["w_ffn_down"][c0:c1, :])


def _ffn_bounds(d_ff, step):
    return [(c, min(c + step, d_ff)) for c in range(0, d_ff, step)]


def _ple_out(h, p, w, final_norm):
    gate = _sigmoid(_dot(_rms(h, w["g_ple"][...]).astype(BF16), w["w_ple_gate"][...]))
    h = h + _dot(p.astype(BF16), w["w_ple_proj"][...]) * gate
    if final_norm:
        h = _rms(h, w["g_final"][...])
    return h


def _dense_tail(h, attn, lru_out, p, w, final_norm):
    h = _mix_out(h, attn, lru_out, w)
    hn = _rms(h, w["g_ffn"][...]).astype(BF16)
    d_ff = w["w_ffn_gate"].shape[1]
    for c0, c1 in _ffn_bounds(d_ff, -(-d_ff // (2 * LANES)) * LANES):
        h = _ffn_piece(hn, h, w, c0, c1)
    return _ple_out(h, p, w, final_norm)


def _softmax_pv(scores, values):
    m = functools.reduce(jnp.maximum, [jnp.max(s, axis=-1, keepdims=True) for s in scores])
    num = None
    den = None
    for s, v in zip(scores, values):
        e = jnp.exp2(s - m)
        l = jnp.sum(e, axis=-1, keepdims=True)
        o = _dot(e.astype(BF16), v)
        num = o if num is None else num + o
        den = l if den is None else den + l
    return num / den


WEIGHT_NAMES = (
    "g_mix", "w_in", "conv_w", "conv_b", "w_r", "b_r", "w_i", "b_i", "lam",
    "g_attn_out", "g_lru_out", "w_out", "g_ffn", "w_ffn_gate", "w_ffn_up", "w_ffn_down",
    "g_ple", "w_ple_gate", "w_ple_proj", "g_final")


def _order_after(x, *deps):
    bits = None
    for dep in deps:
        b = pltpu.bitcast(dep[:SUBLANES, :LANES], jnp.uint32)
        bits = b if bits is None else bits | b
    zero = pltpu.bitcast((bits >> 16) >> 16, F32)
    head = x[:SUBLANES] + jnp.concatenate([zero] * (x.shape[1] // LANES), axis=1)
    return jnp.concatenate([head, x[SUBLANES:]], axis=0)


def _project(x, w):
    xn = _rms(x, w["g_mix"][...]).astype(BF16)
    proj = _dot(xn, w["w_in"][...])
    aw = w["g_attn_out"].shape[1]
    lw = w["g_lru_out"].shape[1]
    q = proj[:, :aw] * (SCALE * LOG2E)
    k = proj[:, aw:2 * aw]
    v = proj[:, 2 * aw:3 * aw]
    xr = proj[:, 3 * aw:3 * aw + lw]
    gr = proj[:, 3 * aw + lw:]
    return q, k, v, xr, gr


def _conv(xr_buf, base, rows, w):
    cw = w["conv_w"]
    ext = xr_buf[base - SUBLANES:base + rows, :]
    xc = w["conv_b"][...] + ext[SUBLANES:] * cw[CONV_WIDTH - 1:CONV_WIDTH, :]
    for back in range(1, CONV_WIDTH):
        j = CONV_WIDTH - 1 - back
        xc = xc + pltpu.roll(ext, back, axis=0)[SUBLANES:] * cw[j:j + 1, :]
    return xc


def _bias_kernel(ext_ref, bias_ref):
    n_heads = ext_ref.shape[0]
    for h in range(n_heads):
        rows = jnp.broadcast_to(ext_ref[h], (CHUNK, EXT))
        rolled = pltpu.roll(rows, EXT - (CHUNK - 1), axis=1, stride=1, stride_axis=0)
        bias_ref[h] = rolled[:, :BAND] * LOG2E


def _bias_table(rel_table):
    n_heads, rel_size = rel_table.shape
    assert rel_size == REL_CLIP + CHUNK
    left = (BAND - 1) - REL_CLIP
    ext = jnp.pad(rel_table.astype(F32), ((0, 0), (left, EXT - left - rel_size)), mode="edge")
    return pl.pallas_call(
        _bias_kernel,
        out_shape=jax.ShapeDtypeStruct((n_heads, CHUNK, BAND), F32),
        name="rel_bias_expand",
    )(ext.reshape(n_heads, 1, EXT))


def _prompt_kernel(*refs, tile, seq, final_norm):
    x_ref, xprev_ref, pprev_ref, bias_ref = refs[:4]
    nw = len(WEIGHT_NAMES)
    w = dict(zip(WEIGHT_NAMES, refs[4:4 + nw]))
    y_ref, ko_ref, vo_ref, convo_ref, ho_ref = refs[4 + nw:9 + nw]
    kbuf, vbuf, q_scr, attn_scr, gr_scr, xr_buf, h_scr = refs[9 + nw:]

    t = pl.program_id(1)
    n_tiles = seq // tile
    n_pairs = kbuf.shape[1] // LANES

    @pl.when(t == 0)
    def _():
        kbuf[0:PAD, :] = jnp.zeros((PAD, kbuf.shape[1]), BF16)
        vbuf[0:PAD, :] = jnp.zeros((PAD, vbuf.shape[1]), BF16)
        xr_buf[...] = jnp.zeros(xr_buf.shape, F32)
        gr_scr[...] = jnp.zeros(gr_scr.shape, F32)
        h_scr[...] = jnp.zeros(h_scr.shape, F32)
        attn_scr[...] = jnp.zeros(attn_scr.shape, F32)

    st = {}

    xc = _conv(xr_buf, SUBLANES, tile, w)
    xcb = xc.astype(BF16)
    r_pre = _dot(xcb, w["w_r"][...])
    i_pre = _dot(xcb, w["w_i"][...])
    h0 = h_scr[...]
    gr_prev = gr_scr[...]
    xr_tail = xr_buf[tile:tile + SUBLANES, :]

    q, k, v, xr, gr = _project(_order_after(x_ref[0], r_pre, i_pre), w)

    a, u = _lru_inputs(xc, r_pre, i_pre, w)
    u = _scan_blocked(a, u, h0)
    lru_out = u * _gelu_tanh(gr_prev)

    q_scr[...] = q.astype(BF16)
    kbuf[PAD:PAD + tile, :] = k.astype(BF16)
    vbuf[PAD:PAD + tile, :] = v.astype(BF16)
    h_scr[...] = jnp.where(t > 0, u[tile - 1:tile, :], 0.0)

    xr_buf[0:SUBLANES, :] = xr_tail
    xr_buf[SUBLANES:SUBLANES + tile, :] = xr
    gr_scr[...] = gr

    keep_from = seq - PAD

    @pl.when(jnp.logical_and(t < n_tiles, t * tile >= keep_from))
    def _():
        off = pl.multiple_of(t * tile - keep_from, tile)
        ko_ref[0, pl.ds(off, tile), :] = k
        vo_ref[0, pl.ds(off, tile), :] = v

    @pl.when(t == n_tiles)
    def _():
        convo_ref[0] = xr_buf[SUBLANES - (CONV_WIDTH - 1):SUBLANES, :]
        ho_ref[0] = h_scr[...]

    st["h"] = _mix_out(xprev_ref[0], attn_scr[...], lru_out, w)

    rows_w = tile // 16
    dense = []

    def ffn_norm():
        st["hn"] = _rms(st["h"], w["g_ffn"][...]).astype(BF16)

    def ffn_gate(c0, c1):
        st["g"] = _dot(st["hn"], w["w_ffn_gate"][:, c0:c1])

    def ffn_up(c0, c1):
        g = st.pop("g")
        st["act"] = ((g * _sigmoid(g)) * _dot(st["hn"], w["w_ffn_up"][:, c0:c1])).astype(BF16)

    def ffn_down(c0, c1):
        st["h"] = st["h"] + _dot(st.pop("act"), w["w_ffn_down"][c0:c1, :])

    d_model = x_ref.shape[2]
    d_ff = w["w_ffn_gate"].shape[1]
    dense += [(0, ffn_norm)]
    for c0, c1 in _ffn_bounds(d_ff, FFN_CHUNK):
        wt = rows_w * (d_model // MXU_DIM) * -(-(c1 - c0) // MXU_DIM)
        dense += [(wt, functools.partial(ffn_gate, c0, c1)), (wt, functools.partial(ffn_up, c0, c1)),
                  (wt, functools.partial(ffn_down, c0, c1))]

    def ple_norm():
        st["hpn"] = _rms(st["h"], w["g_ple"][...]).astype(BF16)
        st["pg"] = []

    def ple_gate(c0, c1):
        st["pg"].append(_sigmoid(_dot(st["hpn"], w["w_ple_gate"][:, c0:c1])))

    def ple_out():
        h = st["h"] + _dot(pprev_ref[0].astype(BF16), w["w_ple_proj"][...]) * jnp.concatenate(
            st.pop("pg"), axis=-1)
        if final_norm:
            h = _rms(h, w["g_final"][...])
        y_ref[0] = h

    dense += [(0, ple_norm)]
    for c0, c1 in _ffn_bounds(d_model, MXU_DIM):
        dense += [(rows_w * (d_model // MXU_DIM), functools.partial(ple_gate, c0, c1))]
    dense += [(rows_w * (d_model // MXU_DIM), ple_out)]

    lane = lax.broadcasted_iota(jnp.int32, (CHUNK, LANES), 1)
    low_half = lane < HEAD_DIM
    kidx = lax.broadcasted_iota(jnp.int32, (1, BAND), 1)
    units = [(j, pr) for j in range(tile // CHUNK) for pr in range(n_pairs)]

    def scores(j, pr):
        lanes = slice(pr * LANES, (pr + 1) * LANES)
        q2 = q_scr[j * CHUNK:(j + 1) * CHUNK, lanes]
        zero = jnp.zeros_like(q2)
        qs = jnp.concatenate([jnp.where(low_half, q2, zero), jnp.where(low_half, zero, q2)],
                             axis=0)
        s = _dot_nt(qs, kbuf[j * CHUNK:j * CHUNK + BAND, lanes]) + bias_ref[pr]
        valid = (t * tile + j * CHUNK + kidx - PAD) >= PAD
        return jnp.where(valid, s, NEG)

    def attend(j, pr, s):
        lanes = slice(pr * LANES, (pr + 1) * LANES)
        o = _softmax_pv([s], [vbuf[j * CHUNK:j * CHUNK + BAND, lanes]])
        st["attn", j, pr] = jnp.where(low_half, o[:CHUNK], o[CHUNK:])

    total_w = sum(wt for wt, _ in dense)
    done_w = 0
    pending = list(dense)
    s_cur = scores(*units[0])
    for i, unit in enumerate(units):
        s_next = scores(*units[i + 1]) if i + 1 < len(units) else None
        while pending and (done_w < total_w * (i + 1) // len(units) or i + 1 == len(units)):
            wt, piece = pending.pop(0)
            piece()
            done_w += wt
        attend(*unit, s_cur)
        s_cur = s_next

    for j, pr in units:
        attn_scr[j * CHUNK:(j + 1) * CHUNK, pr * LANES:(pr + 1) * LANES] = st.pop(("attn", j, pr))

    for r in range(0, PAD, tile):
        n = min(tile, PAD - r)
        kbuf[r:r + n, :] = kbuf[r + tile:r + tile + n, :]
        vbuf[r:r + n, :] = vbuf[r + tile:r + tile + n, :]


def _const_spec(arr):
    nd = arr.ndim
    return pl.BlockSpec(arr.shape, lambda *_: (0,) * nd, pipeline_mode=pl.Buffered(1))


def _prompt_layer(x, p, bias, weights, final_norm, tile):
    b, seq, d = x.shape
    aw = weights["g_attn_out"].shape[1]
    lw = weights["g_lru_out"].shape[1]
    assert seq % tile == 0 and PAD % tile == 0 and seq >= PAD and tile % CHUNK == 0
    assert tile & (tile - 1) == 0 and aw % LANES == 0
    n_tiles = seq // tile
    wlist = [weights[n] for n in WEIGHT_NAMES]
    kern = functools.partial(_prompt_kernel, tile=tile, seq=seq, final_norm=final_norm)
    out_shape = (
        jax.ShapeDtypeStruct((b, seq, d), F32),
        jax.ShapeDtypeStruct((b, PAD, aw), F32),
        jax.ShapeDtypeStruct((b, PAD, aw), F32),
        jax.ShapeDtypeStruct((b, CONV_WIDTH - 1, lw), F32),
        jax.ShapeDtypeStruct((b, 1, lw), F32),
    )
    cur = lambda i, j: (i, jnp.minimum(j, n_tiles - 1), 0)
    prev = lambda i, j: (i, jnp.maximum(j - 1, 0), 0)
    in_specs = [
        pl.BlockSpec((1, tile, d), cur),
        pl.BlockSpec((1, tile, d), prev),
        pl.BlockSpec((1, tile, p.shape[-1]), prev),
        _const_spec(bias),
    ] + [_const_spec(a) for a in wlist]
    out_specs = (
        pl.BlockSpec((1, tile, d), prev),
        pl.BlockSpec((1, PAD, aw), lambda i, j: (i, 0, 0)),
        pl.BlockSpec((1, PAD, aw), lambda i, j: (i, 0, 0)),
        pl.BlockSpec((1, CONV_WIDTH - 1, lw), lambda i, j: (i, 0, 0)),
        pl.BlockSpec((1, 1, lw), lambda i, j: (i, 0, 0)),
    )
    scratch = [
        pltpu.VMEM((PAD + tile, aw), BF16),
        pltpu.VMEM((PAD + tile, aw), BF16),
        pltpu.VMEM((tile, aw), BF16),
        pltpu.VMEM((tile, aw), F32),
        pltpu.VMEM((tile, lw), F32),
        pltpu.VMEM((SUBLANES + tile, lw), F32),
        pltpu.VMEM((1, lw), F32),
    ]
    return pl.pallas_call(
        kern,
        out_shape=out_shape,
        grid=(b, n_tiles + 1),
        in_specs=in_specs,
        out_specs=out_specs,
        scratch_shapes=scratch,
        compiler_params=pltpu.CompilerParams(
            dimension_semantics=("arbitrary", "arbitrary"), vmem_limit_bytes=VMEM_LIMIT),
        name="prompt_layer",
    )(x, x, p, bias, *wlist)


def _sample_kernel(*refs, final_norm):
    x_ref, p_ref, ck_ref, cv_ref, sconv_ref, sh_ref, bias_ref = refs[:7]
    nw = len(WEIGHT_NAMES)
    w = dict(zip(WEIGHT_NAMES, refs[7:7 + nw]))
    y_ref, ko_ref, vo_ref, convo_ref, ho_ref = refs[7 + nw:12 + nw]
    attn_scr, xr_buf = refs[12 + nw:]

    nb, t, d = x_ref.shape
    n_cache = ck_ref.shape[1]
    aw = ck_ref.shape[2]
    n_pairs = aw // LANES
    rows = nb * t

    x = x_ref[...].reshape(rows, d)
    q, k, v, xr, gr = _project(x, w)
    ko_ref[...] = k.reshape(nb, t, aw)
    vo_ref[...] = v.reshape(nb, t, aw)
    qb = q.astype(BF16)
    kb_new = k.astype(BF16)
    vb_new = v.astype(BF16)

    lane = lax.broadcasted_iota(jnp.int32, (t, LANES), 1)
    low_half = lane < HEAD_DIM
    for b in range(nb):
        rs = slice(b * t, (b + 1) * t)
        for pr in range(n_pairs):
            lanes = slice(pr * LANES, (pr + 1) * LANES)
            q2 = qb[rs, lanes]
            kc = ck_ref[b, :, lanes].astype(BF16)
            vc = cv_ref[b, :, lanes].astype(BF16)
            kn = kb_new[rs, lanes]
            vn = vb_new[rs, lanes]
            outs = []
            for e in range(2):
                h = 2 * pr + e
                qm = jnp.where(low_half if e == 0 else jnp.logical_not(low_half), q2,
                               jnp.zeros_like(q2))
                s_c = _dot_nt(qm, kc) + bias_ref[h, 0:t, PAD - n_cache:PAD]
                s_n = _dot_nt(qm, kn) + bias_ref[h, 0:t, PAD:PAD + t]
                outs.append(_softmax_pv([s_c, s_n], [vc, vn]))
            attn_scr[rs, lanes] = jnp.where(low_half, outs[0], outs[1])

    seg_rows = SUBLANES + t
    xcs = []
    for b in range(nb):
        base = b * seg_rows
        xr_buf[base + SUBLANES - (CONV_WIDTH - 1):base + SUBLANES, :] = sconv_ref[b]
        xr_buf[base + SUBLANES:base + seg_rows, :] = xr[b * t:(b + 1) * t, :]
        xcs.append(_conv(xr_buf, base + SUBLANES, t, w))
        convo_ref[b] = xr_buf[base + seg_rows - (CONV_WIDTH - 1):base + seg_rows, :]
    xc = jnp.concatenate(xcs, axis=0)
    h0_rows = jnp.concatenate(
        [jnp.broadcast_to(sh_ref[b], (t, sh_ref.shape[2])) for b in range(nb)], axis=0)
    hs, lru_out = _rglru(xc, gr, h0_rows, t, w)
    for b in range(nb):
        ho_ref[b] = hs[(b + 1) * t - 1:(b + 1) * t, :]

    y = _dense_tail(x, attn_scr[...], lru_out, p_ref[...].reshape(rows, p_ref.shape[2]), w,
                    final_norm)
    y_ref[...] = y.reshape(nb, t, d)


def _sample_layer(x, p, cache_k, cache_v, state_conv, state_h, bias, weights, final_norm, group):
    nbatch, t, d = x.shape
    n_cache, aw = cache_k.shape[1:]
    lw = weights["g_lru_out"].shape[1]
    assert nbatch % group == 0 and t % SUBLANES == 0 and t & (t - 1) == 0
    assert CONV_WIDTH - 1 <= t <= CHUNK and n_cache <= PAD and aw % LANES == 0
    wlist = [weights[n] for n in WEIGHT_NAMES]
    kern = functools.partial(_sample_kernel, final_norm=final_norm)
    state_h3 = state_h.reshape(nbatch, 1, lw)

    def grp(shape):
        nd = len(shape)
        return pl.BlockSpec((group,) + tuple(shape[1:]), lambda i: (i,) + (0,) * (nd - 1))

    out_shape = (
        jax.ShapeDtypeStruct((nbatch, t, d), F32),
        jax.ShapeDtypeStruct((nbatch, t, aw), F32),
        jax.ShapeDtypeStruct((nbatch, t, aw), F32),
        jax.ShapeDtypeStruct((nbatch, CONV_WIDTH - 1, lw), F32),
        jax.ShapeDtypeStruct((nbatch, 1, lw), F32),
    )
    ins = [x, p, cache_k, cache_v, state_conv, state_h3]
    in_specs = [grp(a.shape) for a in ins] + [_const_spec(bias)] + [_const_spec(a) for a in wlist]
    out_specs = tuple(grp(s.shape) for s in out_shape)
    scratch = [
        pltpu.VMEM((group * t, aw), F32),
        pltpu.VMEM((group * (SUBLANES + t), lw), F32),
    ]
    return pl.pallas_call(
        kern,
        out_shape=out_shape,
        grid=(nbatch // group,),
        in_specs=in_specs,
        out_specs=out_specs,
        scratch_shapes=scratch,
        compiler_params=pltpu.CompilerParams(
            dimension_semantics=("arbitrary",), vmem_limit_bytes=VMEM_LIMIT),
        name="sample_layer",
    )(*ins, bias, *wlist)


def _block_diag(wb):
    n, kk, jj = wb.shape
    eye = jnp.eye(n, dtype=wb.dtype)
    return (wb[:, :, None, :] * eye[:, None, :, None]).reshape(n * kk, n * jj)


PROMPT_TILE = 256
SAMPLE_GROUP = 4


def kernel(x_prompt, x_sample, p_prompt, p_sample, cache_k, cache_v, state_conv, state_h, g_mix, w_in, conv_w, conv_b, w_rgate, b_rgate, w_igate, b_igate, lru_lambda, rel_bias_table, g_attn_out, g_lru_out, w_out, g_ffn, w_ffn_gate, w_ffn_up, w_ffn_down, g_ple, w_ple_gate, w_ple_proj, g_final):
    depth = w_in.shape[0]
    hp, hs = x_prompt, x_sample
    outs = [[] for _ in range(8)]
    for i in range(depth):
        row = lambda a: a[i].reshape(1, -1).astype(F32)
        weights = {
            "g_mix": row(g_mix), "w_in": w_in[i].astype(BF16),
            "conv_w": conv_w[i].astype(F32), "conv_b": row(conv_b),
            "w_r": _block_diag(w_rgate[i]).astype(BF16), "b_r": row(b_rgate),
            "w_i": _block_diag(w_igate[i]).astype(BF16), "b_i": row(b_igate),
            "lam": row(lru_lambda),
            "g_attn_out": row(g_attn_out), "g_lru_out": row(g_lru_out),
            "w_out": w_out[i].astype(BF16), "g_ffn": row(g_ffn),
            "w_ffn_gate": w_ffn_gate[i].astype(BF16), "w_ffn_up": w_ffn_up[i].astype(BF16),
            "w_ffn_down": w_ffn_down[i].astype(BF16), "g_ple": row(g_ple),
            "w_ple_gate": w_ple_gate[i].astype(BF16), "w_ple_proj": w_ple_proj[i].astype(BF16),
            "g_final": g_final.reshape(1, -1).astype(F32),
        }
        final = i == depth - 1
        bias = _bias_table(rel_bias_table[i])
        n_heads = rel_bias_table.shape[1]
        aw = n_heads * HEAD_DIM

        hp, k1, v1, c1, r1 = _prompt_layer(
            hp, p_prompt[i], bias.reshape(n_heads // 2, 2 * CHUNK, BAND), weights, final,
            PROMPT_TILE)
        nb, nc = cache_k.shape[1:3]
        hs, k2, v2, c2, r2 = _sample_layer(
            hs, p_sample[i], cache_k[i].reshape(nb, nc, aw), cache_v[i].reshape(nb, nc, aw),
            state_conv[i], state_h[i], bias, weights, final, SAMPLE_GROUP)
        bp, keep = k1.shape[:2]
        ts = k2.shape[1]
        for lst, val in zip(outs, (
                k1.reshape(bp, keep, n_heads, HEAD_DIM), v1.reshape(bp, keep, n_heads, HEAD_DIM),
                c1, r1.reshape(bp, -1),
                k2.reshape(nb, ts, n_heads, HEAD_DIM), v2.reshape(nb, ts, n_heads, HEAD_DIM),
                c2, r2.reshape(nb, -1))):
            lst.append(val)
    return (hp, hs) + tuple(jnp.stack(l) for l in outs)
```

```python
import functools

import jax
import jax.numpy as jnp
from jax import lax
from jax.experimental import pallas as pl
from jax.experimental.pallas import tpu as pltpu

CHUNK = 64
LEFT_CHUNKS = 8
PAD = LEFT_CHUNKS * CHUNK
BAND = PAD + CHUNK
HEAD_DIM = 64
CONV_WIDTH = 4
LRU_C = 8.0
REL_CLIP = 128
EPS = 1e-6
NEG = -1e30
SCALE = HEAD_DIM ** -0.5
LOG2E = 1.4426950408889634
LANES = 128
SUBLANES = 8
MXU_DIM = 256
EXT = 640
VMEM_LIMIT = 60 * 1024 * 1024
FFN_CHUNK = 2 * MXU_DIM

F32 = jnp.float32
BF16 = jnp.bfloat16


def _dot(a, b):
    return jnp.dot(a, b, preferred_element_type=F32)


def _dot_nt(a, b):
    return lax.dot_general(a, b, (((1,), (1,)), ((), ())), preferred_element_type=F32)


def _rms(x, g):
    ms = jnp.mean(x * x, axis=-1, keepdims=True)
    return (x * lax.rsqrt(ms + EPS)) * g


def _sigmoid(x):
    return 1.0 / (1.0 + jnp.exp2(x * (-LOG2E)))


def _gelu_tanh(x):
    c = 0.7978845608028654
    return 0.5 * x * (1.0 + jnp.tanh(c * (x + 0.044715 * (x * x * x))))


def _shift_rows(x, d, fill, seg):
    rows = x.shape[0]
    if seg == rows and d % SUBLANES == 0:
        return jnp.concatenate([jnp.full((d, x.shape[1]), fill, x.dtype), x[:rows - d]], axis=0)
    rolled = pltpu.roll(x, d, axis=0)
    row = lax.broadcasted_iota(jnp.int32, x.shape, 0)
    return jnp.where((row & (seg - 1)) >= d, rolled, fill)


def _scan_distances(seg):
    return [1 << b for b in range(seg.bit_length() - 1)]


def _scan_step(a, u, d, seg):
    u = u + a * _shift_rows(u, d, 0.0, seg)
    if 2 * d < seg:
        a = a * _shift_rows(a, d, 1.0, seg)
    return a, u


def _sqrt_nonneg(y):
    return jnp.where(y > 0.0, y * lax.rsqrt(y), 0.0)


def _lru_inputs(xc, r_pre, i_pre, w):
    r = _sigmoid(r_pre + w["b_r"][...])
    ig = _sigmoid(i_pre + w["b_i"][...])
    z = -w["lam"][...]
    softplus = jnp.maximum(z, 0.0) + jnp.log1p(jnp.exp(-jnp.abs(z)))
    a = jnp.exp2(r * ((-LRU_C * LOG2E) * softplus))
    u = _sqrt_nonneg(1.0 - a * a) * (ig * xc)
    return a, u


def _rglru(xc, gr, h0_rows, seg, w):
    xcb = xc.astype(BF16)
    a, u = _lru_inputs(xc, _dot(xcb, w["w_r"][...]), _dot(xcb, w["w_i"][...]), w)
    row = lax.broadcasted_iota(jnp.int32, xc.shape, 0)
    u = u + jnp.where((row & (seg - 1)) == 0, a * h0_rows, 0.0)
    for d in _scan_distances(seg):
        a, u = _scan_step(a, u, d, seg)
    return u, u * _gelu_tanh(gr)


def _scan_blocked(a, u, h0):
    for d in _scan_distances(SUBLANES):
        u = u + a * _shift_rows(u, d, 0.0, SUBLANES)
        a = a * _shift_rows(a, d, 1.0, SUBLANES)
    carry = jnp.broadcast_to(h0, (SUBLANES, a.shape[1]))
    groups = []
    for g in range(a.shape[0] // SUBLANES):
        rows = slice(g * SUBLANES, (g + 1) * SUBLANES)
        h = u[rows] + a[rows] * carry
        groups.append(h)
        carry = jnp.broadcast_to(h[SUBLANES - 1:SUBLANES], h.shape)
    return jnp.concatenate(groups, axis=0)


def _mix_out(h, attn, lru_out, w):
    mixed = jnp.concatenate(
        [_rms(attn, w["g_attn_out"][...]), _rms(lru_out, w["g_lru_out"][...])], axis=-1)
    return h + _dot(mixed.astype(BF16), w["w_out"][...])


def _ffn_piece(hn, acc, w, c0, c1):
    gate = _dot(hn, w["w_ffn_gate"][:, c0:c1])
    up = _dot(hn, w["w_ffn_up"][:, c0:c1])
    act = (gate * _sigmoid(gate)) * up
    return acc + _dot(act.astype(BF16), w["w_ffn_down"][c0:c1, :])


def _ffn_bounds(d_ff, step):
    return [(c, min(c + step, d_ff)) for c in range(0, d_ff, step)]


def _ple_out(h, p, w, final_norm):
    gate = _sigmoid(_dot(_rms(h, w["g_ple"][...]).astype(BF16), w["w_ple_gate"][...]))
    h = h + _dot(p.astype(BF16), w["w_ple_proj"][...]) * gate
    if final_norm:
        h = _rms(h, w["g_final"][...])
    return h


def _dense_tail(h, attn, lru_out, p, w, final_norm):
    h = _mix_out(h, attn, lru_out, w)
    hn = _rms(h, w["g_ffn"][...]).astype(BF16)
    d_ff = w["w_ffn_gate"].shape[1]
    for c0, c1 in _ffn_bounds(d_ff, -(-d_ff // (2 * LANES)) * LANES):
        h = _ffn_piece(hn, h, w, c0, c1)
    return _ple_out(h, p, w, final_norm)


def _softmax_pv(scores, values):
    m = functools.reduce(jnp.maximum, [jnp.max(s, axis=-1, keepdims=True) for s in scores])
    num = None
    den = None
    for s, v in zip(scores, values):
        e = jnp.exp2(s - m)
        l = jnp.sum(e, axis=-1, keepdims=True)
        o = _dot(e.astype(BF16), v)
        num = o if num is None else num + o
        den = l if den is None else den + l
    return num / den


WEIGHT_NAMES = (
    "g_mix", "w_in", "conv_w", "conv_b", "w_r", "b_r", "w_i", "b_i", "lam",
    "g_attn_out", "g_lru_out", "w_out", "g_ffn", "w_ffn_gate", "w_ffn_up", "w_ffn_down",
    "g_ple", "w_ple_gate", "w_ple_proj", "g_final")


def _order_after(x, *deps):
    bits = None
    for dep in deps:
        b = pltpu.bitcast(dep[:SUBLANES, :LANES], jnp.uint32)
        bits = b if bits is None else bits | b
    zero = pltpu.bitcast((bits >> 16) >> 16, F32)
    head = x[:SUBLANES] + jnp.concatenate([zero] * (x.shape[1] // LANES), axis=1)
    return jnp.concatenate([head, x[SUBLANES:]], axis=0)


def _project(x, w):
    xn = _rms(x, w["g_mix"][...]).astype(BF16)
    proj = _dot(xn, w["w_in"][...])
    aw = w["g_attn_out"].shape[1]
    lw = w["g_lru_out"].shape[1]
    q = proj[:, :aw] * (SCALE * LOG2E)
    k = proj[:, aw:2 * aw]
    v = proj[:, 2 * aw:3 * aw]
    xr = proj[:, 3 * aw:3 * aw + lw]
    gr = proj[:, 3 * aw + lw:]
    return q, k, v, xr, gr


def _conv(xr_buf, base, rows, w):
    cw = w["conv_w"]
    ext = xr_buf[base - SUBLANES:base + rows, :]
    xc = w["conv_b"][...] + ext[SUBLANES:] * cw[CONV_WIDTH - 1:CONV_WIDTH, :]
    for back in range(1, CONV_WIDTH):
        j = CONV_WIDTH - 1 - back
        xc = xc + pltpu.roll(ext, back, axis=0)[SUBLANES:] * cw[j:j + 1, :]
    return xc


def _bias_kernel(ext_ref, bias_ref):
    n_heads = ext_ref.shape[0]
    for h in range(n_heads):
        rows = jnp.broadcast_to(ext_ref[h], (CHUNK, EXT))
        rolled = pltpu.roll(rows, EXT - (CHUNK - 1), axis=1, stride=1, stride_axis=0)
        bias_ref[h] = rolled[:, :BAND] * LOG2E


def _bias_table(rel_table):
    n_heads, rel_size = rel_table.shape
    assert rel_size == REL_CLIP + CHUNK
    left = (BAND - 1) - REL_CLIP
    ext = jnp.pad(rel_table.astype(F32), ((0, 0), (left, EXT - left - rel_size)), mode="edge")
    return pl.pallas_call(
        _bias_kernel,
        out_shape=jax.ShapeDtypeStruct((n_heads, CHUNK, BAND), F32),
        name="rel_bias_expand",
    )(ext.reshape(n_heads, 1, EXT))


def _prompt_kernel(*refs, tile, seq, final_norm):
    x_ref, xprev_ref, pprev_ref, bias_ref = refs[:4]
    nw = len(WEIGHT_NAMES)
    w = dict(zip(WEIGHT_NAMES, refs[4:4 + nw]))
    y_ref, ko_ref, vo_ref, convo_ref, ho_ref = refs[4 + nw:9 + nw]
    kbuf, vbuf, q_scr, attn_scr, gr_scr, xr_buf, h_scr = refs[9 + nw:]

    t = pl.program_id(1)
    n_tiles = seq // tile
    n_pairs = kbuf.shape[1] // LANES

    @pl.when(t == 0)
    def _():
        kbuf[0:PAD, :] = jnp.zeros((PAD, kbuf.shape[1]), BF16)
        vbuf[0:PAD, :] = jnp.zeros((PAD, vbuf.shape[1]), BF16)
        xr_buf[...] = jnp.zeros(xr_buf.shape, F32)
        gr_scr[...] = jnp.zeros(gr_scr.shape, F32)
        h_scr[...] = jnp.zeros(h_scr.shape, F32)
        attn_scr[...] = jnp.zeros(attn_scr.shape, F32)

    st = {}

    xc = _conv(xr_buf, SUBLANES, tile, w)
    xcb = xc.astype(BF16)
    r_pre = _dot(xcb, w["w_r"][...])
    i_pre = _dot(xcb, w["w_i"][...])
    h0 = h_scr[...]
    gr_prev = gr_scr[...]
    xr_tail = xr_buf[tile:tile + SUBLANES, :]

    q, k, v, xr, gr = _project(_order_after(x_ref[0], r_pre, i_pre), w)

    a, u = _lru_inputs(xc, r_pre, i_pre, w)
    u = _scan_blocked(a, u, h0)
    lru_out = u * _gelu_tanh(gr_prev)

    q_scr[...] = q.astype(BF16)
    kbuf[PAD:PAD + tile, :] = k.astype(BF16)
    vbuf[PAD:PAD + tile, :] = v.astype(BF16)
    h_scr[...] = jnp.where(t > 0, u[tile - 1:tile, :], 0.0)

    xr_buf[0:SUBLANES, :] = xr_tail
    xr_buf[SUBLANES:SUBLANES + tile, :] = xr
    gr_scr[...] = gr

    keep_from = seq - PAD

    @pl.when(jnp.logical_and(t < n_tiles, t * tile >= keep_from))
    def _():
        off = pl.multiple_of(t * tile - keep_from, tile)
        ko_ref[0, pl.ds(off, tile), :] = k
        vo_ref[0, pl.ds(off, tile), :] = v

    @pl.when(t == n_tiles)
    def _():
        convo_ref[0] = xr_buf[SUBLANES - (CONV_WIDTH - 1):SUBLANES, :]
        ho_ref[0] = h_scr[...]

    st["h"] = _mix_out(xprev_ref[0], attn_scr[...], lru_out, w)

    rows_w = tile // 16
    dense = []

    def ffn_norm():
        st["hn"] = _rms(st["h"], w["g_ffn"][...]).astype(BF16)

    def ffn_gate(c0, c1):
        st["g"] = _dot(st["hn"], w["w_ffn_gate"][:, c0:c1])

    def ffn_up(c0, c1):
        g = st.pop("g")
        st["act"] = ((g * _sigmoid(g)) * _dot(st["hn"], w["w_ffn_up"][:, c0:c1])).astype(BF16)

    def ffn_down(c0, c1):
        st["h"] = st["h"] + _dot(st.pop("act"), w["w_ffn_down"][c0:c1, :])

    d_model = x_ref.shape[2]
    d_ff = w["w_ffn_gate"].shape[1]
    dense += [(0, ffn_norm)]
    for c0, c1 in _ffn_bounds(d_ff, FFN_CHUNK):
        wt = rows_w * (d_model // MXU_DIM) * -(-(c1 - c0) // MXU_DIM)
        dense += [(wt, functools.partial(ffn_gate, c0, c1)), (wt, functools.partial(ffn_up, c0, c1)),
                  (wt, functools.partial(ffn_down, c0, c1))]

    def ple_norm():
        st["hpn"] = _rms(st["h"], w["g_ple"][...]).astype(BF16)
        st["pg"] = []

    def ple_gate(c0, c1):
        st["pg"].append(_sigmoid(_dot(st["hpn"], w["w_ple_gate"][:, c0:c1])))

    def ple_out():
        h = st["h"] + _dot(pprev_ref[0].astype(BF16), w["w_ple_proj"][...]) * jnp.concatenate(
            st.pop("pg"), axis=-1)
        if final_norm:
            h = _rms(h, w["g_final"][...])
        y_ref[0] = h

    dense += [(0, ple_norm)]
    for c0, c1 in _ffn_bounds(d_model, MXU_DIM):
        dense += [(rows_w * (d_model // MXU_DIM), functools.partial(ple_gate, c0, c1))]
    dense += [(rows_w * (d_model // MXU_DIM), ple_out)]

    lane = lax.broadcasted_iota(jnp.int32, (CHUNK, LANES), 1)
    low_half = lane < HEAD_DIM
    kidx = lax.broadcasted_iota(jnp.int32, (1, BAND), 1)
    units = [(j, pr) for j in range(tile // CHUNK) for pr in range(n_pairs)]

    def scores(j, pr):
        lanes = slice(pr * LANES, (pr + 1) * LANES)
        q2 = q_scr[j * CHUNK:(j + 1) * CHUNK, lanes]
        zero = jnp.zeros_like(q2)
        qs = jnp.concatenate([jnp.where(low_half, q2, zero), jnp.where(low_half, zero, q2)],
                             axis=0)
        s = _dot_nt(qs, kbuf[j * CHUNK:j * CHUNK + BAND, lanes]) + bias_ref[pr]
        valid = (t * tile + j * CHUNK + kidx - PAD) >= PAD
        return jnp.where(valid, s, NEG)

    def attend(j, pr, s):
        lanes = slice(pr * LANES, (pr + 1) * LANES)
        o = _softmax_pv([s], [vbuf[j * CHUNK:j * CHUNK + BAND, lanes]])
        st["attn", j, pr] = jnp.where(low_half, o[:CHUNK], o[CHUNK:])

    total_w = sum(wt for wt, _ in dense)
    done_w = 0
    pending = list(dense)
    s_cur = scores(*units[0])
    for i, unit in enumerate(units):
        s_next = scores(*units[i + 1]) if i + 1 < len(units) else None
        while pending and (done_w < total_w * (i + 1) // len(units) or i + 1 == len(units)):
            wt, piece = pending.pop(0)
            piece()
            done_w += wt
        attend(*unit, s_cur)
        s_cur = s_next

    for j, pr in units:
        attn_scr[j * CHUNK:(j + 1) * CHUNK, pr * LANES:(pr + 1) * LANES] = st.pop(("attn", j, pr))

    for r in range(0, PAD, tile):
        n = min(tile, PAD - r)
        kbuf[r:r + n, :] = kbuf[r + tile:r + tile + n, :]
        vbuf[r:r + n, :] = vbuf[r + tile:r + tile + n, :]


def _const_spec(arr):
    nd = arr.ndim
    return pl.BlockSpec(arr.shape, lambda *_: (0,) * nd, pipeline_mode=pl.Buffered(1))


def _prompt_layer(x, p, bias, weights, final_norm, tile):
    b, seq, d = x.shape
    aw = weights["g_attn_out"].shape[1]
    lw = weights["g_lru_out"].shape[1]
    assert seq % tile == 0 and PAD % tile == 0 and seq >= PAD and tile % CHUNK == 0
    assert tile & (tile - 1) == 0 and aw % LANES == 0
    n_tiles = seq // tile
    wlist = [weights[n] for n in WEIGHT_NAMES]
    kern = functools.partial(_prompt_kernel, tile=tile, seq=seq, final_norm=final_norm)
    out_shape = (
        jax.ShapeDtypeStruct((b, seq, d), F32),
        jax.ShapeDtypeStruct((b, PAD, aw), F32),
        jax.ShapeDtypeStruct((b, PAD, aw), F32),
        jax.ShapeDtypeStruct((b, CONV_WIDTH - 1, lw), F32),
        jax.ShapeDtypeStruct((b, 1, lw), F32),
    )
    cur = lambda i, j: (i, jnp.minimum(j, n_tiles - 1), 0)
    prev = lambda i, j: (i, jnp.maximum(j - 1, 0), 0)
    in_specs = [
        pl.BlockSpec((1, tile, d), cur),
        pl.BlockSpec((1, tile, d), prev),
        pl.BlockSpec((1, tile, p.shape[-1]), prev),
        _const_spec(bias),
    ] + [_const_spec(a) for a in wlist]
    out_specs = (
        pl.BlockSpec((1, tile, d), prev),
        pl.BlockSpec((1, PAD, aw), lambda i, j: (i, 0, 0)),
        pl.BlockSpec((1, PAD, aw), lambda i, j: (i, 0, 0)),
        pl.BlockSpec((1, CONV_WIDTH - 1, lw), lambda i, j: (i, 0, 0)),
        pl.BlockSpec((1, 1, lw), lambda i, j: (i, 0, 0)),
    )
    scratch = [
        pltpu.VMEM((PAD + tile, aw), BF16),
        pltpu.VMEM((PAD + tile, aw), BF16),
        pltpu.VMEM((tile, aw), BF16),
        pltpu.VMEM((tile, aw), F32),
        pltpu.VMEM((tile, lw), F32),
        pltpu.VMEM((SUBLANES + tile, lw), F32),
        pltpu.VMEM((1, lw), F32),
    ]
    return pl.pallas_call(
        kern,
        out_shape=out_shape,
        grid=(b, n_tiles + 1),
        in_specs=in_specs,
        out_specs=out_specs,
        scratch_shapes=scratch,
        compiler_params=pltpu.CompilerParams(
            dimension_semantics=("arbitrary", "arbitrary"), vmem_limit_bytes=VMEM_LIMIT),
        name="prompt_layer",
    )(x, x, p, bias, *wlist)


FRONT_NAMES = ("g_mix", "w_in", "conv_w", "conv_b", "w_r", "b_r", "w_i", "b_i", "lam",
               "g_attn_out", "g_lru_out")
BACK_NAMES = ("g_attn_out", "g_lru_out", "w_out", "g_ffn", "w_ffn_gate", "w_ffn_up",
              "w_ffn_down", "g_ple", "w_ple_gate", "w_ple_proj", "g_final")


def _sample_front_kernel(*refs, n_heads):
    x_ref, ck_ref, cv_ref, sconv_ref, sh_ref, bias_ref = refs[:6]
    nw = len(FRONT_NAMES)
    w = dict(zip(FRONT_NAMES, refs[6:6 + nw]))
    attn_ref, lru_ref, ko_ref, vo_ref, convo_ref, ho_ref = refs[6 + nw:12 + nw]
    (xr_buf,) = refs[12 + nw:]

    nb, t, d = x_ref.shape
    n_cache = ck_ref.shape[1] // n_heads
    aw = n_heads * HEAD_DIM
    rows = nb * t

    x = x_ref[...].reshape(rows, d)
    q, k, v, xr, gr = _project(x, w)
    ko_ref[...] = k.reshape(nb, t, aw)
    vo_ref[...] = v.reshape(nb, t, aw)
    qb = q.astype(BF16)
    kb_new = k.astype(BF16)
    vb_new = v.astype(BF16)

    units = [(b, h) for b in range(nb) for h in range(n_heads)]

    def scores(b, h):
        rs = slice(b * t, (b + 1) * t)
        cols = slice(h * HEAD_DIM, (h + 1) * HEAD_DIM)
        qh = qb[rs, cols]
        kc = ck_ref[b, pl.ds(h, n_cache, stride=n_heads), :].astype(BF16)
        s_c = _dot_nt(qh, kc) + bias_ref[h, 0:t, PAD - n_cache:PAD]
        s_n = _dot_nt(qh, kb_new[rs, cols]) + bias_ref[h, 0:t, PAD:PAD + t]
        return s_c, s_n

    outs = {}
    s_cur = scores(*units[0])
    for i, (b, h) in enumerate(units):
        s_next = scores(*units[i + 1]) if i + 1 < len(units) else None
        vc = cv_ref[b, pl.ds(h, n_cache, stride=n_heads), :].astype(BF16)
        vn = vb_new[b * t:(b + 1) * t, h * HEAD_DIM:(h + 1) * HEAD_DIM]
        outs[b, h] = _softmax_pv(list(s_cur), [vc, vn])
        s_cur = s_next
    for b in range(nb):
        attn_ref[b] = jnp.concatenate([outs[b, h] for h in range(n_heads)], axis=-1)

    seg_rows = SUBLANES + t
    xcs = []
    for b in range(nb):
        base = b * seg_rows
        xr_buf[base + SUBLANES - (CONV_WIDTH - 1):base + SUBLANES, :] = sconv_ref[b]
        xr_buf[base + SUBLANES:base + seg_rows, :] = xr[b * t:(b + 1) * t, :]
        xcs.append(_conv(xr_buf, base + SUBLANES, t, w))
        convo_ref[b] = xr_buf[base + seg_rows - (CONV_WIDTH - 1):base + seg_rows, :]
    xc = jnp.concatenate(xcs, axis=0)
    h0_rows = jnp.concatenate(
        [jnp.broadcast_to(sh_ref[b], (t, sh_ref.shape[2])) for b in range(nb)], axis=0)
    hs, lru_out = _rglru(xc, gr, h0_rows, t, w)
    for b in range(nb):
        ho_ref[b] = hs[(b + 1) * t - 1:(b + 1) * t, :]
    lru_ref[...] = lru_out.reshape(nb, t, lru_out.shape[1])


def _sample_back_kernel(*refs, final_norm):
    x_ref, attn_ref, lru_ref, p_ref = refs[:4]
    nw = len(BACK_NAMES)
    w = dict(zip(BACK_NAMES, refs[4:4 + nw]))
    y_ref = refs[4 + nw]
    y_ref[...] = _dense_tail(x_ref[...], attn_ref[...], lru_ref[...], p_ref[...], w, final_norm)


def _sample_layer(x, p, cache_k, cache_v, state_conv, state_h, bias, weights, final_norm, group,
                  back_rows):
    nbatch, t, d = x.shape
    n_heads = bias.shape[0]
    aw = n_heads * HEAD_DIM
    n_cache = cache_k.shape[1] // n_heads
    lw = weights["g_lru_out"].shape[1]
    assert nbatch % group == 0 and t % SUBLANES == 0 and t & (t - 1) == 0
    assert CONV_WIDTH - 1 <= t <= CHUNK and n_cache <= PAD and cache_k.shape[2] == HEAD_DIM
    assert (nbatch * t) % back_rows == 0
    state_h3 = state_h.reshape(nbatch, 1, lw)

    def grp(shape):
        nd = len(shape)
        return pl.BlockSpec((group,) + tuple(shape[1:]), lambda i: (i,) + (0,) * (nd - 1))

    front_w = [weights[n] for n in FRONT_NAMES]
    front_out = (
        jax.ShapeDtypeStruct((nbatch, t, aw), F32),
        jax.ShapeDtypeStruct((nbatch, t, lw), F32),
        jax.ShapeDtypeStruct((nbatch, t, aw), F32),
        jax.ShapeDtypeStruct((nbatch, t, aw), F32),
        jax.ShapeDtypeStruct((nbatch, CONV_WIDTH - 1, lw), F32),
        jax.ShapeDtypeStruct((nbatch, 1, lw), F32),
    )
    ins = [x, cache_k, cache_v, state_conv, state_h3]
    attn, lru_out, k_new, v_new, conv_new, h_new = pl.pallas_call(
        functools.partial(_sample_front_kernel, n_heads=n_heads),
        out_shape=front_out,
        grid=(nbatch // group,),
        in_specs=[grp(a.shape) for a in ins] + [_const_spec(bias)]
        + [_const_spec(a) for a in front_w],
        out_specs=tuple(grp(o.shape) for o in front_out),
        scratch_shapes=[pltpu.VMEM((group * (SUBLANES + t), lw), F32)],
        compiler_params=pltpu.CompilerParams(
            dimension_semantics=("arbitrary",), vmem_limit_bytes=VMEM_LIMIT),
        name="sample_front",
    )(*ins, bias, *front_w)

    rows = nbatch * t
    back_w = [weights[n] for n in BACK_NAMES]
    flat = [x.reshape(rows, d), attn.reshape(rows, aw), lru_out.reshape(rows, lw),
            p.reshape(rows, p.shape[-1])]
    y = pl.pallas_call(
        functools.partial(_sample_back_kernel, final_norm=final_norm),
        out_shape=jax.ShapeDtypeStruct((rows, d), F32),
        grid=(rows // back_rows,),
        in_specs=[pl.BlockSpec((back_rows, a.shape[1]), lambda i: (i, 0)) for a in flat]
        + [_const_spec(a) for a in back_w],
        out_specs=pl.BlockSpec((back_rows, d), lambda i: (i, 0)),
        compiler_params=pltpu.CompilerParams(
            dimension_semantics=("arbitrary",), vmem_limit_bytes=VMEM_LIMIT),
        name="sample_back",
    )(*flat, *back_w)
    return y.reshape(nbatch, t, d), k_new, v_new, conv_new, h_new


def _block_diag(wb):
    n, kk, jj = wb.shape
    eye = jnp.eye(n, dtype=wb.dtype)
    return (wb[:, :, None, :] * eye[:, None, :, None]).reshape(n * kk, n * jj)


PROMPT_TILE = 256
SAMPLE_GROUP = 4
SAMPLE_BACK_ROWS = 256


def kernel(x_prompt, x_sample, p_prompt, p_sample, cache_k, cache_v, state_conv, state_h, g_mix, w_in, conv_w, conv_b, w_rgate, b_rgate, w_igate, b_igate, lru_lambda, rel_bias_table, g_attn_out, g_lru_out, w_out, g_ffn, w_ffn_gate, w_ffn_up, w_ffn_down, g_ple, w_ple_gate, w_ple_proj, g_final):
    depth = w_in.shape[0]
    hp, hs = x_prompt, x_sample
    outs = [[] for _ in range(8)]
    for i in range(depth):
        row = lambda a: a[i].reshape(1, -1).astype(F32)
        weights = {
            "g_mix": row(g_mix), "w_in": w_in[i].astype(BF16),
            "conv_w": conv_w[i].astype(F32), "conv_b": row(conv_b),
            "w_r": _block_diag(w_rgate[i]).astype(BF16), "b_r": row(b_rgate),
            "w_i": _block_diag(w_igate[i]).astype(BF16), "b_i": row(b_igate),
            "lam": row(lru_lambda),
            "g_attn_out": row(g_attn_out), "g_lru_out": row(g_lru_out),
            "w_out": w_out[i].astype(BF16), "g_ffn": row(g_ffn),
            "w_ffn_gate": w_ffn_gate[i].astype(BF16), "w_ffn_up": w_ffn_up[i].astype(BF16),
            "w_ffn_down": w_ffn_down[i].astype(BF16), "g_ple": row(g_ple),
            "w_ple_gate": w_ple_gate[i].astype(BF16), "w_ple_proj": w_ple_proj[i].astype(BF16),
            "g_final": g_final.reshape(1, -1).astype(F32),
        }
        final = i == depth - 1
        bias = _bias_table(rel_bias_table[i])
        n_heads = rel_bias_table.shape[1]
        aw = n_heads * HEAD_DIM

        hp, k1, v1, c1, r1 = _prompt_layer(
            hp, p_prompt[i], bias.reshape(n_heads // 2, 2 * CHUNK, BAND), weights, final,
            PROMPT_TILE)
        nb, nc = cache_k.shape[1:3]
        hs, k2, v2, c2, r2 = _sample_layer(
            hs, p_sample[i], cache_k[i].reshape(nb, nc * n_heads, HEAD_DIM),
            cache_v[i].reshape(nb, nc * n_heads, HEAD_DIM), state_conv[i], state_h[i], bias,
            weights, final, SAMPLE_GROUP, SAMPLE_BACK_ROWS)
        bp, keep = k1.shape[:2]
        ts = k2.shape[1]
        for lst, val in zip(outs, (
                k1.reshape(bp, keep, n_heads, HEAD_DIM), v1.reshape(bp, keep, n_heads, HEAD_DIM),
                c1, r1.reshape(bp, -1),
                k2.reshape(nb, ts, n_heads, HEAD_DIM), v2.reshape(nb, ts, n_heads, HEAD_DIM),
                c2, r2.reshape(nb, -1))):
            lst.append(val)
    return (hp, hs) + tuple(jnp.stack(l) for l in outs)
```

```python
import functools

import jax
import jax.numpy as jnp
from jax import lax
from jax.experimental import pallas as pl
from jax.experimental.pallas import tpu as pltpu

CHUNK = 64
LEFT_CHUNKS = 8
PAD = LEFT_CHUNKS * CHUNK
BAND = PAD + CHUNK
HEAD_DIM = 64
CONV_WIDTH = 4
LRU_C = 8.0
REL_CLIP = 128
EPS = 1e-6
NEG = -1e30
SCALE = HEAD_DIM ** -0.5
LOG2E = 1.4426950408889634
LANES = 128
SUBLANES = 8
MXU_DIM = 256
EXT = 640
VMEM_LIMIT = 60 * 1024 * 1024
FFN_CHUNK = 2 * MXU_DIM

F32 = jnp.float32
BF16 = jnp.bfloat16


def _dot(a, b):
    return jnp.dot(a, b, preferred_element_type=F32)


def _dot_nt(a, b):
    return lax.dot_general(a, b, (((1,), (1,)), ((), ())), preferred_element_type=F32)


def _rms(x, g):
    ms = jnp.mean(x * x, axis=-1, keepdims=True)
    return (x * lax.rsqrt(ms + EPS)) * g


def _sigmoid(x):
    return 1.0 / (1.0 + jnp.exp2(x * (-LOG2E)))


def _gelu_tanh(x):
    c = 0.7978845608028654
    return 0.5 * x * (1.0 + jnp.tanh(c * (x + 0.044715 * (x * x * x))))


def _shift_rows(x, d, fill, seg):
    rows = x.shape[0]
    if seg == rows and d % SUBLANES == 0:
        return jnp.concatenate([jnp.full((d, x.shape[1]), fill, x.dtype), x[:rows - d]], axis=0)
    rolled = pltpu.roll(x, d, axis=0)
    row = lax.broadcasted_iota(jnp.int32, x.shape, 0)
    return jnp.where((row & (seg - 1)) >= d, rolled, fill)


def _scan_distances(seg):
    return [1 << b for b in range(seg.bit_length() - 1)]


def _scan_step(a, u, d, seg):
    u = u + a * _shift_rows(u, d, 0.0, seg)
    if 2 * d < seg:
        a = a * _shift_rows(a, d, 1.0, seg)
    return a, u


def _sqrt_nonneg(y):
    return jnp.where(y > 0.0, y * lax.rsqrt(y), 0.0)


def _lru_inputs(xc, r_pre, i_pre, w):
    r = _sigmoid(r_pre + w["b_r"][...])
    ig = _sigmoid(i_pre + w["b_i"][...])
    z = -w["lam"][...]
    softplus = jnp.maximum(z, 0.0) + jnp.log1p(jnp.exp(-jnp.abs(z)))
    a = jnp.exp2(r * ((-LRU_C * LOG2E) * softplus))
    u = _sqrt_nonneg(1.0 - a * a) * (ig * xc)
    return a, u


def _rglru(xc, gr, h0_rows, seg, w):
    xcb = xc.astype(BF16)
    a, u = _lru_inputs(xc, _dot(xcb, w["w_r"][...]), _dot(xcb, w["w_i"][...]), w)
    row = lax.broadcasted_iota(jnp.int32, xc.shape, 0)
    u = u + jnp.where((row & (seg - 1)) == 0, a * h0_rows, 0.0)
    for d in _scan_distances(seg):
        a, u = _scan_step(a, u, d, seg)
    return u, u * _gelu_tanh(gr)


def _scan_blocked(a, u, h0):
    for d in _scan_distances(SUBLANES):
        u = u + a * _shift_rows(u, d, 0.0, SUBLANES)
        a = a * _shift_rows(a, d, 1.0, SUBLANES)
    carry = jnp.broadcast_to(h0, (SUBLANES, a.shape[1]))
    groups = []
    for g in range(a.shape[0] // SUBLANES):
        rows = slice(g * SUBLANES, (g + 1) * SUBLANES)
        h = u[rows] + a[rows] * carry
        groups.append(h)
        carry = jnp.broadcast_to(h[SUBLANES - 1:SUBLANES], h.shape)
    return jnp.concatenate(groups, axis=0)


def _mix_out(h, attn, lru_out, w):
    mixed = jnp.concatenate(
        [_rms(attn, w["g_attn_out"][...]), _rms(lru_out, w["g_lru_out"][...])], axis=-1)
    return h + _dot(mixed.astype(BF16), w["w_out"][...])


def _ffn_piece(hn, acc, w, c0, c1):
    gate = _dot(hn, w["w_ffn_gate"][:, c0:c1])
    up = _dot(hn, w["w_ffn_up"][:, c0:c1])
    act = (gate * _sigmoid(gate)) * up
    return acc + _dot(act.astype(BF16), w["w_ffn_down"][c0:c1, :])


def _ffn_bounds(d_ff, step):
    return [(c, min(c + step, d_ff)) for c in range(0, d_ff, step)]


def _ple_out(h, p, w, final_norm):
    gate = _sigmoid(_dot(_rms(h, w["g_ple"][...]).astype(BF16), w["w_ple_gate"][...]))
    h = h + _dot(p.astype(BF16), w["w_ple_proj"][...]) * gate
    if final_norm:
        h = _rms(h, w["g_final"][...])
    return h


def _dense_tail(h, attn, lru_out, p, w, final_norm):
    h = _mix_out(h, attn, lru_out, w)
    hn = _rms(h, w["g_ffn"][...]).astype(BF16)
    d_ff = w["w_ffn_gate"].shape[1]
    for c0, c1 in _ffn_bounds(d_ff, -(-d_ff // (2 * LANES)) * LANES):
        h = _ffn_piece(hn, h, w, c0, c1)
    return _ple_out(h, p, w, final_norm)


def _softmax_pv(scores, values, transposed=None):
    transposed = transposed or [False] * len(values)
    m = functools.reduce(jnp.maximum, [jnp.max(s, axis=-1, keepdims=True) for s in scores])
    num = None
    den = None
    for s, v, v_t in zip(scores, values, transposed):
        e = jnp.exp2(s - m)
        l = jnp.sum(e, axis=-1, keepdims=True)
        o = _dot_nt(e.astype(BF16), v) if v_t else _dot(e.astype(BF16), v)
        num = o if num is None else num + o
        den = l if den is None else den + l
    return num / den


WEIGHT_NAMES = (
    "g_mix", "w_in", "conv_w", "conv_b", "w_r", "b_r", "w_i", "b_i", "lam",
    "g_attn_out", "g_lru_out", "w_out", "g_ffn", "w_ffn_gate", "w_ffn_up", "w_ffn_down",
    "g_ple", "w_ple_gate", "w_ple_proj", "g_final")


def _order_after(x, *deps):
    bits = None
    for dep in deps:
        b = pltpu.bitcast(dep[:SUBLANES, :LANES], jnp.uint32)
        bits = b if bits is None else bits | b
    zero = pltpu.bitcast((bits >> 16) >> 16, F32)
    head = x[:SUBLANES] + jnp.concatenate([zero] * (x.shape[1] // LANES), axis=1)
    return jnp.concatenate([head, x[SUBLANES:]], axis=0)


def _project(x, w):
    xn = _rms(x, w["g_mix"][...]).astype(BF16)
    proj = _dot(xn, w["w_in"][...])
    aw = w["g_attn_out"].shape[1]
    lw = w["g_lru_out"].shape[1]
    q = proj[:, :aw] * (SCALE * LOG2E)
    k = proj[:, aw:2 * aw]
    v = proj[:, 2 * aw:3 * aw]
    xr = proj[:, 3 * aw:3 * aw + lw]
    gr = proj[:, 3 * aw + lw:]
    return q, k, v, xr, gr


def _conv(xr_buf, base, rows, w):
    cw = w["conv_w"]
    ext = xr_buf[base - SUBLANES:base + rows, :]
    xc = w["conv_b"][...] + ext[SUBLANES:] * cw[CONV_WIDTH - 1:CONV_WIDTH, :]
    for back in range(1, CONV_WIDTH):
        j = CONV_WIDTH - 1 - back
        xc = xc + pltpu.roll(ext, back, axis=0)[SUBLANES:] * cw[j:j + 1, :]
    return xc


def _bias_kernel(ext_ref, bias_ref):
    n_heads = ext_ref.shape[0]
    for h in range(n_heads):
        rows = jnp.broadcast_to(ext_ref[h], (CHUNK, EXT))
        rolled = pltpu.roll(rows, EXT - (CHUNK - 1), axis=1, stride=1, stride_axis=0)
        bias_ref[h] = rolled[:, :BAND] * LOG2E


def _bias_table(rel_table):
    n_heads, rel_size = rel_table.shape
    assert rel_size == REL_CLIP + CHUNK
    left = (BAND - 1) - REL_CLIP
    ext = jnp.pad(rel_table.astype(F32), ((0, 0), (left, EXT - left - rel_size)), mode="edge")
    return pl.pallas_call(
        _bias_kernel,
        out_shape=jax.ShapeDtypeStruct((n_heads, CHUNK, BAND), F32),
        name="rel_bias_expand",
    )(ext.reshape(n_heads, 1, EXT))


def _prompt_kernel(*refs, tile, seq, final_norm):
    x_ref, xprev_ref, pprev_ref, bias_ref = refs[:4]
    nw = len(WEIGHT_NAMES)
    w = dict(zip(WEIGHT_NAMES, refs[4:4 + nw]))
    y_ref, ko_ref, vo_ref, convo_ref, ho_ref = refs[4 + nw:9 + nw]
    kbuf, vbuf, q_scr, attn_scr, gr_scr, xr_buf, h_scr = refs[9 + nw:]

    t = pl.program_id(1)
    n_tiles = seq // tile
    n_pairs = kbuf.shape[1] // LANES

    @pl.when(t == 0)
    def _():
        kbuf[0:PAD, :] = jnp.zeros((PAD, kbuf.shape[1]), BF16)
        vbuf[0:PAD, :] = jnp.zeros((PAD, vbuf.shape[1]), BF16)
        xr_buf[...] = jnp.zeros(xr_buf.shape, F32)
        gr_scr[...] = jnp.zeros(gr_scr.shape, F32)
        h_scr[...] = jnp.zeros(h_scr.shape, F32)
        attn_scr[...] = jnp.zeros(attn_scr.shape, F32)

    st = {}

    xc = _conv(xr_buf, SUBLANES, tile, w)
    xcb = xc.astype(BF16)
    r_pre = _dot(xcb, w["w_r"][...])
    i_pre = _dot(xcb, w["w_i"][...])
    h0 = h_scr[...]
    gr_prev = gr_scr[...]
    xr_tail = xr_buf[tile:tile + SUBLANES, :]

    q, k, v, xr, gr = _project(_order_after(x_ref[0], r_pre, i_pre), w)

    a, u = _lru_inputs(xc, r_pre, i_pre, w)
    u = _scan_blocked(a, u, h0)
    lru_out = u * _gelu_tanh(gr_prev)

    q_scr[...] = q.astype(BF16)
    kbuf[PAD:PAD + tile, :] = k.astype(BF16)
    vbuf[PAD:PAD + tile, :] = v.astype(BF16)
    h_scr[...] = jnp.where(t > 0, u[tile - 1:tile, :], 0.0)

    xr_buf[0:SUBLANES, :] = xr_tail
    xr_buf[SUBLANES:SUBLANES + tile, :] = xr
    gr_scr[...] = gr

    keep_from = seq - PAD

    @pl.when(jnp.logical_and(t < n_tiles, t * tile >= keep_from))
    def _():
        off = pl.multiple_of(t * tile - keep_from, tile)
        ko_ref[0, pl.ds(off, tile), :] = k
        vo_ref[0, pl.ds(off, tile), :] = v

    @pl.when(t == n_tiles)
    def _():
        convo_ref[0] = xr_buf[SUBLANES - (CONV_WIDTH - 1):SUBLANES, :]
        ho_ref[0] = h_scr[...]

    st["h"] = _mix_out(xprev_ref[0], attn_scr[...], lru_out, w)

    rows_w = tile // 16
    dense = []

    def ffn_norm():
        st["hn"] = _rms(st["h"], w["g_ffn"][...]).astype(BF16)

    def ffn_gate(c0, c1):
        st["g"] = _dot(st["hn"], w["w_ffn_gate"][:, c0:c1])

    def ffn_up(c0, c1):
        g = st.pop("g")
        st["act"] = ((g * _sigmoid(g)) * _dot(st["hn"], w["w_ffn_up"][:, c0:c1])).astype(BF16)

    def ffn_down(c0, c1):
        st["h"] = st["h"] + _dot(st.pop("act"), w["w_ffn_down"][c0:c1, :])

    d_model = x_ref.shape[2]
    d_ff = w["w_ffn_gate"].shape[1]
    dense += [(0, ffn_norm)]
    for c0, c1 in _ffn_bounds(d_ff, FFN_CHUNK):
        wt = rows_w * (d_model // MXU_DIM) * -(-(c1 - c0) // MXU_DIM)
        dense += [(wt, functools.partial(ffn_gate, c0, c1)), (wt, functools.partial(ffn_up, c0, c1)),
                  (wt, functools.partial(ffn_down, c0, c1))]

    def ple_norm():
        st["hpn"] = _rms(st["h"], w["g_ple"][...]).astype(BF16)
        st["pg"] = []

    def ple_gate(c0, c1):
        st["pg"].append(_sigmoid(_dot(st["hpn"], w["w_ple_gate"][:, c0:c1])))

    def ple_out():
        h = st["h"] + _dot(pprev_ref[0].astype(BF16), w["w_ple_proj"][...]) * jnp.concatenate(
            st.pop("pg"), axis=-1)
        if final_norm:
            h = _rms(h, w["g_final"][...])
        y_ref[0] = h

    dense += [(0, ple_norm)]
    for c0, c1 in _ffn_bounds(d_model, MXU_DIM):
        dense += [(rows_w * (d_model // MXU_DIM), functools.partial(ple_gate, c0, c1))]
    dense += [(rows_w * (d_model // MXU_DIM), ple_out)]

    lane = lax.broadcasted_iota(jnp.int32, (CHUNK, LANES), 1)
    low_half = lane < HEAD_DIM
    kidx = lax.broadcasted_iota(jnp.int32, (1, BAND), 1)
    units = [(j, pr) for j in range(tile // CHUNK) for pr in range(n_pairs)]

    def scores(j, pr):
        lanes = slice(pr * LANES, (pr + 1) * LANES)
        q2 = q_scr[j * CHUNK:(j + 1) * CHUNK, lanes]
        zero = jnp.zeros_like(q2)
        qs = jnp.concatenate([jnp.where(low_half, q2, zero), jnp.where(low_half, zero, q2)],
                             axis=0)
        s = _dot_nt(qs, kbuf[j * CHUNK:j * CHUNK + BAND, lanes]) + bias_ref[pr]
        valid = (t * tile + j * CHUNK + kidx - PAD) >= PAD
        return jnp.where(valid, s, NEG)

    def attend(j, pr, s):
        lanes = slice(pr * LANES, (pr + 1) * LANES)
        o = _softmax_pv([s], [vbuf[j * CHUNK:j * CHUNK + BAND, lanes]])
        st["attn", j, pr] = jnp.where(low_half, o[:CHUNK], o[CHUNK:])

    total_w = sum(wt for wt, _ in dense)
    done_w = 0
    pending = list(dense)
    s_cur = scores(*units[0])
    for i, unit in enumerate(units):
        s_next = scores(*units[i + 1]) if i + 1 < len(units) else None
        while pending and (done_w < total_w * (i + 1) // len(units) or i + 1 == len(units)):
            wt, piece = pending.pop(0)
            piece()
            done_w += wt
        attend(*unit, s_cur)
        s_cur = s_next

    for j, pr in units:
        attn_scr[j * CHUNK:(j + 1) * CHUNK, pr * LANES:(pr + 1) * LANES] = st.pop(("attn", j, pr))

    for r in range(0, PAD, tile):
        n = min(tile, PAD - r)
        kbuf[r:r + n, :] = kbuf[r + tile:r + tile + n, :]
        vbuf[r:r + n, :] = vbuf[r + tile:r + tile + n, :]


def _const_spec(arr):
    nd = arr.ndim
    return pl.BlockSpec(arr.shape, lambda *_: (0,) * nd, pipeline_mode=pl.Buffered(1))


def _prompt_layer(x, p, bias, weights, final_norm, tile):
    b, seq, d = x.shape
    aw = weights["g_attn_out"].shape[1]
    lw = weights["g_lru_out"].shape[1]
    assert seq % tile == 0 and PAD % tile == 0 and seq >= PAD and tile % CHUNK == 0
    assert tile & (tile - 1) == 0 and aw % LANES == 0
    n_tiles = seq // tile
    wlist = [weights[n] for n in WEIGHT_NAMES]
    kern = functools.partial(_prompt_kernel, tile=tile, seq=seq, final_norm=final_norm)
    out_shape = (
        jax.ShapeDtypeStruct((b, seq, d), F32),
        jax.ShapeDtypeStruct((b, PAD, aw), F32),
        jax.ShapeDtypeStruct((b, PAD, aw), F32),
        jax.ShapeDtypeStruct((b, CONV_WIDTH - 1, lw), F32),
        jax.ShapeDtypeStruct((b, 1, lw), F32),
    )
    cur = lambda i, j: (i, jnp.minimum(j, n_tiles - 1), 0)
    prev = lambda i, j: (i, jnp.maximum(j - 1, 0), 0)
    in_specs = [
        pl.BlockSpec((1, tile, d), cur),
        pl.BlockSpec((1, tile, d), prev),
        pl.BlockSpec((1, tile, p.shape[-1]), prev),
        _const_spec(bias),
    ] + [_const_spec(a) for a in wlist]
    out_specs = (
        pl.BlockSpec((1, tile, d), prev),
        pl.BlockSpec((1, PAD, aw), lambda i, j: (i, 0, 0)),
        pl.BlockSpec((1, PAD, aw), lambda i, j: (i, 0, 0)),
        pl.BlockSpec((1, CONV_WIDTH - 1, lw), lambda i, j: (i, 0, 0)),
        pl.BlockSpec((1, 1, lw), lambda i, j: (i, 0, 0)),
    )
    scratch = [
        pltpu.VMEM((PAD + tile, aw), BF16),
        pltpu.VMEM((PAD + tile, aw), BF16),
        pltpu.VMEM((tile, aw), BF16),
        pltpu.VMEM((tile, aw), F32),
        pltpu.VMEM((tile, lw), F32),
        pltpu.VMEM((SUBLANES + tile, lw), F32),
        pltpu.VMEM((1, lw), F32),
    ]
    return pl.pallas_call(
        kern,
        out_shape=out_shape,
        grid=(b, n_tiles + 1),
        in_specs=in_specs,
        out_specs=out_specs,
        scratch_shapes=scratch,
        compiler_params=pltpu.CompilerParams(
            dimension_semantics=("arbitrary", "arbitrary"), vmem_limit_bytes=VMEM_LIMIT),
        name="prompt_layer",
    )(x, x, p, bias, *wlist)


FRONT_NAMES = ("g_mix", "w_in", "conv_w", "conv_b", "w_r", "b_r", "w_i", "b_i", "lam",
               "g_attn_out", "g_lru_out")
BACK_NAMES = ("g_attn_out", "g_lru_out", "w_out", "g_ffn", "w_ffn_gate", "w_ffn_up",
              "w_ffn_down", "g_ple", "w_ple_gate", "w_ple_proj", "g_final")


def _sample_front_kernel(*refs, n_heads):
    x_ref, ck_ref, cv_ref, sconv_ref, sh_ref, bias_ref = refs[:6]
    nw = len(FRONT_NAMES)
    w = dict(zip(FRONT_NAMES, refs[6:6 + nw]))
    attn_ref, lru_ref, ko_ref, vo_ref, convo_ref, ho_ref = refs[6 + nw:12 + nw]
    (xr_buf,) = refs[12 + nw:]

    nb, t, d = x_ref.shape
    n_cache = ck_ref.shape[3]
    aw = n_heads * HEAD_DIM
    rows = nb * t

    x = x_ref[...].reshape(rows, d)
    q, k, v, xr, gr = _project(x, w)
    ko_ref[...] = k.reshape(nb, t, aw)
    vo_ref[...] = v.reshape(nb, t, aw)
    qb = q.astype(BF16)
    kb_new = k.astype(BF16)
    vb_new = v.astype(BF16)

    units = [(b, h) for b in range(nb) for h in range(n_heads)]

    def scores(b, h):
        rs = slice(b * t, (b + 1) * t)
        cols = slice(h * HEAD_DIM, (h + 1) * HEAD_DIM)
        qh = qb[rs, cols]
        kc_t = ck_ref[b, h].astype(BF16)
        s_c = _dot(qh, kc_t) + bias_ref[h, 0:t, PAD - n_cache:PAD]
        s_n = _dot_nt(qh, kb_new[rs, cols]) + bias_ref[h, 0:t, PAD:PAD + t]
        return s_c, s_n

    outs = {}
    s_cur = scores(*units[0])
    for i, (b, h) in enumerate(units):
        s_next = scores(*units[i + 1]) if i + 1 < len(units) else None
        vc_t = cv_ref[b, h].astype(BF16)
        vn = vb_new[b * t:(b + 1) * t, h * HEAD_DIM:(h + 1) * HEAD_DIM]
        outs[b, h] = _softmax_pv(list(s_cur), [vc_t, vn], [True, False])
        s_cur = s_next
    for b in range(nb):
        attn_ref[b] = jnp.concatenate([outs[b, h] for h in range(n_heads)], axis=-1)

    seg_rows = SUBLANES + t
    xcs = []
    for b in range(nb):
        base = b * seg_rows
        xr_buf[base + SUBLANES - (CONV_WIDTH - 1):base + SUBLANES, :] = sconv_ref[b]
        xr_buf[base + SUBLANES:base + seg_rows, :] = xr[b * t:(b + 1) * t, :]
        xcs.append(_conv(xr_buf, base + SUBLANES, t, w))
        convo_ref[b] = xr_buf[base + seg_rows - (CONV_WIDTH - 1):base + seg_rows, :]
    xc = jnp.concatenate(xcs, axis=0)
    h0_rows = jnp.concatenate(
        [jnp.broadcast_to(sh_ref[b], (t, sh_ref.shape[2])) for b in range(nb)], axis=0)
    hs, lru_out = _rglru(xc, gr, h0_rows, t, w)
    for b in range(nb):
        ho_ref[b] = hs[(b + 1) * t - 1:(b + 1) * t, :]
    lru_ref[...] = lru_out.reshape(nb, t, lru_out.shape[1])


def _sample_back_kernel(*refs, final_norm):
    x_ref, attn_ref, lru_ref, p_ref = refs[:4]
    nw = len(BACK_NAMES)
    w = dict(zip(BACK_NAMES, refs[4:4 + nw]))
    y_ref = refs[4 + nw]
    y_ref[...] = _dense_tail(x_ref[...], attn_ref[...], lru_ref[...], p_ref[...], w, final_norm)


def _sample_layer(x, p, cache_k, cache_v, state_conv, state_h, bias, weights, final_norm, group,
                  back_rows):
    nbatch, t, d = x.shape
    n_heads = bias.shape[0]
    aw = n_heads * HEAD_DIM
    n_cache = cache_k.shape[3]
    lw = weights["g_lru_out"].shape[1]
    assert nbatch % group == 0 and t % SUBLANES == 0 and t & (t - 1) == 0
    assert CONV_WIDTH - 1 <= t <= CHUNK and n_cache <= PAD and cache_k.shape[1:3] == (n_heads, HEAD_DIM)
    assert (nbatch * t) % back_rows == 0
    state_h3 = state_h.reshape(nbatch, 1, lw)

    def grp(shape):
        nd = len(shape)
        return pl.BlockSpec((group,) + tuple(shape[1:]), lambda i: (i,) + (0,) * (nd - 1))

    front_w = [weights[n] for n in FRONT_NAMES]
    front_out = (
        jax.ShapeDtypeStruct((nbatch, t, aw), F32),
        jax.ShapeDtypeStruct((nbatch, t, lw), F32),
        jax.ShapeDtypeStruct((nbatch, t, aw), F32),
        jax.ShapeDtypeStruct((nbatch, t, aw), F32),
        jax.ShapeDtypeStruct((nbatch, CONV_WIDTH - 1, lw), F32),
        jax.ShapeDtypeStruct((nbatch, 1, lw), F32),
    )
    ins = [x, cache_k, cache_v, state_conv, state_h3]
    attn, lru_out, k_new, v_new, conv_new, h_new = pl.pallas_call(
        functools.partial(_sample_front_kernel, n_heads=n_heads),
        out_shape=front_out,
        grid=(nbatch // group,),
        in_specs=[grp(a.shape) for a in ins] + [_const_spec(bias)]
        + [_const_spec(a) for a in front_w],
        out_specs=tuple(grp(o.shape) for o in front_out),
        scratch_shapes=[pltpu.VMEM((group * (SUBLANES + t), lw), F32)],
        compiler_params=pltpu.CompilerParams(
            dimension_semantics=("arbitrary",), vmem_limit_bytes=VMEM_LIMIT),
        name="sample_front",
    )(*ins, bias, *front_w)

    rows = nbatch * t
    back_w = [weights[n] for n in BACK_NAMES]
    flat = [x.reshape(rows, d), attn.reshape(rows, aw), lru_out.reshape(rows, lw),
            p.reshape(rows, p.shape[-1])]
    y = pl.pallas_call(
        functools.partial(_sample_back_kernel, final_norm=final_norm),
        out_shape=jax.ShapeDtypeStruct((rows, d), F32),
        grid=(rows // back_rows,),
        in_specs=[pl.BlockSpec((back_rows, a.shape[1]), lambda i: (i, 0)) for a in flat]
        + [_const_spec(a) for a in back_w],
        out_specs=pl.BlockSpec((back_rows, d), lambda i: (i, 0)),
        compiler_params=pltpu.CompilerParams(
            dimension_semantics=("arbitrary",), vmem_limit_bytes=VMEM_LIMIT),
        name="sample_back",
    )(*flat, *back_w)
    return y.reshape(nbatch, t, d), k_new, v_new, conv_new, h_new


def _block_diag(wb):
    n, kk, jj = wb.shape
    eye = jnp.eye(n, dtype=wb.dtype)
    return (wb[:, :, None, :] * eye[:, None, :, None]).reshape(n * kk, n * jj)


PROMPT_TILE = 256
SAMPLE_GROUP = 8
SAMPLE_BACK_ROWS = 256


def kernel(x_prompt, x_sample, p_prompt, p_sample, cache_k, cache_v, state_conv, state_h, g_mix, w_in, conv_w, conv_b, w_rgate, b_rgate, w_igate, b_igate, lru_lambda, rel_bias_table, g_attn_out, g_lru_out, w_out, g_ffn, w_ffn_gate, w_ffn_up, w_ffn_down, g_ple, w_ple_gate, w_ple_proj, g_final):
    depth = w_in.shape[0]
    hp, hs = x_prompt, x_sample
    outs = [[] for _ in range(8)]
    for i in range(depth):
        row = lambda a: a[i].reshape(1, -1).astype(F32)
        weights = {
            "g_mix": row(g_mix), "w_in": w_in[i].astype(BF16),
            "conv_w": conv_w[i].astype(F32), "conv_b": row(conv_b),
            "w_r": _block_diag(w_rgate[i]).astype(BF16), "b_r": row(b_rgate),
            "w_i": _block_diag(w_igate[i]).astype(BF16), "b_i": row(b_igate),
            "lam": row(lru_lambda),
            "g_attn_out": row(g_attn_out), "g_lru_out": row(g_lru_out),
            "w_out": w_out[i].astype(BF16), "g_ffn": row(g_ffn),
            "w_ffn_gate": w_ffn_gate[i].astype(BF16), "w_ffn_up": w_ffn_up[i].astype(BF16),
            "w_ffn_down": w_ffn_down[i].astype(BF16), "g_ple": row(g_ple),
            "w_ple_gate": w_ple_gate[i].astype(BF16), "w_ple_proj": w_ple_proj[i].astype(BF16),
            "g_final": g_final.reshape(1, -1).astype(F32),
        }
        final = i == depth - 1
        bias = _bias_table(rel_bias_table[i])
        n_heads = rel_bias_table.shape[1]
        aw = n_heads * HEAD_DIM

        hp, k1, v1, c1, r1 = _prompt_layer(
            hp, p_prompt[i], bias.reshape(n_heads // 2, 2 * CHUNK, BAND), weights, final,
            PROMPT_TILE)
        nb, nc = cache_k.shape[1:3]
        hs, k2, v2, c2, r2 = _sample_layer(
            hs, p_sample[i], jnp.transpose(cache_k[i], (0, 2, 3, 1)),
            jnp.transpose(cache_v[i], (0, 2, 3, 1)), state_conv[i], state_h[i], bias,
            weights, final, SAMPLE_GROUP, SAMPLE_BACK_ROWS)
        bp, keep = k1.shape[:2]
        ts = k2.shape[1]
        for lst, val in zip(outs, (
                k1.reshape(bp, keep, n_heads, HEAD_DIM), v1.reshape(bp, keep, n_heads, HEAD_DIM),
                c1, r1.reshape(bp, -1),
                k2.reshape(nb, ts, n_heads, HEAD_DIM), v2.reshape(nb, ts, n_heads, HEAD_DIM),
                c2, r2.reshape(nb, -1))):
            lst.append(val)
    return (hp, hs) + tuple(jnp.stack(l) for l in outs)
```

```python
import functools

import jax
import jax.numpy as jnp
from jax import lax
from jax.experimental import pallas as pl
from jax.experimental.pallas import tpu as pltpu

CHUNK = 64
LEFT_CHUNKS = 8
PAD = LEFT_CHUNKS * CHUNK
BAND = PAD + CHUNK
HEAD_DIM = 64
CONV_WIDTH = 4
LRU_C = 8.0
REL_CLIP = 128
EPS = 1e-6
NEG = -1e30
SCALE = HEAD_DIM ** -0.5
LOG2E = 1.4426950408889634
LANES = 128
SUBLANES = 8
MXU_DIM = 256
EXT = 640
VMEM_LIMIT = 60 * 1024 * 1024
FFN_CHUNK = 4 * MXU_DIM
DENSE_EARLY = 0

F32 = jnp.float32
BF16 = jnp.bfloat16


def _dot(a, b):
    return jnp.dot(a, b, preferred_element_type=F32)


def _dot_nt(a, b):
    return lax.dot_general(a, b, (((1,), (1,)), ((), ())), preferred_element_type=F32)


def _rms(x, g):
    ms = jnp.mean(x * x, axis=-1, keepdims=True)
    return (x * lax.rsqrt(ms + EPS)) * g


def _sigmoid(x):
    return 1.0 / (1.0 + jnp.exp2(x * (-LOG2E)))


def _gelu_tanh(x):
    c = 0.7978845608028654
    return 0.5 * x * (1.0 + jnp.tanh(c * (x + 0.044715 * (x * x * x))))


def _shift_rows(x, d, fill, seg):
    rows = x.shape[0]
    if seg == rows and d % SUBLANES == 0:
        return jnp.concatenate([jnp.full((d, x.shape[1]), fill, x.dtype), x[:rows - d]], axis=0)
    rolled = pltpu.roll(x, d, axis=0)
    row = lax.broadcasted_iota(jnp.int32, x.shape, 0)
    return jnp.where((row & (seg - 1)) >= d, rolled, fill)


def _scan_distances(seg):
    return [1 << b for b in range(seg.bit_length() - 1)]


def _scan_step(a, u, d, seg):
    u = u + a * _shift_rows(u, d, 0.0, seg)
    if 2 * d < seg:
        a = a * _shift_rows(a, d, 1.0, seg)
    return a, u


def _sqrt_nonneg(y):
    return jnp.where(y > 0.0, y * lax.rsqrt(y), 0.0)


def _lru_inputs(xc, r_pre, i_pre, w):
    r = _sigmoid(r_pre + w["b_r"][...])
    ig = _sigmoid(i_pre + w["b_i"][...])
    z = -w["lam"][...]
    softplus = jnp.maximum(z, 0.0) + jnp.log1p(jnp.exp(-jnp.abs(z)))
    a = jnp.exp2(r * ((-LRU_C * LOG2E) * softplus))
    u = _sqrt_nonneg(1.0 - a * a) * (ig * xc)
    return a, u


def _gate_dot(xcb, w_ref):
    n = w_ref.shape[0]
    if n % MXU_DIM:
        return _dot(xcb, w_ref[...])
    return jnp.concatenate(
        [_dot(xcb[:, c:c + MXU_DIM], w_ref[c:c + MXU_DIM, c:c + MXU_DIM])
         for c in range(0, n, MXU_DIM)], axis=1)


def _rglru(xc, gr, h0_rows, seg, w):
    xcb = xc.astype(BF16)
    a, u = _lru_inputs(xc, _gate_dot(xcb, w["w_r"]), _gate_dot(xcb, w["w_i"]), w)
    row = lax.broadcasted_iota(jnp.int32, xc.shape, 0)
    u = u + jnp.where((row & (seg - 1)) == 0, a * h0_rows, 0.0)
    for d in _scan_distances(seg):
        a, u = _scan_step(a, u, d, seg)
    return u, u * _gelu_tanh(gr)


def _scan_blocked(a, u, h0):
    for d in _scan_distances(SUBLANES):
        u = u + a * _shift_rows(u, d, 0.0, SUBLANES)
        a = a * _shift_rows(a, d, 1.0, SUBLANES)
    carry = jnp.broadcast_to(h0, (SUBLANES, a.shape[1]))
    groups = []
    for g in range(a.shape[0] // SUBLANES):
        rows = slice(g * SUBLANES, (g + 1) * SUBLANES)
        h = u[rows] + a[rows] * carry
        groups.append(h)
        carry = jnp.broadcast_to(h[SUBLANES - 1:SUBLANES], h.shape)
    return jnp.concatenate(groups, axis=0)


def _mix_out(h, attn, lru_out, w):
    mixed = jnp.concatenate(
        [_rms(attn, w["g_attn_out"][...]), _rms(lru_out, w["g_lru_out"][...])], axis=-1)
    return h + _dot(mixed.astype(BF16), w["w_out"][...])


def _ffn_piece(hn, acc, w, c0, c1):
    gate = _dot(hn, w["w_ffn_gate"][:, c0:c1])
    up = _dot(hn, w["w_ffn_up"][:, c0:c1])
    act = (gate * _sigmoid(gate)) * up
    return acc + _dot(act.astype(BF16), w["w_ffn_down"][c0:c1, :])


def _ffn_bounds(d_ff, step):
    return [(c, min(c + step, d_ff)) for c in range(0, d_ff, step)]


def _ple_out(h, p, w, final_norm):
    gate = _sigmoid(_dot(_rms(h, w["g_ple"][...]).astype(BF16), w["w_ple_gate"][...]))
    h = h + _dot(p.astype(BF16), w["w_ple_proj"][...]) * gate
    if final_norm:
        h = _rms(h, w["g_final"][...])
    return h


def _dense_tail(h, attn, lru_out, p, w, final_norm):
    h = _mix_out(h, attn, lru_out, w)
    hn = _rms(h, w["g_ffn"][...]).astype(BF16)
    d_ff = w["w_ffn_gate"].shape[1]
    for c0, c1 in _ffn_bounds(d_ff, -(-d_ff // (2 * LANES)) * LANES):
        h = _ffn_piece(hn, h, w, c0, c1)
    return _ple_out(h, p, w, final_norm)


def _softmax_pv(scores, values, transposed=None):
    transposed = transposed or [False] * len(values)
    m = functools.reduce(jnp.maximum, [jnp.max(s, axis=-1, keepdims=True) for s in scores])
    num = None
    den = None
    for s, v, v_t in zip(scores, values, transposed):
        e = jnp.exp2(s - m)
        l = jnp.sum(e, axis=-1, keepdims=True)
        o = _dot_nt(e.astype(BF16), v) if v_t else _dot(e.astype(BF16), v)
        num = o if num is None else num + o
        den = l if den is None else den + l
    return num / den


WEIGHT_NAMES = (
    "g_mix", "w_in", "conv_w", "conv_b", "w_r", "b_r", "w_i", "b_i", "lam",
    "g_attn_out", "g_lru_out", "w_out", "g_ffn", "w_ffn_gate", "w_ffn_up", "w_ffn_down",
    "g_ple", "w_ple_gate", "w_ple_proj", "g_final")


def _order_after(x, *deps):
    bits = None
    for dep in deps:
        b = pltpu.bitcast(dep[:SUBLANES, :LANES], jnp.uint32)
        bits = b if bits is None else bits | b
    zero = pltpu.bitcast((bits >> 16) >> 16, F32)
    head = x[:SUBLANES] + jnp.concatenate([zero] * (x.shape[1] // LANES), axis=1)
    return jnp.concatenate([head, x[SUBLANES:]], axis=0)


def _project(x, w):
    xn = _rms(x, w["g_mix"][...]).astype(BF16)
    proj = _dot(xn, w["w_in"][...])
    aw = w["g_attn_out"].shape[1]
    lw = w["g_lru_out"].shape[1]
    q = proj[:, :aw] * (SCALE * LOG2E)
    k = proj[:, aw:2 * aw]
    v = proj[:, 2 * aw:3 * aw]
    xr = proj[:, 3 * aw:3 * aw + lw]
    gr = proj[:, 3 * aw + lw:]
    return q, k, v, xr, gr


def _conv(xr_buf, base, rows, w):
    cw = w["conv_w"]
    ext = xr_buf[base - SUBLANES:base + rows, :]
    xc = w["conv_b"][...] + ext[SUBLANES:] * cw[CONV_WIDTH - 1:CONV_WIDTH, :]
    for back in range(1, CONV_WIDTH):
        j = CONV_WIDTH - 1 - back
        xc = xc + pltpu.roll(ext, back, axis=0)[SUBLANES:] * cw[j:j + 1, :]
    return xc


def _bias_kernel(ext_ref, bias_ref):
    n_heads = ext_ref.shape[0]
    for h in range(n_heads):
        rows = jnp.broadcast_to(ext_ref[h], (CHUNK, EXT))
        rolled = pltpu.roll(rows, EXT - (CHUNK - 1), axis=1, stride=1, stride_axis=0)
        bias_ref[h] = rolled[:, :BAND] * LOG2E


def _bias_table(rel_table):
    n_heads, rel_size = rel_table.shape
    assert rel_size == REL_CLIP + CHUNK
    left = (BAND - 1) - REL_CLIP
    ext = jnp.pad(rel_table.astype(F32), ((0, 0), (left, EXT - left - rel_size)), mode="edge")
    return pl.pallas_call(
        _bias_kernel,
        out_shape=jax.ShapeDtypeStruct((n_heads, CHUNK, BAND), F32),
        name="rel_bias_expand",
    )(ext.reshape(n_heads, 1, EXT))


def _prompt_kernel(*refs, tile, seq, final_norm):
    x_ref, xprev_ref, pprev_ref, bias_ref = refs[:4]
    nw = len(WEIGHT_NAMES)
    w = dict(zip(WEIGHT_NAMES, refs[4:4 + nw]))
    y_ref, ko_ref, vo_ref, convo_ref, ho_ref = refs[4 + nw:9 + nw]
    kbuf, vbuf, q_scr, attn_scr, gr_scr, xr_buf, h_scr = refs[9 + nw:]

    t = pl.program_id(1)
    n_tiles = seq // tile
    n_pairs = kbuf.shape[1] // LANES

    @pl.when(t == 0)
    def _():
        kbuf[0:PAD, :] = jnp.zeros((PAD, kbuf.shape[1]), BF16)
        vbuf[0:PAD, :] = jnp.zeros((PAD, vbuf.shape[1]), BF16)
        xr_buf[...] = jnp.zeros(xr_buf.shape, F32)
        gr_scr[...] = jnp.zeros(gr_scr.shape, F32)
        h_scr[...] = jnp.zeros(h_scr.shape, F32)
        attn_scr[...] = jnp.zeros(attn_scr.shape, F32)

    st = {}

    xc = _conv(xr_buf, SUBLANES, tile, w)
    xcb = xc.astype(BF16)
    r_pre = _gate_dot(xcb, w["w_r"])
    i_pre = _gate_dot(xcb, w["w_i"])
    h0 = h_scr[...]
    gr_prev = gr_scr[...]
    xr_tail = xr_buf[tile:tile + SUBLANES, :]

    q, k, v, xr, gr = _project(_order_after(x_ref[0], r_pre, i_pre), w)

    a, u = _lru_inputs(xc, r_pre, i_pre, w)
    u = _scan_blocked(a, u, h0)
    lru_out = u * _gelu_tanh(gr_prev)

    q_scr[...] = q.astype(BF16)
    kbuf[PAD:PAD + tile, :] = k.astype(BF16)
    vbuf[PAD:PAD + tile, :] = v.astype(BF16)
    h_scr[...] = jnp.where(t > 0, u[tile - 1:tile, :], 0.0)

    xr_buf[0:SUBLANES, :] = xr_tail
    xr_buf[SUBLANES:SUBLANES + tile, :] = xr
    gr_scr[...] = gr

    keep_from = seq - PAD

    @pl.when(jnp.logical_and(t < n_tiles, t * tile >= keep_from))
    def _():
        off = pl.multiple_of(t * tile - keep_from, tile)
        ko_ref[0, pl.ds(off, tile), :] = k
        vo_ref[0, pl.ds(off, tile), :] = v

    @pl.when(t == n_tiles)
    def _():
        convo_ref[0] = xr_buf[SUBLANES - (CONV_WIDTH - 1):SUBLANES, :]
        ho_ref[0] = h_scr[...]

    st["h"] = _mix_out(xprev_ref[0], attn_scr[...], lru_out, w)

    rows_w = tile // 16
    dense = []

    def ffn_norm():
        st["hn"] = _rms(st["h"], w["g_ffn"][...]).astype(BF16)

    def ffn_gate(c0, c1):
        st["g"] = _dot(st["hn"], w["w_ffn_gate"][:, c0:c1])

    def ffn_up(c0, c1):
        g = st.pop("g")
        st["act"] = ((g * _sigmoid(g)) * _dot(st["hn"], w["w_ffn_up"][:, c0:c1])).astype(BF16)

    def ffn_down(c0, c1):
        st["h"] = st["h"] + _dot(st.pop("act"), w["w_ffn_down"][c0:c1, :])

    d_model = x_ref.shape[2]
    d_ff = w["w_ffn_gate"].shape[1]
    dense += [(0, ffn_norm)]
    for c0, c1 in _ffn_bounds(d_ff, FFN_CHUNK):
        wt = rows_w * (d_model // MXU_DIM) * -(-(c1 - c0) // MXU_DIM)
        dense += [(wt, functools.partial(ffn_gate, c0, c1)), (wt, functools.partial(ffn_up, c0, c1)),
                  (wt, functools.partial(ffn_down, c0, c1))]

    def ple_norm():
        st["hpn"] = _rms(st["h"], w["g_ple"][...]).astype(BF16)
        st["pg"] = []

    def ple_gate(c0, c1):
        st["pg"].append(_sigmoid(_dot(st["hpn"], w["w_ple_gate"][:, c0:c1])))

    def ple_out():
        h = st["h"] + _dot(pprev_ref[0].astype(BF16), w["w_ple_proj"][...]) * jnp.concatenate(
            st.pop("pg"), axis=-1)
        if final_norm:
            h = _rms(h, w["g_final"][...])
        y_ref[0] = h

    dense += [(0, ple_norm)]
    for c0, c1 in _ffn_bounds(d_model, MXU_DIM):
        dense += [(rows_w * (d_model // MXU_DIM), functools.partial(ple_gate, c0, c1))]
    dense += [(rows_w * (d_model // MXU_DIM), ple_out)]

    lane = lax.broadcasted_iota(jnp.int32, (CHUNK, LANES), 1)
    low_half = lane < HEAD_DIM
    kidx = lax.broadcasted_iota(jnp.int32, (1, BAND), 1)
    units = [(j, pr) for j in range(tile // CHUNK) for pr in range(n_pairs)]

    def scores(j, pr):
        lanes = slice(pr * LANES, (pr + 1) * LANES)
        q2 = q_scr[j * CHUNK:(j + 1) * CHUNK, lanes]
        zero = jnp.zeros_like(q2)
        qs = jnp.concatenate([jnp.where(low_half, q2, zero), jnp.where(low_half, zero, q2)],
                             axis=0)
        s = _dot_nt(qs, kbuf[j * CHUNK:j * CHUNK + BAND, lanes]) + bias_ref[pr]
        valid = (t * tile + j * CHUNK + kidx - PAD) >= PAD
        return jnp.where(valid, s, NEG)

    def attend(j, pr, s):
        lanes = slice(pr * LANES, (pr + 1) * LANES)
        o = _softmax_pv([s], [vbuf[j * CHUNK:j * CHUNK + BAND, lanes]])
        st["attn", j, pr] = jnp.where(low_half, o[:CHUNK], o[CHUNK:])

    total_w = sum(wt for wt, _ in dense)
    done_w = 0
    pending = list(dense)
    s_cur = scores(*units[0])
    for i, unit in enumerate(units):
        s_next = scores(*units[i + 1]) if i + 1 < len(units) else None
        share = total_w * (i + 1) // max(len(units) - DENSE_EARLY, 1)
        while pending and (done_w < share or i + 1 == len(units)):
            wt, piece = pending.pop(0)
            piece()
            done_w += wt
        attend(*unit, s_cur)
        s_cur = s_next

    for j, pr in units:
        attn_scr[j * CHUNK:(j + 1) * CHUNK, pr * LANES:(pr + 1) * LANES] = st.pop(("attn", j, pr))

    for r in range(0, PAD, tile):
        n = min(tile, PAD - r)
        kbuf[r:r + n, :] = kbuf[r + tile:r + tile + n, :]
        vbuf[r:r + n, :] = vbuf[r + tile:r + tile + n, :]


def _const_spec(arr):
    nd = arr.ndim
    return pl.BlockSpec(arr.shape, lambda *_: (0,) * nd, pipeline_mode=pl.Buffered(1))


def _prompt_layer(x, p, bias, weights, final_norm, tile):
    b, seq, d = x.shape
    aw = weights["g_attn_out"].shape[1]
    lw = weights["g_lru_out"].shape[1]
    assert seq % tile == 0 and PAD % tile == 0 and seq >= PAD and tile % CHUNK == 0
    assert tile & (tile - 1) == 0 and aw % LANES == 0
    n_tiles = seq // tile
    wlist = [weights[n] for n in WEIGHT_NAMES]
    kern = functools.partial(_prompt_kernel, tile=tile, seq=seq, final_norm=final_norm)
    out_shape = (
        jax.ShapeDtypeStruct((b, seq, d), F32),
        jax.ShapeDtypeStruct((b, PAD, aw), F32),
        jax.ShapeDtypeStruct((b, PAD, aw), F32),
        jax.ShapeDtypeStruct((b, CONV_WIDTH - 1, lw), F32),
        jax.ShapeDtypeStruct((b, 1, lw), F32),
    )
    cur = lambda i, j: (i, jnp.minimum(j, n_tiles - 1), 0)
    prev = lambda i, j: (i, jnp.maximum(j - 1, 0), 0)
    in_specs = [
        pl.BlockSpec((1, tile, d), cur),
        pl.BlockSpec((1, tile, d), prev),
        pl.BlockSpec((1, tile, p.shape[-1]), prev),
        _const_spec(bias),
    ] + [_const_spec(a) for a in wlist]
    out_specs = (
        pl.BlockSpec((1, tile, d), prev),
        pl.BlockSpec((1, PAD, aw), lambda i, j: (i, 0, 0)),
        pl.BlockSpec((1, PAD, aw), lambda i, j: (i, 0, 0)),
        pl.BlockSpec((1, CONV_WIDTH - 1, lw), lambda i, j: (i, 0, 0)),
        pl.BlockSpec((1, 1, lw), lambda i, j: (i, 0, 0)),
    )
    scratch = [
        pltpu.VMEM((PAD + tile, aw), BF16),
        pltpu.VMEM((PAD + tile, aw), BF16),
        pltpu.VMEM((tile, aw), BF16),
        pltpu.VMEM((tile, aw), F32),
        pltpu.VMEM((tile, lw), F32),
        pltpu.VMEM((SUBLANES + tile, lw), F32),
        pltpu.VMEM((1, lw), F32),
    ]
    return pl.pallas_call(
        kern,
        out_shape=out_shape,
        grid=(b, n_tiles + 1),
        in_specs=in_specs,
        out_specs=out_specs,
        scratch_shapes=scratch,
        compiler_params=pltpu.CompilerParams(
            dimension_semantics=("arbitrary", "arbitrary"), vmem_limit_bytes=VMEM_LIMIT),
        name="prompt_layer",
    )(x, x, p, bias, *wlist)


FRONT_NAMES = ("g_mix", "w_in", "conv_w", "conv_b", "w_r", "b_r", "w_i", "b_i", "lam",
               "g_attn_out", "g_lru_out")
BACK_NAMES = ("g_attn_out", "g_lru_out", "w_out", "g_ffn", "w_ffn_gate", "w_ffn_up",
              "w_ffn_down", "g_ple", "w_ple_gate", "w_ple_proj", "g_final")


def _sample_front_kernel(*refs, n_heads):
    x_ref, ck_ref, cv_ref, sconv_ref, sh_ref, bias_ref = refs[:6]
    nw = len(FRONT_NAMES)
    w = dict(zip(FRONT_NAMES, refs[6:6 + nw]))
    attn_ref, lru_ref, ko_ref, vo_ref, convo_ref, ho_ref = refs[6 + nw:12 + nw]
    (xr_buf,) = refs[12 + nw:]

    nb, t, d = x_ref.shape
    n_cache = ck_ref.shape[3]
    aw = n_heads * HEAD_DIM
    rows = nb * t

    x = x_ref[...].reshape(rows, d)
    q, k, v, xr, gr = _project(x, w)
    ko_ref[...] = k.reshape(nb, t, aw)
    vo_ref[...] = v.reshape(nb, t, aw)
    qb = q.astype(BF16)
    kb_new = k.astype(BF16)
    vb_new = v.astype(BF16)

    def scores(b, h):
        rs = slice(b * t, (b + 1) * t)
        cols = slice(h * HEAD_DIM, (h + 1) * HEAD_DIM)
        qh = qb[rs, cols]
        kc_t = ck_ref[b, h].astype(BF16)
        s_c = _dot(qh, kc_t) + bias_ref[h, 0:t, PAD - n_cache:PAD]
        s_n = _dot_nt(qh, kb_new[rs, cols]) + bias_ref[h, 0:t, PAD:PAD + t]
        return s_c, s_n

    s_cur = [scores(0, h) for h in range(n_heads)]
    for b in range(nb):
        s_next = [scores(b + 1, h) for h in range(n_heads)] if b + 1 < nb else None
        outs = []
        for h in range(n_heads):
            vc_t = cv_ref[b, h].astype(BF16)
            vn = vb_new[b * t:(b + 1) * t, h * HEAD_DIM:(h + 1) * HEAD_DIM]
            outs.append(_softmax_pv(list(s_cur[h]), [vc_t, vn], [True, False]))
        attn_ref[b] = jnp.concatenate(outs, axis=-1)
        s_cur = s_next

    seg_rows = SUBLANES + t
    xcs = []
    for b in range(nb):
        base = b * seg_rows
        xr_buf[base + SUBLANES - (CONV_WIDTH - 1):base + SUBLANES, :] = sconv_ref[b]
        xr_buf[base + SUBLANES:base + seg_rows, :] = xr[b * t:(b + 1) * t, :]
        xcs.append(_conv(xr_buf, base + SUBLANES, t, w))
        convo_ref[b] = xr_buf[base + seg_rows - (CONV_WIDTH - 1):base + seg_rows, :]
    xc = jnp.concatenate(xcs, axis=0)
    h0_rows = jnp.concatenate(
        [jnp.broadcast_to(sh_ref[b], (t, sh_ref.shape[2])) for b in range(nb)], axis=0)
    hs, lru_out = _rglru(xc, gr, h0_rows, t, w)
    for b in range(nb):
        ho_ref[b] = hs[(b + 1) * t - 1:(b + 1) * t, :]
    lru_ref[...] = lru_out.reshape(nb, t, lru_out.shape[1])


def _sample_back_kernel(*refs, final_norm):
    x_ref, attn_ref, lru_ref, p_ref = refs[:4]
    nw = len(BACK_NAMES)
    w = dict(zip(BACK_NAMES, refs[4:4 + nw]))
    y_ref = refs[4 + nw]
    y_ref[...] = _dense_tail(x_ref[...], attn_ref[...], lru_ref[...], p_ref[...], w, final_norm)


def _sample_layer(x, p, cache_k, cache_v, state_conv, state_h, bias, weights, final_norm, group,
                  back_rows):
    nbatch, t, d = x.shape
    n_heads = bias.shape[0]
    aw = n_heads * HEAD_DIM
    n_cache = cache_k.shape[3]
    lw = weights["g_lru_out"].shape[1]
    assert nbatch % group == 0 and t % SUBLANES == 0 and t & (t - 1) == 0
    assert CONV_WIDTH - 1 <= t <= CHUNK and n_cache <= PAD and cache_k.shape[1:3] == (n_heads, HEAD_DIM)
    assert (nbatch * t) % back_rows == 0
    state_h3 = state_h.reshape(nbatch, 1, lw)

    def grp(shape):
        nd = len(shape)
        return pl.BlockSpec((group,) + tuple(shape[1:]), lambda i: (i,) + (0,) * (nd - 1))

    front_w = [weights[n] for n in FRONT_NAMES]
    front_out = (
        jax.ShapeDtypeStruct((nbatch, t, aw), F32),
        jax.ShapeDtypeStruct((nbatch, t, lw), F32),
        jax.ShapeDtypeStruct((nbatch, t, aw), F32),
        jax.ShapeDtypeStruct((nbatch, t, aw), F32),
        jax.ShapeDtypeStruct((nbatch, CONV_WIDTH - 1, lw), F32),
        jax.ShapeDtypeStruct((nbatch, 1, lw), F32),
    )
    ins = [x, cache_k, cache_v, state_conv, state_h3]
    attn, lru_out, k_new, v_new, conv_new, h_new = pl.pallas_call(
        functools.partial(_sample_front_kernel, n_heads=n_heads),
        out_shape=front_out,
        grid=(nbatch // group,),
        in_specs=[grp(a.shape) for a in ins] + [_const_spec(bias)]
        + [_const_spec(a) for a in front_w],
        out_specs=tuple(grp(o.shape) for o in front_out),
        scratch_shapes=[pltpu.VMEM((group * (SUBLANES + t), lw), F32)],
        compiler_params=pltpu.CompilerParams(
            dimension_semantics=("arbitrary",), vmem_limit_bytes=VMEM_LIMIT),
        name="sample_front",
    )(*ins, bias, *front_w)

    rows = nbatch * t
    back_w = [weights[n] for n in BACK_NAMES]
    flat = [x.reshape(rows, d), attn.reshape(rows, aw), lru_out.reshape(rows, lw),
            p.reshape(rows, p.shape[-1])]
    y = pl.pallas_call(
        functools.partial(_sample_back_kernel, final_norm=final_norm),
        out_shape=jax.ShapeDtypeStruct((rows, d), F32),
        grid=(rows // back_rows,),
        in_specs=[pl.BlockSpec((back_rows, a.shape[1]), lambda i: (i, 0)) for a in flat]
        + [_const_spec(a) for a in back_w],
        out_specs=pl.BlockSpec((back_rows, d), lambda i: (i, 0)),
        compiler_params=pltpu.CompilerParams(
            dimension_semantics=("arbitrary",), vmem_limit_bytes=VMEM_LIMIT),
        name="sample_back",
    )(*flat, *back_w)
    return y.reshape(nbatch, t, d), k_new, v_new, conv_new, h_new


def _block_diag(wb):
    n, kk, jj = wb.shape
    eye = jnp.eye(n, dtype=wb.dtype)
    return (wb[:, :, None, :] * eye[:, None, :, None]).reshape(n * kk, n * jj)


PROMPT_TILE = 256
SAMPLE_GROUP = 8
SAMPLE_BACK_ROWS = 256


def kernel(x_prompt, x_sample, p_prompt, p_sample, cache_k, cache_v, state_conv, state_h, g_mix, w_in, conv_w, conv_b, w_rgate, b_rgate, w_igate, b_igate, lru_lambda, rel_bias_table, g_attn_out, g_lru_out, w_out, g_ffn, w_ffn_gate, w_ffn_up, w_ffn_down, g_ple, w_ple_gate, w_ple_proj, g_final):
    depth = w_in.shape[0]
    hp, hs = x_prompt, x_sample
    outs = [[] for _ in range(8)]
    for i in range(depth):
        row = lambda a: a[i].reshape(1, -1).astype(F32)
        weights = {
            "g_mix": row(g_mix), "w_in": w_in[i].astype(BF16),
            "conv_w": conv_w[i].astype(F32), "conv_b": row(conv_b),
            "w_r": _block_diag(w_rgate[i]).astype(BF16), "b_r": row(b_rgate),
            "w_i": _block_diag(w_igate[i]).astype(BF16), "b_i": row(b_igate),
            "lam": row(lru_lambda),
            "g_attn_out": row(g_attn_out), "g_lru_out": row(g_lru_out),
            "w_out": w_out[i].astype(BF16), "g_ffn": row(g_ffn),
            "w_ffn_gate": w_ffn_gate[i].astype(BF16), "w_ffn_up": w_ffn_up[i].astype(BF16),
            "w_ffn_down": w_ffn_down[i].astype(BF16), "g_ple": row(g_ple),
            "w_ple_gate": w_ple_gate[i].astype(BF16), "w_ple_proj": w_ple_proj[i].astype(BF16),
            "g_final": g_final.reshape(1, -1).astype(F32),
        }
        final = i == depth - 1
        bias = _bias_table(rel_bias_table[i])
        n_heads = rel_bias_table.shape[1]
        aw = n_heads * HEAD_DIM

        hp, k1, v1, c1, r1 = _prompt_layer(
            hp, p_prompt[i], bias.reshape(n_heads // 2, 2 * CHUNK, BAND), weights, final,
            PROMPT_TILE)
        nb, nc = cache_k.shape[1:3]
        hs, k2, v2, c2, r2 = _sample_layer(
            hs, p_sample[i], jnp.transpose(cache_k[i], (0, 2, 3, 1)),
            jnp.transpose(cache_v[i], (0, 2, 3, 1)), state_conv[i], state_h[i], bias,
            weights, final, SAMPLE_GROUP, SAMPLE_BACK_ROWS)
        bp, keep = k1.shape[:2]
        ts = k2.shape[1]
        for lst, val in zip(outs, (
                k1.reshape(bp, keep, n_heads, HEAD_DIM), v1.reshape(bp, keep, n_heads, HEAD_DIM),
                c1, r1.reshape(bp, -1),
                k2.reshape(nb, ts, n_heads, HEAD_DIM), v2.reshape(nb, ts, n_heads, HEAD_DIM),
                c2, r2.reshape(nb, -1))):
            lst.append(val)
    return (hp, hs) + tuple(jnp.stack(l) for l in outs)
```

```python
import functools

import jax
import jax.numpy as jnp
from jax import lax
from jax.experimental import pallas as pl
from jax.experimental.pallas import tpu as pltpu

CHUNK = 64
LEFT_CHUNKS = 8
PAD = LEFT_CHUNKS * CHUNK
BAND = PAD + CHUNK
HEAD_DIM = 64
CONV_WIDTH = 4
LRU_C = 8.0
REL_CLIP = 128
EPS = 1e-6
NEG = -1e30
SCALE = HEAD_DIM ** -0.5
LOG2E = 1.4426950408889634
LANES = 128
SUBLANES = 8
MXU_DIM = 256
EXT = 640
VMEM_LIMIT = 60 * 1024 * 1024
FFN_CHUNK = 4 * MXU_DIM
DENSE_EARLY = 0

F32 = jnp.float32
BF16 = jnp.bfloat16


def _dot(a, b):
    return jnp.dot(a, b, preferred_element_type=F32)


def _dot_nt(a, b):
    return lax.dot_general(a, b, (((1,), (1,)), ((), ())), preferred_element_type=F32)


def _rms(x, g):
    ms = jnp.mean(x * x, axis=-1, keepdims=True)
    return (x * lax.rsqrt(ms + EPS)) * g


def _sigmoid(x):
    return 1.0 / (1.0 + jnp.exp2(x * (-LOG2E)))


def _gelu_tanh(x):
    c = -0.7978845608028654 * 2.0 * LOG2E
    return x / (1.0 + jnp.exp2(x * (c + (c * 0.044715) * (x * x))))


def _shift_rows(x, d, fill, seg):
    rows = x.shape[0]
    if seg == rows and d % SUBLANES == 0:
        return jnp.concatenate([jnp.full((d, x.shape[1]), fill, x.dtype), x[:rows - d]], axis=0)
    rolled = pltpu.roll(x, d, axis=0)
    row = lax.broadcasted_iota(jnp.int32, x.shape, 0)
    return jnp.where((row & (seg - 1)) >= d, rolled, fill)


def _scan_distances(seg):
    return [1 << b for b in range(seg.bit_length() - 1)]


def _scan_step(a, u, d, seg):
    u = u + a * _shift_rows(u, d, 0.0, seg)
    if 2 * d < seg:
        a = a * _shift_rows(a, d, 1.0, seg)
    return a, u


def _sqrt_nonneg(y):
    return jnp.where(y > 0.0, y * lax.rsqrt(y), 0.0)


def _lru_inputs(xc, r_pre, i_pre, w):
    r = _sigmoid(r_pre + w["b_r"][...])
    ig = _sigmoid(i_pre + w["b_i"][...])
    z = -w["lam"][...]
    softplus = jnp.maximum(z, 0.0) + jnp.log1p(jnp.exp(-jnp.abs(z)))
    a = jnp.exp2(r * ((-LRU_C * LOG2E) * softplus))
    u = _sqrt_nonneg(1.0 - a * a) * (ig * xc)
    return a, u


def _gate_dot(xcb, w_ref):
    n = w_ref.shape[0]
    if n % MXU_DIM:
        return _dot(xcb, w_ref[...])
    return jnp.concatenate(
        [_dot(xcb[:, c:c + MXU_DIM], w_ref[c:c + MXU_DIM, c:c + MXU_DIM])
         for c in range(0, n, MXU_DIM)], axis=1)


def _rglru(xc, gr, h0_rows, seg, w):
    xcb = xc.astype(BF16)
    a, u = _lru_inputs(xc, _gate_dot(xcb, w["w_r"]), _gate_dot(xcb, w["w_i"]), w)
    row = lax.broadcasted_iota(jnp.int32, xc.shape, 0)
    u = u + jnp.where((row & (seg - 1)) == 0, a * h0_rows, 0.0)
    for d in _scan_distances(seg):
        a, u = _scan_step(a, u, d, seg)
    return u, u * _gelu_tanh(gr)


def _scan_blocked(a, u, h0):
    shape = a.shape
    grouped = (shape[0] // SUBLANES, SUBLANES, shape[1])
    a = a.reshape(grouped)
    u = u.reshape(grouped)
    row = lax.broadcasted_iota(jnp.int32, grouped, 1)
    for d in _scan_distances(SUBLANES):
        inside = row >= d
        u = u + jnp.where(inside, a, 0.0) * pltpu.roll(u, d, axis=1)
        a = a * jnp.where(inside, pltpu.roll(a, d, axis=1), 1.0)
    a = a.reshape(shape)
    u = u.reshape(shape)
    carry = jnp.broadcast_to(h0, (SUBLANES, a.shape[1]))
    groups = []
    for g in range(a.shape[0] // SUBLANES):
        rows = slice(g * SUBLANES, (g + 1) * SUBLANES)
        h = u[rows] + a[rows] * carry
        groups.append(h)
        carry = jnp.broadcast_to(h[SUBLANES - 1:SUBLANES], h.shape)
    return jnp.concatenate(groups, axis=0)


def _mix_out(h, attn, lru_out, w):
    mixed = jnp.concatenate(
        [_rms(attn, w["g_attn_out"][...]), _rms(lru_out, w["g_lru_out"][...])], axis=-1)
    return h + _dot(mixed.astype(BF16), w["w_out"][...])


def _ffn_piece(hn, acc, w, c0, c1):
    gate = _dot(hn, w["w_ffn_gate"][:, c0:c1])
    up = _dot(hn, w["w_ffn_up"][:, c0:c1])
    act = (gate * _sigmoid(gate)) * up
    return acc + _dot(act.astype(BF16), w["w_ffn_down"][c0:c1, :])


def _ffn_bounds(d_ff, step):
    return [(c, min(c + step, d_ff)) for c in range(0, d_ff, step)]


def _ple_out(h, p, w, final_norm):
    gate = _sigmoid(_dot(_rms(h, w["g_ple"][...]).astype(BF16), w["w_ple_gate"][...]))
    h = h + _dot(p.astype(BF16), w["w_ple_proj"][...]) * gate
    if final_norm:
        h = _rms(h, w["g_final"][...])
    return h


def _dense_tail(h, attn, lru_out, p, w, final_norm):
    h = _mix_out(h, attn, lru_out, w)
    hn = _rms(h, w["g_ffn"][...]).astype(BF16)
    d_ff = w["w_ffn_gate"].shape[1]
    for c0, c1 in _ffn_bounds(d_ff, -(-d_ff // (2 * LANES)) * LANES):
        h = _ffn_piece(hn, h, w, c0, c1)
    return _ple_out(h, p, w, final_norm)


def _softmax_pv(scores, values, transposed=None):
    transposed = transposed or [False] * len(values)
    m = functools.reduce(jnp.maximum, [jnp.max(s, axis=-1, keepdims=True) for s in scores])
    num = None
    den = None
    for s, v, v_t in zip(scores, values, transposed):
        e = jnp.exp2(s - m)
        l = jnp.sum(e, axis=-1, keepdims=True)
        o = _dot_nt(e.astype(BF16), v) if v_t else _dot(e.astype(BF16), v)
        num = o if num is None else num + o
        den = l if den is None else den + l
    return num / den


WEIGHT_NAMES = (
    "g_mix", "w_in", "conv_w", "conv_b", "w_r", "b_r", "w_i", "b_i", "lam",
    "g_attn_out", "g_lru_out", "w_out", "g_ffn", "w_ffn_gate", "w_ffn_up", "w_ffn_down",
    "g_ple", "w_ple_gate", "w_ple_proj", "g_final")


def _order_after(x, *deps):
    bits = None
    for dep in deps:
        b = pltpu.bitcast(dep[:SUBLANES, :LANES], jnp.uint32)
        bits = b if bits is None else bits | b
    zero = pltpu.bitcast((bits >> 16) >> 16, F32)
    head = x[:SUBLANES] + jnp.concatenate([zero] * (x.shape[1] // LANES), axis=1)
    return jnp.concatenate([head, x[SUBLANES:]], axis=0)


def _project(x, w):
    xn = _rms(x, w["g_mix"][...]).astype(BF16)
    proj = _dot(xn, w["w_in"][...])
    aw = w["g_attn_out"].shape[1]
    lw = w["g_lru_out"].shape[1]
    q = proj[:, :aw] * (SCALE * LOG2E)
    k = proj[:, aw:2 * aw]
    v = proj[:, 2 * aw:3 * aw]
    xr = proj[:, 3 * aw:3 * aw + lw]
    gr = proj[:, 3 * aw + lw:]
    return q, k, v, xr, gr


def _conv(xr_buf, base, rows, w):
    cw = w["conv_w"]
    ext = xr_buf[base - SUBLANES:base + rows, :]
    xc = w["conv_b"][...] + ext[SUBLANES:] * cw[CONV_WIDTH - 1:CONV_WIDTH, :]
    for back in range(1, CONV_WIDTH):
        j = CONV_WIDTH - 1 - back
        xc = xc + pltpu.roll(ext, back, axis=0)[SUBLANES:] * cw[j:j + 1, :]
    return xc


def _bias_kernel(ext_ref, bias_ref):
    n_heads = ext_ref.shape[0]
    for h in range(n_heads):
        rows = jnp.broadcast_to(ext_ref[h], (CHUNK, EXT))
        rolled = pltpu.roll(rows, EXT - (CHUNK - 1), axis=1, stride=1, stride_axis=0)
        bias_ref[h] = rolled[:, :BAND] * LOG2E


def _bias_table(rel_table):
    n_heads, rel_size = rel_table.shape
    assert rel_size == REL_CLIP + CHUNK
    left = (BAND - 1) - REL_CLIP
    ext = jnp.pad(rel_table.astype(F32), ((0, 0), (left, EXT - left - rel_size)), mode="edge")
    return pl.pallas_call(
        _bias_kernel,
        out_shape=jax.ShapeDtypeStruct((n_heads, CHUNK, BAND), F32),
        name="rel_bias_expand",
    )(ext.reshape(n_heads, 1, EXT))


def _prompt_kernel(*refs, tile, seq, final_norm):
    x_ref, xprev_ref, pprev_ref, bias_ref = refs[:4]
    nw = len(WEIGHT_NAMES)
    w = dict(zip(WEIGHT_NAMES, refs[4:4 + nw]))
    y_ref, ko_ref, vo_ref, convo_ref, ho_ref = refs[4 + nw:9 + nw]
    kbuf, vbuf, q_scr, attn_scr, gr_scr, xr_buf, h_scr = refs[9 + nw:]

    t = pl.program_id(1)
    n_tiles = seq // tile
    n_pairs = kbuf.shape[1] // LANES

    @pl.when(t == 0)
    def _():
        kbuf[0:PAD, :] = jnp.zeros((PAD, kbuf.shape[1]), BF16)
        vbuf[0:PAD, :] = jnp.zeros((PAD, vbuf.shape[1]), BF16)
        xr_buf[...] = jnp.zeros(xr_buf.shape, F32)
        gr_scr[...] = jnp.zeros(gr_scr.shape, F32)
        h_scr[...] = jnp.zeros(h_scr.shape, F32)
        attn_scr[...] = jnp.zeros(attn_scr.shape, F32)

    st = {}

    xc = _conv(xr_buf, SUBLANES, tile, w)
    xcb = xc.astype(BF16)
    r_pre = _gate_dot(xcb, w["w_r"])
    i_pre = _gate_dot(xcb, w["w_i"])
    h0 = h_scr[...]
    gr_prev = gr_scr[...]
    xr_tail = xr_buf[tile:tile + SUBLANES, :]

    q, k, v, xr, gr = _project(_order_after(x_ref[0], r_pre, i_pre), w)

    a, u = _lru_inputs(xc, r_pre, i_pre, w)
    u = _scan_blocked(a, u, h0)
    lru_out = u * _gelu_tanh(gr_prev)

    q_scr[...] = q.astype(BF16)
    kbuf[PAD:PAD + tile, :] = k.astype(BF16)
    vbuf[PAD:PAD + tile, :] = v.astype(BF16)
    h_scr[...] = jnp.where(t > 0, u[tile - 1:tile, :], 0.0)

    xr_buf[0:SUBLANES, :] = xr_tail
    xr_buf[SUBLANES:SUBLANES + tile, :] = xr
    gr_scr[...] = gr

    keep_from = seq - PAD

    @pl.when(jnp.logical_and(t < n_tiles, t * tile >= keep_from))
    def _():
        off = pl.multiple_of(t * tile - keep_from, tile)
        ko_ref[0, pl.ds(off, tile), :] = k
        vo_ref[0, pl.ds(off, tile), :] = v

    @pl.when(t == n_tiles)
    def _():
        convo_ref[0] = xr_buf[SUBLANES - (CONV_WIDTH - 1):SUBLANES, :]
        ho_ref[0] = h_scr[...]

    st["h"] = _mix_out(xprev_ref[0], attn_scr[...], lru_out, w)

    rows_w = tile // 16
    dense = []

    def ffn_norm():
        st["hn"] = _rms(st["h"], w["g_ffn"][...]).astype(BF16)

    def ffn_gate(c0, c1):
        st["g"] = _dot(st["hn"], w["w_ffn_gate"][:, c0:c1])

    def ffn_up(c0, c1):
        g = st.pop("g")
        st["act"] = ((g * _sigmoid(g)) * _dot(st["hn"], w["w_ffn_up"][:, c0:c1])).astype(BF16)

    def ffn_down(c0, c1):
        st["h"] = st["h"] + _dot(st.pop("act"), w["w_ffn_down"][c0:c1, :])

    d_model = x_ref.shape[2]
    d_ff = w["w_ffn_gate"].shape[1]
    dense += [(0, ffn_norm)]
    for c0, c1 in _ffn_bounds(d_ff, FFN_CHUNK):
        wt = rows_w * (d_model // MXU_DIM) * -(-(c1 - c0) // MXU_DIM)
        dense += [(wt, functools.partial(ffn_gate, c0, c1)), (wt, functools.partial(ffn_up, c0, c1)),
                  (wt, functools.partial(ffn_down, c0, c1))]

    def ple_norm():
        st["hpn"] = _rms(st["h"], w["g_ple"][...]).astype(BF16)
        st["pg"] = []

    def ple_gate(c0, c1):
        st["pg"].append(_sigmoid(_dot(st["hpn"], w["w_ple_gate"][:, c0:c1])))

    def ple_out():
        h = st["h"] + _dot(pprev_ref[0].astype(BF16), w["w_ple_proj"][...]) * jnp.concatenate(
            st.pop("pg"), axis=-1)
        if final_norm:
            h = _rms(h, w["g_final"][...])
        y_ref[0] = h

    dense += [(0, ple_norm)]
    for c0, c1 in _ffn_bounds(d_model, MXU_DIM):
        dense += [(rows_w * (d_model // MXU_DIM), functools.partial(ple_gate, c0, c1))]
    dense += [(rows_w * (d_model // MXU_DIM), ple_out)]

    lane = lax.broadcasted_iota(jnp.int32, (CHUNK, LANES), 1)
    low_half = lane < HEAD_DIM
    kidx = lax.broadcasted_iota(jnp.int32, (1, BAND), 1)
    units = [(j, pr) for j in range(tile // CHUNK) for pr in range(n_pairs)]

    def scores(j, pr):
        lanes = slice(pr * LANES, (pr + 1) * LANES)
        q2 = q_scr[j * CHUNK:(j + 1) * CHUNK, lanes]
        zero = jnp.zeros_like(q2)
        qs = jnp.concatenate([jnp.where(low_half, q2, zero), jnp.where(low_half, zero, q2)],
                             axis=0)
        s = _dot_nt(qs, kbuf[j * CHUNK:j * CHUNK + BAND, lanes]) + bias_ref[pr]
        valid = (t * tile + j * CHUNK + kidx - PAD) >= PAD
        return jnp.where(valid, s, NEG)

    def attend(j, pr, s):
        lanes = slice(pr * LANES, (pr + 1) * LANES)
        o = _softmax_pv([s], [vbuf[j * CHUNK:j * CHUNK + BAND, lanes]])
        st["attn", j, pr] = jnp.where(low_half, o[:CHUNK], o[CHUNK:])

    total_w = sum(wt for wt, _ in dense)
    done_w = 0
    pending = list(dense)
    s_cur = scores(*units[0])
    for i, unit in enumerate(units):
        s_next = scores(*units[i + 1]) if i + 1 < len(units) else None
        share = total_w * (i + 1) // max(len(units) - DENSE_EARLY, 1)
        while pending and (done_w < share or i + 1 == len(units)):
            wt, piece = pending.pop(0)
            piece()
            done_w += wt
        attend(*unit, s_cur)
        s_cur = s_next

    for j, pr in units:
        attn_scr[j * CHUNK:(j + 1) * CHUNK, pr * LANES:(pr + 1) * LANES] = st.pop(("attn", j, pr))

    for r in range(0, PAD, tile):
        n = min(tile, PAD - r)
        kbuf[r:r + n, :] = kbuf[r + tile:r + tile + n, :]
        vbuf[r:r + n, :] = vbuf[r + tile:r + tile + n, :]


def _const_spec(arr):
    nd = arr.ndim
    return pl.BlockSpec(arr.shape, lambda *_: (0,) * nd, pipeline_mode=pl.Buffered(1))


def _prompt_layer(x, p, bias, weights, final_norm, tile):
    b, seq, d = x.shape
    aw = weights["g_attn_out"].shape[1]
    lw = weights["g_lru_out"].shape[1]
    assert seq % tile == 0 and PAD % tile == 0 and seq >= PAD and tile % CHUNK == 0
    assert tile & (tile - 1) == 0 and aw % LANES == 0
    n_tiles = seq // tile
    wlist = [weights[n] for n in WEIGHT_NAMES]
    kern = functools.partial(_prompt_kernel, tile=tile, seq=seq, final_norm=final_norm)
    out_shape = (
        jax.ShapeDtypeStruct((b, seq, d), F32),
        jax.ShapeDtypeStruct((b, PAD, aw), F32),
        jax.ShapeDtypeStruct((b, PAD, aw), F32),
        jax.ShapeDtypeStruct((b, CONV_WIDTH - 1, lw), F32),
        jax.ShapeDtypeStruct((b, 1, lw), F32),
    )
    cur = lambda i, j: (i, jnp.minimum(j, n_tiles - 1), 0)
    prev = lambda i, j: (i, jnp.maximum(j - 1, 0), 0)
    in_specs = [
        pl.BlockSpec((1, tile, d), cur),
        pl.BlockSpec((1, tile, d), prev),
        pl.BlockSpec((1, tile, p.shape[-1]), prev),
        _const_spec(bias),
    ] + [_const_spec(a) for a in wlist]
    out_specs = (
        pl.BlockSpec((1, tile, d), prev),
        pl.BlockSpec((1, PAD, aw), lambda i, j: (i, 0, 0)),
        pl.BlockSpec((1, PAD, aw), lambda i, j: (i, 0, 0)),
        pl.BlockSpec((1, CONV_WIDTH - 1, lw), lambda i, j: (i, 0, 0)),
        pl.BlockSpec((1, 1, lw), lambda i, j: (i, 0, 0)),
    )
    scratch = [
        pltpu.VMEM((PAD + tile, aw), BF16),
        pltpu.VMEM((PAD + tile, aw), BF16),
        pltpu.VMEM((tile, aw), BF16),
        pltpu.VMEM((tile, aw), F32),
        pltpu.VMEM((tile, lw), F32),
        pltpu.VMEM((SUBLANES + tile, lw), F32),
        pltpu.VMEM((1, lw), F32),
    ]
    return pl.pallas_call(
        kern,
        out_shape=out_shape,
        grid=(b, n_tiles + 1),
        in_specs=in_specs,
        out_specs=out_specs,
        scratch_shapes=scratch,
        compiler_params=pltpu.CompilerParams(
            dimension_semantics=("arbitrary", "arbitrary"), vmem_limit_bytes=VMEM_LIMIT),
        name="prompt_layer",
    )(x, x, p, bias, *wlist)


FRONT_NAMES = ("g_mix", "w_in", "conv_w", "conv_b", "w_r", "b_r", "w_i", "b_i", "lam",
               "g_attn_out", "g_lru_out")
BACK_NAMES = ("g_attn_out", "g_lru_out", "w_out", "g_ffn", "w_ffn_gate", "w_ffn_up",
              "w_ffn_down", "g_ple", "w_ple_gate", "w_ple_proj", "g_final")


def _sample_front_kernel(*refs, n_heads):
    x_ref, ck_ref, cv_ref, sconv_ref, sh_ref, bias_ref = refs[:6]
    nw = len(FRONT_NAMES)
    w = dict(zip(FRONT_NAMES, refs[6:6 + nw]))
    attn_ref, lru_ref, ko_ref, vo_ref, convo_ref, ho_ref = refs[6 + nw:12 + nw]
    (xr_buf,) = refs[12 + nw:]

    nb, t, d = x_ref.shape
    n_cache = ck_ref.shape[3]
    aw = n_heads * HEAD_DIM
    rows = nb * t

    x = x_ref[...].reshape(rows, d)
    q, k, v, xr, gr = _project(x, w)
    ko_ref[...] = k.reshape(nb, t, aw)
    vo_ref[...] = v.reshape(nb, t, aw)
    qb = q.astype(BF16)
    kb_new = k.astype(BF16)
    vb_new = v.astype(BF16)

    def scores(b, h):
        rs = slice(b * t, (b + 1) * t)
        cols = slice(h * HEAD_DIM, (h + 1) * HEAD_DIM)
        qh = qb[rs, cols]
        kc_t = ck_ref[b, h].astype(BF16)
        s_c = _dot(qh, kc_t) + bias_ref[h, 0:t, PAD - n_cache:PAD]
        s_n = _dot_nt(qh, kb_new[rs, cols]) + bias_ref[h, 0:t, PAD:PAD + t]
        return s_c, s_n

    s_cur = [scores(0, h) for h in range(n_heads)]
    for b in range(nb):
        s_next = [scores(b + 1, h) for h in range(n_heads)] if b + 1 < nb else None
        outs = []
        for h in range(n_heads):
            vc_t = cv_ref[b, h].astype(BF16)
            vn = vb_new[b * t:(b + 1) * t, h * HEAD_DIM:(h + 1) * HEAD_DIM]
            outs.append(_softmax_pv(list(s_cur[h]), [vc_t, vn], [True, False]))
        attn_ref[b] = jnp.concatenate(outs, axis=-1)
        s_cur = s_next

    seg_rows = SUBLANES + t
    xcs = []
    for b in range(nb):
        base = b * seg_rows
        xr_buf[base + SUBLANES - (CONV_WIDTH - 1):base + SUBLANES, :] = sconv_ref[b]
        xr_buf[base + SUBLANES:base + seg_rows, :] = xr[b * t:(b + 1) * t, :]
        xcs.append(_conv(xr_buf, base + SUBLANES, t, w))
        convo_ref[b] = xr_buf[base + seg_rows - (CONV_WIDTH - 1):base + seg_rows, :]
    xc = jnp.concatenate(xcs, axis=0)
    h0_rows = jnp.concatenate(
        [jnp.broadcast_to(sh_ref[b], (t, sh_ref.shape[2])) for b in range(nb)], axis=0)
    hs, lru_out = _rglru(xc, gr, h0_rows, t, w)
    for b in range(nb):
        ho_ref[b] = hs[(b + 1) * t - 1:(b + 1) * t, :]
    lru_ref[...] = lru_out.reshape(nb, t, lru_out.shape[1])


def _sample_back_kernel(*refs, final_norm):
    x_ref, attn_ref, lru_ref, p_ref = refs[:4]
    nw = len(BACK_NAMES)
    w = dict(zip(BACK_NAMES, refs[4:4 + nw]))
    y_ref = refs[4 + nw]
    y_ref[...] = _dense_tail(x_ref[...], attn_ref[...], lru_ref[...], p_ref[...], w, final_norm)


def _sample_layer(x, p, cache_k, cache_v, state_conv, state_h, bias, weights, final_norm, group,
                  back_rows):
    nbatch, t, d = x.shape
    n_heads = bias.shape[0]
    aw = n_heads * HEAD_DIM
    n_cache = cache_k.shape[3]
    lw = weights["g_lru_out"].shape[1]
    assert nbatch % group == 0 and t % SUBLANES == 0 and t & (t - 1) == 0
    assert CONV_WIDTH - 1 <= t <= CHUNK and n_cache <= PAD and cache_k.shape[1:3] == (n_heads, HEAD_DIM)
    assert (nbatch * t) % back_rows == 0
    state_h3 = state_h.reshape(nbatch, 1, lw)

    def grp(shape):
        nd = len(shape)
        return pl.BlockSpec((group,) + tuple(shape[1:]), lambda i: (i,) + (0,) * (nd - 1))

    front_w = [weights[n] for n in FRONT_NAMES]
    front_out = (
        jax.ShapeDtypeStruct((nbatch, t, aw), F32),
        jax.ShapeDtypeStruct((nbatch, t, lw), F32),
        jax.ShapeDtypeStruct((nbatch, t, aw), F32),
        jax.ShapeDtypeStruct((nbatch, t, aw), F32),
        jax.ShapeDtypeStruct((nbatch, CONV_WIDTH - 1, lw), F32),
        jax.ShapeDtypeStruct((nbatch, 1, lw), F32),
    )
    ins = [x, cache_k, cache_v, state_conv, state_h3]
    attn, lru_out, k_new, v_new, conv_new, h_new = pl.pallas_call(
        functools.partial(_sample_front_kernel, n_heads=n_heads),
        out_shape=front_out,
        grid=(nbatch // group,),
        in_specs=[grp(a.shape) for a in ins] + [_const_spec(bias)]
        + [_const_spec(a) for a in front_w],
        out_specs=tuple(grp(o.shape) for o in front_out),
        scratch_shapes=[pltpu.VMEM((group * (SUBLANES + t), lw), F32)],
        compiler_params=pltpu.CompilerParams(
            dimension_semantics=("arbitrary",), vmem_limit_bytes=VMEM_LIMIT),
        name="sample_front",
    )(*ins, bias, *front_w)

    rows = nbatch * t
    back_w = [weights[n] for n in BACK_NAMES]
    flat = [x.reshape(rows, d), attn.reshape(rows, aw), lru_out.reshape(rows, lw),
            p.reshape(rows, p.shape[-1])]
    y = pl.pallas_call(
        functools.partial(_sample_back_kernel, final_norm=final_norm),
        out_shape=jax.ShapeDtypeStruct((rows, d), F32),
        grid=(rows // back_rows,),
        in_specs=[pl.BlockSpec((back_rows, a.shape[1]), lambda i: (i, 0)) for a in flat]
        + [_const_spec(a) for a in back_w],
        out_specs=pl.BlockSpec((back_rows, d), lambda i: (i, 0)),
        compiler_params=pltpu.CompilerParams(
            dimension_semantics=("arbitrary",), vmem_limit_bytes=VMEM_LIMIT),
        name="sample_back",
    )(*flat, *back_w)
    return y.reshape(nbatch, t, d), k_new, v_new, conv_new, h_new


def _block_diag(wb):
    n, kk, jj = wb.shape
    eye = jnp.eye(n, dtype=wb.dtype)
    return (wb[:, :, None, :] * eye[:, None, :, None]).reshape(n * kk, n * jj)


PROMPT_TILE = 256
SAMPLE_GROUP = 8
SAMPLE_BACK_ROWS = 256


def kernel(x_prompt, x_sample, p_prompt, p_sample, cache_k, cache_v, state_conv, state_h, g_mix, w_in, conv_w, conv_b, w_rgate, b_rgate, w_igate, b_igate, lru_lambda, rel_bias_table, g_attn_out, g_lru_out, w_out, g_ffn, w_ffn_gate, w_ffn_up, w_ffn_down, g_ple, w_ple_gate, w_ple_proj, g_final):
    depth = w_in.shape[0]
    hp, hs = x_prompt, x_sample
    outs = [[] for _ in range(8)]
    for i in range(depth):
        row = lambda a: a[i].reshape(1, -1).astype(F32)
        weights = {
            "g_mix": row(g_mix), "w_in": w_in[i].astype(BF16),
            "conv_w": conv_w[i].astype(F32), "conv_b": row(conv_b),
            "w_r": _block_diag(w_rgate[i]).astype(BF16), "b_r": row(b_rgate),
            "w_i": _block_diag(w_igate[i]).astype(BF16), "b_i": row(b_igate),
            "lam": row(lru_lambda),
            "g_attn_out": row(g_attn_out), "g_lru_out": row(g_lru_out),
            "w_out": w_out[i].astype(BF16), "g_ffn": row(g_ffn),
            "w_ffn_gate": w_ffn_gate[i].astype(BF16), "w_ffn_up": w_ffn_up[i].astype(BF16),
            "w_ffn_down": w_ffn_down[i].astype(BF16), "g_ple": row(g_ple),
            "w_ple_gate": w_ple_gate[i].astype(BF16), "w_ple_proj": w_ple_proj[i].astype(BF16),
            "g_final": g_final.reshape(1, -1).astype(F32),
        }
        final = i == depth - 1
        bias = _bias_table(rel_bias_table[i])
        n_heads = rel_bias_table.shape[1]
        aw = n_heads * HEAD_DIM

        hp, k1, v1, c1, r1 = _prompt_layer(
            hp, p_prompt[i], bias.reshape(n_heads // 2, 2 * CHUNK, BAND), weights, final,
            PROMPT_TILE)
        nb, nc = cache_k.shape[1:3]
        hs, k2, v2, c2, r2 = _sample_layer(
            hs, p_sample[i], jnp.transpose(cache_k[i], (0, 2, 3, 1)),
            jnp.transpose(cache_v[i], (0, 2, 3, 1)), state_conv[i], state_h[i], bias,
            weights, final, SAMPLE_GROUP, SAMPLE_BACK_ROWS)
        bp, keep = k1.shape[:2]
        ts = k2.shape[1]
        for lst, val in zip(outs, (
                k1.reshape(bp, keep, n_heads, HEAD_DIM), v1.reshape(bp, keep, n_heads, HEAD_DIM),
                c1, r1.reshape(bp, -1),
                k2.reshape(nb, ts, n_heads, HEAD_DIM), v2.reshape(nb, ts, n_heads, HEAD_DIM),
                c2, r2.reshape(nb, -1))):
            lst.append(val)
    return (hp, hs) + tuple(jnp.stack(l) for l in outs)
```

```python
import functools

import jax
import jax.numpy as jnp
from jax import lax
from jax.experimental import pallas as pl
from jax.experimental.pallas import tpu as pltpu

CHUNK = 64
LEFT_CHUNKS = 8
PAD = LEFT_CHUNKS * CHUNK
BAND = PAD + CHUNK
HEAD_DIM = 64
CONV_WIDTH = 4
LRU_C = 8.0
REL_CLIP = 128
EPS = 1e-6
NEG = -1e30
SCALE = HEAD_DIM ** -0.5
LOG2E = 1.4426950408889634
LANES = 128
SUBLANES = 8
MXU_DIM = 256
EXT = 640
VMEM_LIMIT = 60 * 1024 * 1024
FFN_CHUNK = 4 * MXU_DIM

F32 = jnp.float32
BF16 = jnp.bfloat16


def _dot(a, b):
    return jnp.dot(a, b, preferred_element_type=F32)


def _dot_nt(a, b):
    return lax.dot_general(a, b, (((1,), (1,)), ((), ())), preferred_element_type=F32)


def _rms(x, g):
    ms = jnp.mean(x * x, axis=-1, keepdims=True)
    return (x * lax.rsqrt(ms + EPS)) * g


def _sigmoid(x):
    return 1.0 / (1.0 + jnp.exp2(x * (-LOG2E)))


def _gelu_tanh(x):
    c = -0.7978845608028654 * 2.0 * LOG2E
    return x / (1.0 + jnp.exp2(x * (c + (c * 0.044715) * (x * x))))


def _shift_rows(x, d, fill, seg):
    rows = x.shape[0]
    if seg == rows and d % SUBLANES == 0:
        return jnp.concatenate([jnp.full((d, x.shape[1]), fill, x.dtype), x[:rows - d]], axis=0)
    rolled = pltpu.roll(x, d, axis=0)
    row = lax.broadcasted_iota(jnp.int32, x.shape, 0)
    return jnp.where((row & (seg - 1)) >= d, rolled, fill)


def _scan_distances(seg):
    return [1 << b for b in range(seg.bit_length() - 1)]


def _scan_step(a, u, d, seg):
    u = u + a * _shift_rows(u, d, 0.0, seg)
    if 2 * d < seg:
        a = a * _shift_rows(a, d, 1.0, seg)
    return a, u


def _sqrt_nonneg(y):
    return jnp.where(y > 0.0, y * lax.rsqrt(y), 0.0)


def _lru_inputs(xc, r_pre, i_pre, w):
    r = _sigmoid(r_pre + w["b_r"][...])
    ig = _sigmoid(i_pre + w["b_i"][...])
    z = -w["lam"][...]
    softplus = jnp.maximum(z, 0.0) + jnp.log1p(jnp.exp(-jnp.abs(z)))
    a = jnp.exp2(r * ((-LRU_C * LOG2E) * softplus))
    u = _sqrt_nonneg(1.0 - a * a) * (ig * xc)
    return a, u


def _gate_dot(xcb, w_ref):
    n = w_ref.shape[0]
    if n % MXU_DIM:
        return _dot(xcb, w_ref[...])
    return jnp.concatenate(
        [_dot(xcb[:, c:c + MXU_DIM], w_ref[c:c + MXU_DIM, c:c + MXU_DIM])
         for c in range(0, n, MXU_DIM)], axis=1)


def _rglru(xc, gr, h0_rows, seg, w):
    xcb = xc.astype(BF16)
    a, u = _lru_inputs(xc, _gate_dot(xcb, w["w_r"]), _gate_dot(xcb, w["w_i"]), w)
    row = lax.broadcasted_iota(jnp.int32, xc.shape, 0)
    u = u + jnp.where((row & (seg - 1)) == 0, a * h0_rows, 0.0)
    for d in _scan_distances(seg):
        a, u = _scan_step(a, u, d, seg)
    return u, u * _gelu_tanh(gr)


def _scan_blocked(a, u, h0):
    shape = a.shape
    grouped = (shape[0] // SUBLANES, SUBLANES, shape[1])
    a = a.reshape(grouped)
    u = u.reshape(grouped)
    row = lax.broadcasted_iota(jnp.int32, grouped, 1)
    for d in _scan_distances(SUBLANES):
        inside = row >= d
        u = u + jnp.where(inside, a, 0.0) * pltpu.roll(u, d, axis=1)
        a = a * jnp.where(inside, pltpu.roll(a, d, axis=1), 1.0)
    a = a.reshape(shape)
    u = u.reshape(shape)
    carry = jnp.broadcast_to(h0, (SUBLANES, a.shape[1]))
    groups = []
    for g in range(a.shape[0] // SUBLANES):
        rows = slice(g * SUBLANES, (g + 1) * SUBLANES)
        h = u[rows] + a[rows] * carry
        groups.append(h)
        carry = jnp.broadcast_to(h[SUBLANES - 1:SUBLANES], h.shape)
    return jnp.concatenate(groups, axis=0)


def _mix_out(h, attn, lru_out, w):
    mixed = jnp.concatenate(
        [_rms(attn, w["g_attn_out"][...]), _rms(lru_out, w["g_lru_out"][...])], axis=-1)
    return h + _dot(mixed.astype(BF16), w["w_out"][...])


def _ffn_piece(hn, acc, w, c0, c1):
    gate = _dot(hn, w["w_ffn_gate"][:, c0:c1])
    up = _dot(hn, w["w_ffn_up"][:, c0:c1])
    act = (gate * _sigmoid(gate)) * up
    return acc + _dot(act.astype(BF16), w["w_ffn_down"][c0:c1, :])


def _ffn_bounds(d_ff, step):
    return [(c, min(c + step, d_ff)) for c in range(0, d_ff, step)]


def _ple_out(h, p, w, final_norm):
    gate = _sigmoid(_dot(_rms(h, w["g_ple"][...]).astype(BF16), w["w_ple_gate"][...]))
    h = h + _dot(p.astype(BF16), w["w_ple_proj"][...]) * gate
    if final_norm:
        h = _rms(h, w["g_final"][...])
    return h


def _dense_tail(h, attn, lru_out, p, w, final_norm):
    h = _mix_out(h, attn, lru_out, w)
    hn = _rms(h, w["g_ffn"][...]).astype(BF16)
    d_ff = w["w_ffn_gate"].shape[1]
    for c0, c1 in _ffn_bounds(d_ff, -(-d_ff // (2 * LANES)) * LANES):
        h = _ffn_piece(hn, h, w, c0, c1)
    return _ple_out(h, p, w, final_norm)


def _softmax_pv(scores, values, transposed=None):
    transposed = transposed or [False] * len(values)
    m = functools.reduce(jnp.maximum, [jnp.max(s, axis=-1, keepdims=True) for s in scores])
    num = None
    den = None
    for s, v, v_t in zip(scores, values, transposed):
        e = jnp.exp2(s - m)
        l = jnp.sum(e, axis=-1, keepdims=True)
        o = _dot_nt(e.astype(BF16), v) if v_t else _dot(e.astype(BF16), v)
        num = o if num is None else num + o
        den = l if den is None else den + l
    return num / den


WEIGHT_NAMES = (
    "g_mix", "w_in", "conv_w", "conv_b", "w_r", "b_r", "w_i", "b_i", "lam",
    "g_attn_out", "g_lru_out", "w_out", "g_ffn", "w_ffn_gate", "w_ffn_up", "w_ffn_down",
    "g_ple", "w_ple_gate", "w_ple_proj", "g_final")


def _order_after(x, *deps):
    bits = None
    for dep in deps:
        b = pltpu.bitcast(dep[:SUBLANES, :LANES], jnp.uint32)
        bits = b if bits is None else bits | b
    zero = pltpu.bitcast((bits >> 16) >> 16, F32)
    head = x[:SUBLANES] + jnp.concatenate([zero] * (x.shape[1] // LANES), axis=1)
    return jnp.concatenate([head, x[SUBLANES:]], axis=0)


def _project(x, w):
    xn = _rms(x, w["g_mix"][...]).astype(BF16)
    proj = _dot(xn, w["w_in"][...])
    aw = w["g_attn_out"].shape[1]
    lw = w["g_lru_out"].shape[1]
    q = proj[:, :aw] * (SCALE * LOG2E)
    k = proj[:, aw:2 * aw]
    v = proj[:, 2 * aw:3 * aw]
    xr = proj[:, 3 * aw:3 * aw + lw]
    gr = proj[:, 3 * aw + lw:]
    return q, k, v, xr, gr


def _conv(xr_buf, base, rows, w):
    cw = w["conv_w"]
    ext = xr_buf[base - SUBLANES:base + rows, :]
    xc = w["conv_b"][...] + ext[SUBLANES:] * cw[CONV_WIDTH - 1:CONV_WIDTH, :]
    for back in range(1, CONV_WIDTH):
        j = CONV_WIDTH - 1 - back
        xc = xc + pltpu.roll(ext, back, axis=0)[SUBLANES:] * cw[j:j + 1, :]
    return xc


def _bias_kernel(ext_ref, bias_ref):
    n_heads = ext_ref.shape[0]
    for h in range(n_heads):
        rows = jnp.broadcast_to(ext_ref[h], (CHUNK, EXT))
        rolled = pltpu.roll(rows, EXT - (CHUNK - 1), axis=1, stride=1, stride_axis=0)
        bias_ref[h] = rolled[:, :BAND] * LOG2E


def _bias_table(rel_table):
    n_heads, rel_size = rel_table.shape
    assert rel_size == REL_CLIP + CHUNK
    left = (BAND - 1) - REL_CLIP
    ext = jnp.pad(rel_table.astype(F32), ((0, 0), (left, EXT - left - rel_size)), mode="edge")
    return pl.pallas_call(
        _bias_kernel,
        out_shape=jax.ShapeDtypeStruct((n_heads, CHUNK, BAND), F32),
        name="rel_bias_expand",
    )(ext.reshape(n_heads, 1, EXT))


def _prompt_kernel(*refs, tile, seq, final_norm):
    x_ref, xprev_ref, pprev_ref, bias_ref = refs[:4]
    nw = len(WEIGHT_NAMES)
    w = dict(zip(WEIGHT_NAMES, refs[4:4 + nw]))
    y_ref, ko_ref, vo_ref, convo_ref, ho_ref = refs[4 + nw:9 + nw]
    kbuf, vbuf, q_scr, attn_scr, gr_scr, xr_buf, h_scr = refs[9 + nw:]

    t = pl.program_id(1)
    n_tiles = seq // tile
    n_pairs = kbuf.shape[1] // LANES
    keep_from = seq - PAD
    d_model = x_ref.shape[2]
    d_ff = w["w_ffn_gate"].shape[1]
    rows_w = tile // 16

    lane = lax.broadcasted_iota(jnp.int32, (CHUNK, LANES), 1)
    low_half = lane < HEAD_DIM
    kidx = lax.broadcasted_iota(jnp.int32, (1, BAND), 1)
    units = [(j, pr) for j in range(tile // CHUNK) for pr in range(n_pairs)]

    def step(front, back):
        st = {}

        if back:
            xc = _conv(xr_buf, SUBLANES, tile, w)
            xcb = xc.astype(BF16)
            r_pre = _gate_dot(xcb, w["w_r"])
            i_pre = _gate_dot(xcb, w["w_i"])
            h0 = h_scr[...]
            gr_prev = gr_scr[...]
            xr_tail = xr_buf[tile:tile + SUBLANES, :]

        if front:
            x = x_ref[0]
            if back:
                x = _order_after(x, r_pre, i_pre)
            q, k, v, xr, gr = _project(x, w)

        if back:
            a, u = _lru_inputs(xc, r_pre, i_pre, w)
            u = _scan_blocked(a, u, h0)
            lru_out = u * _gelu_tanh(gr_prev)
            h_scr[...] = u[tile - 1:tile, :]
            xr_buf[0:SUBLANES, :] = xr_tail

        if front:
            q_scr[...] = q.astype(BF16)
            kbuf[PAD:PAD + tile, :] = k.astype(BF16)
            vbuf[PAD:PAD + tile, :] = v.astype(BF16)
            xr_buf[SUBLANES:SUBLANES + tile, :] = xr
            gr_scr[...] = gr

            @pl.when(t * tile >= keep_from)
            def _():
                off = pl.multiple_of(t * tile - keep_from, tile)
                ko_ref[0, pl.ds(off, tile), :] = k
                vo_ref[0, pl.ds(off, tile), :] = v

        if not front:
            convo_ref[0] = xr_buf[SUBLANES - (CONV_WIDTH - 1):SUBLANES, :]
            ho_ref[0] = h_scr[...]

        dense = []

        def ffn_norm():
            st["hn"] = _rms(st["h"], w["g_ffn"][...]).astype(BF16)

        def ffn_gate(c0, c1):
            st["g"] = _dot(st["hn"], w["w_ffn_gate"][:, c0:c1])

        def ffn_up(c0, c1):
            g = st.pop("g")
            st["act"] = ((g * _sigmoid(g)) * _dot(st["hn"], w["w_ffn_up"][:, c0:c1])).astype(BF16)

        def ffn_down(c0, c1):
            st["h"] = st["h"] + _dot(st.pop("act"), w["w_ffn_down"][c0:c1, :])

        def ple_norm():
            st["hpn"] = _rms(st["h"], w["g_ple"][...]).astype(BF16)
            st["pg"] = []

        def ple_gate(c0, c1):
            st["pg"].append(_sigmoid(_dot(st["hpn"], w["w_ple_gate"][:, c0:c1])))

        def ple_out():
            h = st["h"] + _dot(pprev_ref[0].astype(BF16), w["w_ple_proj"][...]) * jnp.concatenate(
                st.pop("pg"), axis=-1)
            if final_norm:
                h = _rms(h, w["g_final"][...])
            y_ref[0] = h

        if back:
            st["h"] = _mix_out(xprev_ref[0], attn_scr[...], lru_out, w)
            dense += [(0, ffn_norm)]
            for c0, c1 in _ffn_bounds(d_ff, FFN_CHUNK):
                wt = rows_w * (d_model // MXU_DIM) * -(-(c1 - c0) // MXU_DIM)
                dense += [(wt, functools.partial(ffn_gate, c0, c1)),
                          (wt, functools.partial(ffn_up, c0, c1)),
                          (wt, functools.partial(ffn_down, c0, c1))]
            dense += [(0, ple_norm)]
            for c0, c1 in _ffn_bounds(d_model, MXU_DIM):
                dense += [(rows_w * (d_model // MXU_DIM), functools.partial(ple_gate, c0, c1))]
            dense += [(rows_w * (d_model // MXU_DIM), ple_out)]

        if not front:
            for _, piece in dense:
                piece()
            return

        def scores(j, pr):
            lanes = slice(pr * LANES, (pr + 1) * LANES)
            q2 = q_scr[j * CHUNK:(j + 1) * CHUNK, lanes]
            zero = jnp.zeros_like(q2)
            qs = jnp.concatenate([jnp.where(low_half, q2, zero), jnp.where(low_half, zero, q2)],
                                 axis=0)
            s = _dot_nt(qs, kbuf[j * CHUNK:j * CHUNK + BAND, lanes]) + bias_ref[pr]
            valid = (t * tile + j * CHUNK + kidx - PAD) >= PAD
            return jnp.where(valid, s, NEG)

        def attend(j, pr, s):
            lanes = slice(pr * LANES, (pr + 1) * LANES)
            o = _softmax_pv([s], [vbuf[j * CHUNK:j * CHUNK + BAND, lanes]])
            st["attn", j, pr] = jnp.where(low_half, o[:CHUNK], o[CHUNK:])

        total_w = sum(wt for wt, _ in dense)
        done_w = 0
        pending = list(dense)
        s_cur = scores(*units[0])
        for i, unit in enumerate(units):
            s_next = scores(*units[i + 1]) if i + 1 < len(units) else None
            share = total_w * (i + 1) // len(units)
            while pending and (done_w < share or i + 1 == len(units)):
                wt, piece = pending.pop(0)
                piece()
                done_w += wt
            attend(*unit, s_cur)
            s_cur = s_next

        for j, pr in units:
            attn_scr[j * CHUNK:(j + 1) * CHUNK, pr * LANES:(pr + 1) * LANES] = st.pop(
                ("attn", j, pr))

        for r in range(0, PAD, tile):
            n = min(tile, PAD - r)
            kbuf[r:r + n, :] = kbuf[r + tile:r + tile + n, :]
            vbuf[r:r + n, :] = vbuf[r + tile:r + tile + n, :]

    @pl.when(t == 0)
    def _():
        kbuf[0:PAD, :] = jnp.zeros((PAD, kbuf.shape[1]), BF16)
        vbuf[0:PAD, :] = jnp.zeros((PAD, vbuf.shape[1]), BF16)
        xr_buf[0:SUBLANES, :] = jnp.zeros((SUBLANES, xr_buf.shape[1]), F32)
        h_scr[...] = jnp.zeros(h_scr.shape, F32)
        step(front=True, back=False)

    @pl.when(jnp.logical_and(t > 0, t < n_tiles))
    def _():
        step(front=True, back=True)

    @pl.when(t == n_tiles)
    def _():
        step(front=False, back=True)


def _const_spec(arr):
    nd = arr.ndim
    return pl.BlockSpec(arr.shape, lambda *_: (0,) * nd, pipeline_mode=pl.Buffered(1))


def _prompt_layer(x, p, bias, weights, final_norm, tile):
    b, seq, d = x.shape
    aw = weights["g_attn_out"].shape[1]
    lw = weights["g_lru_out"].shape[1]
    assert seq % tile == 0 and PAD % tile == 0 and seq >= PAD and tile % CHUNK == 0
    assert tile & (tile - 1) == 0 and aw % LANES == 0
    n_tiles = seq // tile
    wlist = [weights[n] for n in WEIGHT_NAMES]
    kern = functools.partial(_prompt_kernel, tile=tile, seq=seq, final_norm=final_norm)
    out_shape = (
        jax.ShapeDtypeStruct((b, seq, d), F32),
        jax.ShapeDtypeStruct((b, PAD, aw), F32),
        jax.ShapeDtypeStruct((b, PAD, aw), F32),
        jax.ShapeDtypeStruct((b, CONV_WIDTH - 1, lw), F32),
        jax.ShapeDtypeStruct((b, 1, lw), F32),
    )
    cur = lambda i, j: (i, jnp.minimum(j, n_tiles - 1), 0)
    prev = lambda i, j: (i, jnp.maximum(j - 1, 0), 0)
    in_specs = [
        pl.BlockSpec((1, tile, d), cur),
        pl.BlockSpec((1, tile, d), prev),
        pl.BlockSpec((1, tile, p.shape[-1]), prev),
        _const_spec(bias),
    ] + [_const_spec(a) for a in wlist]
    out_specs = (
        pl.BlockSpec((1, tile, d), prev),
        pl.BlockSpec((1, PAD, aw), lambda i, j: (i, 0, 0)),
        pl.BlockSpec((1, PAD, aw), lambda i, j: (i, 0, 0)),
        pl.BlockSpec((1, CONV_WIDTH - 1, lw), lambda i, j: (i, 0, 0)),
        pl.BlockSpec((1, 1, lw), lambda i, j: (i, 0, 0)),
    )
    scratch = [
        pltpu.VMEM((PAD + tile, aw), BF16),
        pltpu.VMEM((PAD + tile, aw), BF16),
        pltpu.VMEM((tile, aw), BF16),
        pltpu.VMEM((tile, aw), F32),
        pltpu.VMEM((tile, lw), F32),
        pltpu.VMEM((SUBLANES + tile, lw), F32),
        pltpu.VMEM((1, lw), F32),
    ]
    return pl.pallas_call(
        kern,
        out_shape=out_shape,
        grid=(b, n_tiles + 1),
        in_specs=in_specs,
        out_specs=out_specs,
        scratch_shapes=scratch,
        compiler_params=pltpu.CompilerParams(
            dimension_semantics=("arbitrary", "arbitrary"), vmem_limit_bytes=VMEM_LIMIT),
        name="prompt_layer",
    )(x, x, p, bias, *wlist)


FRONT_NAMES = ("g_mix", "w_in", "conv_w", "conv_b", "w_r", "b_r", "w_i", "b_i", "lam",
               "g_attn_out", "g_lru_out")
BACK_NAMES = ("g_attn_out", "g_lru_out", "w_out", "g_ffn", "w_ffn_gate", "w_ffn_up",
              "w_ffn_down", "g_ple", "w_ple_gate", "w_ple_proj", "g_final")


def _sample_front_kernel(*refs, n_heads):
    x_ref, ck_ref, cv_ref, sconv_ref, sh_ref, bias_ref = refs[:6]
    nw = len(FRONT_NAMES)
    w = dict(zip(FRONT_NAMES, refs[6:6 + nw]))
    attn_ref, lru_ref, ko_ref, vo_ref, convo_ref, ho_ref = refs[6 + nw:12 + nw]
    (xr_buf,) = refs[12 + nw:]

    nb, t, d = x_ref.shape
    n_cache = ck_ref.shape[3]
    aw = n_heads * HEAD_DIM
    rows = nb * t

    x = x_ref[...].reshape(rows, d)
    q, k, v, xr, gr = _project(x, w)
    ko_ref[...] = k.reshape(nb, t, aw)
    vo_ref[...] = v.reshape(nb, t, aw)
    qb = q.astype(BF16)
    kb_new = k.astype(BF16)
    vb_new = v.astype(BF16)

    def scores(b, h):
        rs = slice(b * t, (b + 1) * t)
        cols = slice(h * HEAD_DIM, (h + 1) * HEAD_DIM)
        qh = qb[rs, cols]
        kc_t = ck_ref[b, h].astype(BF16)
        s_c = _dot(qh, kc_t) + bias_ref[h, 0:t, PAD - n_cache:PAD]
        s_n = _dot_nt(qh, kb_new[rs, cols]) + bias_ref[h, 0:t, PAD:PAD + t]
        return s_c, s_n

    s_cur = [scores(0, h) for h in range(n_heads)]
    for b in range(nb):
        s_next = [scores(b + 1, h) for h in range(n_heads)] if b + 1 < nb else None
        outs = []
        for h in range(n_heads):
            vc_t = cv_ref[b, h].astype(BF16)
            vn = vb_new[b * t:(b + 1) * t, h * HEAD_DIM:(h + 1) * HEAD_DIM]
            outs.append(_softmax_pv(list(s_cur[h]), [vc_t, vn], [True, False]))
        attn_ref[b] = jnp.concatenate(outs, axis=-1)
        s_cur = s_next

    seg_rows = SUBLANES + t
    xcs = []
    for b in range(nb):
        base = b * seg_rows
        xr_buf[base + SUBLANES - (CONV_WIDTH - 1):base + SUBLANES, :] = sconv_ref[b]
        xr_buf[base + SUBLANES:base + seg_rows, :] = xr[b * t:(b + 1) * t, :]
        xcs.append(_conv(xr_buf, base + SUBLANES, t, w))
        convo_ref[b] = xr_buf[base + seg_rows - (CONV_WIDTH - 1):base + seg_rows, :]
    xc = jnp.concatenate(xcs, axis=0)
    h0_rows = jnp.concatenate(
        [jnp.broadcast_to(sh_ref[b], (t, sh_ref.shape[2])) for b in range(nb)], axis=0)
    hs, lru_out = _rglru(xc, gr, h0_rows, t, w)
    for b in range(nb):
        ho_ref[b] = hs[(b + 1) * t - 1:(b + 1) * t, :]
    lru_ref[...] = lru_out.reshape(nb, t, lru_out.shape[1])


def _sample_back_kernel(*refs, final_norm):
    x_ref, attn_ref, lru_ref, p_ref = refs[:4]
    nw = len(BACK_NAMES)
    w = dict(zip(BACK_NAMES, refs[4:4 + nw]))
    y_ref = refs[4 + nw]
    y_ref[...] = _dense_tail(x_ref[...], attn_ref[...], lru_ref[...], p_ref[...], w, final_norm)


def _sample_layer(x, p, cache_k, cache_v, state_conv, state_h, bias, weights, final_norm, group,
                  back_rows):
    nbatch, t, d = x.shape
    n_heads = bias.shape[0]
    aw = n_heads * HEAD_DIM
    n_cache = cache_k.shape[3]
    lw = weights["g_lru_out"].shape[1]
    assert nbatch % group == 0 and t % SUBLANES == 0 and t & (t - 1) == 0
    assert CONV_WIDTH - 1 <= t <= CHUNK and n_cache <= PAD and cache_k.shape[1:3] == (n_heads, HEAD_DIM)
    assert (nbatch * t) % back_rows == 0
    state_h3 = state_h.reshape(nbatch, 1, lw)

    def grp(shape):
        nd = len(shape)
        return pl.BlockSpec((group,) + tuple(shape[1:]), lambda i: (i,) + (0,) * (nd - 1))

    front_w = [weights[n] for n in FRONT_NAMES]
    front_out = (
        jax.ShapeDtypeStruct((nbatch, t, aw), F32),
        jax.ShapeDtypeStruct((nbatch, t, lw), F32),
        jax.ShapeDtypeStruct((nbatch, t, aw), F32),
        jax.ShapeDtypeStruct((nbatch, t, aw), F32),
        jax.ShapeDtypeStruct((nbatch, CONV_WIDTH - 1, lw), F32),
        jax.ShapeDtypeStruct((nbatch, 1, lw), F32),
    )
    ins = [x, cache_k, cache_v, state_conv, state_h3]
    attn, lru_out, k_new, v_new, conv_new, h_new = pl.pallas_call(
        functools.partial(_sample_front_kernel, n_heads=n_heads),
        out_shape=front_out,
        grid=(nbatch // group,),
        in_specs=[grp(a.shape) for a in ins] + [_const_spec(bias)]
        + [_const_spec(a) for a in front_w],
        out_specs=tuple(grp(o.shape) for o in front_out),
        scratch_shapes=[pltpu.VMEM((group * (SUBLANES + t), lw), F32)],
        compiler_params=pltpu.CompilerParams(
            dimension_semantics=("arbitrary",), vmem_limit_bytes=VMEM_LIMIT),
        name="sample_front",
    )(*ins, bias, *front_w)

    rows = nbatch * t
    back_w = [weights[n] for n in BACK_NAMES]
    flat = [x.reshape(rows, d), attn.reshape(rows, aw), lru_out.reshape(rows, lw),
            p.reshape(rows, p.shape[-1])]
    y = pl.pallas_call(
        functools.partial(_sample_back_kernel, final_norm=final_norm),
        out_shape=jax.ShapeDtypeStruct((rows, d), F32),
        grid=(rows // back_rows,),
        in_specs=[pl.BlockSpec((back_rows, a.shape[1]), lambda i: (i, 0)) for a in flat]
        + [_const_spec(a) for a in back_w],
        out_specs=pl.BlockSpec((back_rows, d), lambda i: (i, 0)),
        compiler_params=pltpu.CompilerParams(
            dimension_semantics=("arbitrary",), vmem_limit_bytes=VMEM_LIMIT),
        name="sample_back",
    )(*flat, *back_w)
    return y.reshape(nbatch, t, d), k_new, v_new, conv_new, h_new


def _block_diag(wb):
    n, kk, jj = wb.shape
    eye = jnp.eye(n, dtype=wb.dtype)
    return (wb[:, :, None, :] * eye[:, None, :, None]).reshape(n * kk, n * jj)


PROMPT_TILE = 256
SAMPLE_GROUP = 8
SAMPLE_BACK_ROWS = 256


def kernel(x_prompt, x_sample, p_prompt, p_sample, cache_k, cache_v, state_conv, state_h, g_mix, w_in, conv_w, conv_b, w_rgate, b_rgate, w_igate, b_igate, lru_lambda, rel_bias_table, g_attn_out, g_lru_out, w_out, g_ffn, w_ffn_gate, w_ffn_up, w_ffn_down, g_ple, w_ple_gate, w_ple_proj, g_final):
    depth = w_in.shape[0]
    hp, hs = x_prompt, x_sample
    outs = [[] for _ in range(8)]
    for i in range(depth):
        row = lambda a: a[i].reshape(1, -1).astype(F32)
        weights = {
            "g_mix": row(g_mix), "w_in": w_in[i].astype(BF16),
            "conv_w": conv_w[i].astype(F32), "conv_b": row(conv_b),
            "w_r": _block_diag(w_rgate[i]).astype(BF16), "b_r": row(b_rgate),
            "w_i": _block_diag(w_igate[i]).astype(BF16), "b_i": row(b_igate),
            "lam": row(lru_lambda),
            "g_attn_out": row(g_attn_out), "g_lru_out": row(g_lru_out),
            "w_out": w_out[i].astype(BF16), "g_ffn": row(g_ffn),
            "w_ffn_gate": w_ffn_gate[i].astype(BF16), "w_ffn_up": w_ffn_up[i].astype(BF16),
            "w_ffn_down": w_ffn_down[i].astype(BF16), "g_ple": row(g_ple),
            "w_ple_gate": w_ple_gate[i].astype(BF16), "w_ple_proj": w_ple_proj[i].astype(BF16),
            "g_final": g_final.reshape(1, -1).astype(F32),
        }
        final = i == depth - 1
        bias = _bias_table(rel_bias_table[i])
        n_heads = rel_bias_table.shape[1]
        aw = n_heads * HEAD_DIM

        hp, k1, v1, c1, r1 = _prompt_layer(
            hp, p_prompt[i], bias.reshape(n_heads // 2, 2 * CHUNK, BAND), weights, final,
            PROMPT_TILE)
        nb, nc = cache_k.shape[1:3]
        hs, k2, v2, c2, r2 = _sample_layer(
            hs, p_sample[i], jnp.transpose(cache_k[i], (0, 2, 3, 1)),
            jnp.transpose(cache_v[i], (0, 2, 3, 1)), state_conv[i], state_h[i], bias,
            weights, final, SAMPLE_GROUP, SAMPLE_BACK_ROWS)
        bp, keep = k1.shape[:2]
        ts = k2.shape[1]
        for lst, val in zip(outs, (
                k1.reshape(bp, keep, n_heads, HEAD_DIM), v1.reshape(bp, keep, n_heads, HEAD_DIM),
                c1, r1.reshape(bp, -1),
                k2.reshape(nb, ts, n_heads, HEAD_DIM), v2.reshape(nb, ts, n_heads, HEAD_DIM),
                c2, r2.reshape(nb, -1))):
            lst.append(val)
    return (hp, hs) + tuple(jnp.stack(l) for l in outs)
```

```python
import functools

import jax
import jax.numpy as jnp
from jax import lax
from jax.experimental import pallas as pl
from jax.experimental.pallas import tpu as pltpu

CHUNK = 64
LEFT_CHUNKS = 8
PAD = LEFT_CHUNKS * CHUNK
BAND = PAD + CHUNK
HEAD_DIM = 64
CONV_WIDTH = 4
LRU_C = 8.0
REL_CLIP = 128
EPS = 1e-6
NEG = -1e30
SCALE = HEAD_DIM ** -0.5
LOG2E = 1.4426950408889634
LANES = 128
SUBLANES = 8
MXU_DIM = 256
EXT = 640
WINDOW = BAND + CHUNK
VMEM_LIMIT = 60 * 1024 * 1024
FFN_CHUNK = 4 * MXU_DIM

F32 = jnp.float32
BF16 = jnp.bfloat16


def _dot(a, b):
    return jnp.dot(a, b, preferred_element_type=F32)


def _dot_nt(a, b):
    return lax.dot_general(a, b, (((1,), (1,)), ((), ())), preferred_element_type=F32)


def _rms(x, g):
    ms = jnp.mean(x * x, axis=-1, keepdims=True)
    return (x * lax.rsqrt(ms + EPS)) * g


def _sigmoid(x):
    return 1.0 / (1.0 + jnp.exp2(x * (-LOG2E)))


def _gelu_tanh(x):
    c = -0.7978845608028654 * 2.0 * LOG2E
    return x / (1.0 + jnp.exp2(x * (c + (c * 0.044715) * (x * x))))


def _shift_rows(x, d, fill, seg):
    rows = x.shape[0]
    if seg == rows and d % SUBLANES == 0:
        return jnp.concatenate([jnp.full((d, x.shape[1]), fill, x.dtype), x[:rows - d]], axis=0)
    rolled = pltpu.roll(x, d, axis=0)
    row = lax.broadcasted_iota(jnp.int32, x.shape, 0)
    return jnp.where((row & (seg - 1)) >= d, rolled, fill)


def _scan_distances(seg):
    return [1 << b for b in range(seg.bit_length() - 1)]


def _scan_step(a, u, d, seg):
    u = u + a * _shift_rows(u, d, 0.0, seg)
    if 2 * d < seg:
        a = a * _shift_rows(a, d, 1.0, seg)
    return a, u


def _sqrt_nonneg(y):
    return jnp.where(y > 0.0, y * lax.rsqrt(y), 0.0)


def _lru_inputs(xc, r_pre, i_pre, w):
    r = _sigmoid(r_pre + w["b_r"][...])
    ig = _sigmoid(i_pre + w["b_i"][...])
    z = -w["lam"][...]
    softplus = jnp.maximum(z, 0.0) + jnp.log1p(jnp.exp(-jnp.abs(z)))
    a = jnp.exp2(r * ((-LRU_C * LOG2E) * softplus))
    u = _sqrt_nonneg(1.0 - a * a) * (ig * xc)
    return a, u


def _gate_dot(xcb, w_ref):
    n = w_ref.shape[0]
    if n % MXU_DIM:
        return _dot(xcb, w_ref[...])
    return jnp.concatenate(
        [_dot(xcb[:, c:c + MXU_DIM], w_ref[c:c + MXU_DIM, c:c + MXU_DIM])
         for c in range(0, n, MXU_DIM)], axis=1)


def _rglru(xc, gr, h0_rows, seg, w):
    xcb = xc.astype(BF16)
    a, u = _lru_inputs(xc, _gate_dot(xcb, w["w_r"]), _gate_dot(xcb, w["w_i"]), w)
    row = lax.broadcasted_iota(jnp.int32, xc.shape, 0)
    u = u + jnp.where((row & (seg - 1)) == 0, a * h0_rows, 0.0)
    for d in _scan_distances(seg):
        a, u = _scan_step(a, u, d, seg)
    return u, u * _gelu_tanh(gr)


def _scan_blocked(a, u, h0):
    shape = a.shape
    grouped = (shape[0] // SUBLANES, SUBLANES, shape[1])
    a = a.reshape(grouped)
    u = u.reshape(grouped)
    row = lax.broadcasted_iota(jnp.int32, grouped, 1)
    for d in _scan_distances(SUBLANES):
        inside = row >= d
        u = u + jnp.where(inside, a, 0.0) * pltpu.roll(u, d, axis=1)
        a = a * jnp.where(inside, pltpu.roll(a, d, axis=1), 1.0)
    a = a.reshape(shape)
    u = u.reshape(shape)
    carry = jnp.broadcast_to(h0, (SUBLANES, a.shape[1]))
    groups = []
    for g in range(a.shape[0] // SUBLANES):
        rows = slice(g * SUBLANES, (g + 1) * SUBLANES)
        h = u[rows] + a[rows] * carry
        groups.append(h)
        carry = jnp.broadcast_to(h[SUBLANES - 1:SUBLANES], h.shape)
    return jnp.concatenate(groups, axis=0)


def _mix_out(h, attn, lru_out, w):
    mixed = jnp.concatenate(
        [_rms(attn, w["g_attn_out"][...]), _rms(lru_out, w["g_lru_out"][...])], axis=-1)
    return h + _dot(mixed.astype(BF16), w["w_out"][...])


def _ffn_piece(hn, acc, w, c0, c1):
    gate = _dot(hn, w["w_ffn_gate"][:, c0:c1])
    up = _dot(hn, w["w_ffn_up"][:, c0:c1])
    act = (gate * _sigmoid(gate)) * up
    return acc + _dot(act.astype(BF16), w["w_ffn_down"][c0:c1, :])


def _ffn_bounds(d_ff, step):
    return [(c, min(c + step, d_ff)) for c in range(0, d_ff, step)]


def _ple_out(h, p, w, final_norm):
    gate = _sigmoid(_dot(_rms(h, w["g_ple"][...]).astype(BF16), w["w_ple_gate"][...]))
    h = h + _dot(p.astype(BF16), w["w_ple_proj"][...]) * gate
    if final_norm:
        h = _rms(h, w["g_final"][...])
    return h


def _dense_tail(h, attn, lru_out, p, w, final_norm):
    h = _mix_out(h, attn, lru_out, w)
    hn = _rms(h, w["g_ffn"][...]).astype(BF16)
    d_ff = w["w_ffn_gate"].shape[1]
    for c0, c1 in _ffn_bounds(d_ff, -(-d_ff // (2 * LANES)) * LANES):
        h = _ffn_piece(hn, h, w, c0, c1)
    return _ple_out(h, p, w, final_norm)


def _softmax_pv(scores, values, transposed=None):
    transposed = transposed or [False] * len(values)
    m = functools.reduce(jnp.maximum, [jnp.max(s, axis=-1, keepdims=True) for s in scores])
    num = None
    den = None
    for s, v, v_t in zip(scores, values, transposed):
        e = jnp.exp2(s - m)
        l = jnp.sum(e, axis=-1, keepdims=True)
        o = _dot_nt(e.astype(BF16), v) if v_t else _dot(e.astype(BF16), v)
        num = o if num is None else num + o
        den = l if den is None else den + l
    return num / den


WEIGHT_NAMES = (
    "g_mix", "w_in", "conv_w", "conv_b", "w_r", "b_r", "w_i", "b_i", "lam",
    "g_attn_out", "g_lru_out", "w_out", "g_ffn", "w_ffn_gate", "w_ffn_up", "w_ffn_down",
    "g_ple", "w_ple_gate", "w_ple_proj", "g_final")


def _order_after(x, *deps):
    bits = None
    for dep in deps:
        b = pltpu.bitcast(dep[:SUBLANES, :LANES], jnp.uint32)
        bits = b if bits is None else bits | b
    zero = pltpu.bitcast((bits >> 16) >> 16, F32)
    head = x[:SUBLANES] + jnp.concatenate([zero] * (x.shape[1] // LANES), axis=1)
    return jnp.concatenate([head, x[SUBLANES:]], axis=0)


def _project(x, w):
    xn = _rms(x, w["g_mix"][...]).astype(BF16)
    proj = _dot(xn, w["w_in"][...])
    aw = w["g_attn_out"].shape[1]
    lw = w["g_lru_out"].shape[1]
    q = proj[:, :aw] * (SCALE * LOG2E)
    k = proj[:, aw:2 * aw]
    v = proj[:, 2 * aw:3 * aw]
    xr = proj[:, 3 * aw:3 * aw + lw]
    gr = proj[:, 3 * aw + lw:]
    return q, k, v, xr, gr


def _conv(xr_buf, base, rows, w):
    cw = w["conv_w"]
    ext = xr_buf[base - SUBLANES:base + rows, :]
    xc = w["conv_b"][...] + ext[SUBLANES:] * cw[CONV_WIDTH - 1:CONV_WIDTH, :]
    for back in range(1, CONV_WIDTH):
        j = CONV_WIDTH - 1 - back
        xc = xc + pltpu.roll(ext, back, axis=0)[SUBLANES:] * cw[j:j + 1, :]
    return xc


def _bias_kernel(ext_ref, bias_ref, bias2_ref):
    n_heads = ext_ref.shape[0]
    col = lax.broadcasted_iota(jnp.int32, (CHUNK, WINDOW), 1)
    for h in range(n_heads):
        rows = jnp.broadcast_to(ext_ref[h], (CHUNK, EXT))
        rolled = pltpu.roll(rows, EXT - (CHUNK - 1), axis=1, stride=1, stride_axis=0) * LOG2E
        bias_ref[h] = rolled[:, :BAND]
        for e in range(2):
            shifted = rolled if e == 0 else pltpu.roll(rolled, CHUNK, axis=1)
            in_band = jnp.logical_and(col >= e * CHUNK, col < e * CHUNK + BAND)
            r0 = e * 2 * CHUNK + (h % 2) * CHUNK
            bias2_ref[h // 2, r0:r0 + CHUNK, :] = jnp.where(in_band, shifted, NEG)


def _bias_tables(rel_table):
    n_heads, rel_size = rel_table.shape
    assert rel_size == REL_CLIP + CHUNK and EXT == WINDOW and n_heads % 2 == 0
    left = (BAND - 1) - REL_CLIP
    ext = jnp.pad(rel_table.astype(F32), ((0, 0), (left, EXT - left - rel_size)), mode="edge")
    return pl.pallas_call(
        _bias_kernel,
        out_shape=(jax.ShapeDtypeStruct((n_heads, CHUNK, BAND), F32),
                   jax.ShapeDtypeStruct((n_heads // 2, 4 * CHUNK, WINDOW), F32)),
        name="rel_bias_expand",
    )(ext.reshape(n_heads, 1, EXT))


def _prompt_kernel(*refs, tile, seq, final_norm):
    x_ref, xprev_ref, pprev_ref, bias_ref = refs[:4]
    nw = len(WEIGHT_NAMES)
    w = dict(zip(WEIGHT_NAMES, refs[4:4 + nw]))
    y_ref, ko_ref, vo_ref, convo_ref, ho_ref = refs[4 + nw:9 + nw]
    kbuf, vbuf, q_scr, attn_scr, gr_scr, xr_buf, h_scr = refs[9 + nw:]

    t = pl.program_id(1)
    n_tiles = seq // tile
    n_pairs = kbuf.shape[1] // LANES
    keep_from = seq - PAD
    d_model = x_ref.shape[2]
    d_ff = w["w_ffn_gate"].shape[1]
    rows_w = tile // 16

    lane = lax.broadcasted_iota(jnp.int32, (CHUNK, LANES), 1)
    low_half = lane < HEAD_DIM
    units = [(jj, pr) for jj in range(tile // (2 * CHUNK)) for pr in range(n_pairs)]
    band_row = lax.broadcasted_iota(jnp.int32, (4 * CHUNK, WINDOW), 0)
    band_col = lax.broadcasted_iota(jnp.int32, (4 * CHUNK, WINDOW), 1)
    band_lo = jnp.where(band_row >= 2 * CHUNK, CHUNK, 0)
    masked = jnp.where(band_col >= band_lo, NEG, 2 * NEG)
    masked = jnp.where(band_col < band_lo + BAND, masked, 2 * NEG)
    kidx = lax.broadcasted_iota(jnp.int32, (1, WINDOW), 1)

    def step(front, back):
        st = {}

        if back:
            xc = _conv(xr_buf, SUBLANES, tile, w)
            xcb = xc.astype(BF16)
            r_pre = _gate_dot(xcb, w["w_r"])
            i_pre = _gate_dot(xcb, w["w_i"])
            h0 = h_scr[...]
            gr_prev = gr_scr[...]
            xr_tail = xr_buf[tile:tile + SUBLANES, :]

        if front:
            x = x_ref[0]
            if back:
                x = _order_after(x, r_pre, i_pre)
            q, k, v, xr, gr = _project(x, w)

        if back:
            a, u = _lru_inputs(xc, r_pre, i_pre, w)
            u = _scan_blocked(a, u, h0)
            lru_out = u * _gelu_tanh(gr_prev)
            h_scr[...] = u[tile - 1:tile, :]
            xr_buf[0:SUBLANES, :] = xr_tail

        if front:
            q_scr[...] = q.astype(BF16)
            kbuf[PAD:PAD + tile, :] = k.astype(BF16)
            vbuf[PAD:PAD + tile, :] = v.astype(BF16)
            xr_buf[SUBLANES:SUBLANES + tile, :] = xr
            gr_scr[...] = gr

            @pl.when(t * tile >= keep_from)
            def _():
                off = pl.multiple_of(t * tile - keep_from, tile)
                ko_ref[0, pl.ds(off, tile), :] = k
                vo_ref[0, pl.ds(off, tile), :] = v

        if not front:
            convo_ref[0] = xr_buf[SUBLANES - (CONV_WIDTH - 1):SUBLANES, :]
            ho_ref[0] = h_scr[...]

        dense = []

        def ffn_norm():
            st["hn"] = _rms(st["h"], w["g_ffn"][...]).astype(BF16)

        def ffn_gate(c0, c1):
            st["g"] = _dot(st["hn"], w["w_ffn_gate"][:, c0:c1])

        def ffn_up(c0, c1):
            g = st.pop("g")
            st["act"] = ((g * _sigmoid(g)) * _dot(st["hn"], w["w_ffn_up"][:, c0:c1])).astype(BF16)

        def ffn_down(c0, c1):
            st["h"] = st["h"] + _dot(st.pop("act"), w["w_ffn_down"][c0:c1, :])

        def ple_norm():
            st["hpn"] = _rms(st["h"], w["g_ple"][...]).astype(BF16)
            st["pg"] = []

        def ple_gate(c0, c1):
            st["pg"].append(_sigmoid(_dot(st["hpn"], w["w_ple_gate"][:, c0:c1])))

        def ple_out():
            h = st["h"] + _dot(pprev_ref[0].astype(BF16), w["w_ple_proj"][...]) * jnp.concatenate(
                st.pop("pg"), axis=-1)
            if final_norm:
                h = _rms(h, w["g_final"][...])
            y_ref[0] = h

        if back:
            st["h"] = _mix_out(xprev_ref[0], attn_scr[...], lru_out, w)
            dense += [(0, ffn_norm)]
            for c0, c1 in _ffn_bounds(d_ff, FFN_CHUNK):
                wt = rows_w * (d_model // MXU_DIM) * -(-(c1 - c0) // MXU_DIM)
                dense += [(wt, functools.partial(ffn_gate, c0, c1)),
                          (wt, functools.partial(ffn_up, c0, c1)),
                          (wt, functools.partial(ffn_down, c0, c1))]
            dense += [(0, ple_norm)]
            for c0, c1 in _ffn_bounds(d_model, MXU_DIM):
                dense += [(rows_w * (d_model // MXU_DIM), functools.partial(ple_gate, c0, c1))]
            dense += [(rows_w * (d_model // MXU_DIM), ple_out)]

        if not front:
            for _, piece in dense:
                piece()
            return

        def scores(jj, pr):
            lanes = slice(pr * LANES, (pr + 1) * LANES)
            r0 = jj * 2 * CHUNK
            parts = []
            for c in range(2):
                q2 = q_scr[r0 + c * CHUNK:r0 + (c + 1) * CHUNK, lanes]
                zero = jnp.zeros_like(q2)
                parts += [jnp.where(low_half, q2, zero), jnp.where(low_half, zero, q2)]
            s = _dot_nt(jnp.concatenate(parts, axis=0), kbuf[r0:r0 + WINDOW, lanes]) + bias_ref[pr]
            valid = (t * tile + r0 + kidx - PAD) >= PAD
            return jnp.where(valid, s, masked)

        def attend(jj, pr, s):
            lanes = slice(pr * LANES, (pr + 1) * LANES)
            r0 = jj * 2 * CHUNK
            o = _softmax_pv([s], [vbuf[r0:r0 + WINDOW, lanes]])
            for c in range(2):
                st["attn", 2 * jj + c, pr] = jnp.where(
                    low_half, o[2 * c * CHUNK:(2 * c + 1) * CHUNK],
                    o[(2 * c + 1) * CHUNK:(2 * c + 2) * CHUNK])

        total_w = sum(wt for wt, _ in dense)
        done_w = 0
        pending = list(dense)
        s_cur = scores(*units[0])
        for i, unit in enumerate(units):
            s_next = scores(*units[i + 1]) if i + 1 < len(units) else None
            share = total_w * (i + 1) // len(units)
            while pending and (done_w < share or i + 1 == len(units)):
                wt, piece = pending.pop(0)
                piece()
                done_w += wt
            attend(*unit, s_cur)
            s_cur = s_next

        for j in range(tile // CHUNK):
            for pr in range(n_pairs):
                attn_scr[j * CHUNK:(j + 1) * CHUNK, pr * LANES:(pr + 1) * LANES] = st.pop(
                    ("attn", j, pr))

        for r in range(0, PAD, tile):
            n = min(tile, PAD - r)
            kbuf[r:r + n, :] = kbuf[r + tile:r + tile + n, :]
            vbuf[r:r + n, :] = vbuf[r + tile:r + tile + n, :]

    @pl.when(t == 0)
    def _():
        kbuf[0:PAD, :] = jnp.zeros((PAD, kbuf.shape[1]), BF16)
        vbuf[0:PAD, :] = jnp.zeros((PAD, vbuf.shape[1]), BF16)
        xr_buf[0:SUBLANES, :] = jnp.zeros((SUBLANES, xr_buf.shape[1]), F32)
        h_scr[...] = jnp.zeros(h_scr.shape, F32)
        step(front=True, back=False)

    @pl.when(jnp.logical_and(t > 0, t < n_tiles))
    def _():
        step(front=True, back=True)

    @pl.when(t == n_tiles)
    def _():
        step(front=False, back=True)


def _const_spec(arr):
    nd = arr.ndim
    return pl.BlockSpec(arr.shape, lambda *_: (0,) * nd, pipeline_mode=pl.Buffered(1))


def _prompt_layer(x, p, bias, weights, final_norm, tile):
    b, seq, d = x.shape
    aw = weights["g_attn_out"].shape[1]
    lw = weights["g_lru_out"].shape[1]
    assert seq % tile == 0 and PAD % tile == 0 and seq >= PAD and tile % CHUNK == 0
    assert tile & (tile - 1) == 0 and aw % LANES == 0 and tile % (2 * CHUNK) == 0
    n_tiles = seq // tile
    wlist = [weights[n] for n in WEIGHT_NAMES]
    kern = functools.partial(_prompt_kernel, tile=tile, seq=seq, final_norm=final_norm)
    out_shape = (
        jax.ShapeDtypeStruct((b, seq, d), F32),
        jax.ShapeDtypeStruct((b, PAD, aw), F32),
        jax.ShapeDtypeStruct((b, PAD, aw), F32),
        jax.ShapeDtypeStruct((b, CONV_WIDTH - 1, lw), F32),
        jax.ShapeDtypeStruct((b, 1, lw), F32),
    )
    cur = lambda i, j: (i, jnp.minimum(j, n_tiles - 1), 0)
    prev = lambda i, j: (i, jnp.maximum(j - 1, 0), 0)
    in_specs = [
        pl.BlockSpec((1, tile, d), cur),
        pl.BlockSpec((1, tile, d), prev),
        pl.BlockSpec((1, tile, p.shape[-1]), prev),
        _const_spec(bias),
    ] + [_const_spec(a) for a in wlist]
    out_specs = (
        pl.BlockSpec((1, tile, d), prev),
        pl.BlockSpec((1, PAD, aw), lambda i, j: (i, 0, 0)),
        pl.BlockSpec((1, PAD, aw), lambda i, j: (i, 0, 0)),
        pl.BlockSpec((1, CONV_WIDTH - 1, lw), lambda i, j: (i, 0, 0)),
        pl.BlockSpec((1, 1, lw), lambda i, j: (i, 0, 0)),
    )
    scratch = [
        pltpu.VMEM((PAD + tile, aw), BF16),
        pltpu.VMEM((PAD + tile, aw), BF16),
        pltpu.VMEM((tile, aw), BF16),
        pltpu.VMEM((tile, aw), F32),
        pltpu.VMEM((tile, lw), F32),
        pltpu.VMEM((SUBLANES + tile, lw), F32),
        pltpu.VMEM((1, lw), F32),
    ]
    return pl.pallas_call(
        kern,
        out_shape=out_shape,
        grid=(b, n_tiles + 1),
        in_specs=in_specs,
        out_specs=out_specs,
        scratch_shapes=scratch,
        compiler_params=pltpu.CompilerParams(
            dimension_semantics=("arbitrary", "arbitrary"), vmem_limit_bytes=VMEM_LIMIT),
        name="prompt_layer",
    )(x, x, p, bias, *wlist)


FRONT_NAMES = ("g_mix", "w_in", "conv_w", "conv_b", "w_r", "b_r", "w_i", "b_i", "lam",
               "g_attn_out", "g_lru_out")
BACK_NAMES = ("g_attn_out", "g_lru_out", "w_out", "g_ffn", "w_ffn_gate", "w_ffn_up",
              "w_ffn_down", "g_ple", "w_ple_gate", "w_ple_proj", "g_final")


def _sample_front_kernel(*refs, n_heads):
    x_ref, ck_ref, cv_ref, sconv_ref, sh_ref, bias_ref = refs[:6]
    nw = len(FRONT_NAMES)
    w = dict(zip(FRONT_NAMES, refs[6:6 + nw]))
    attn_ref, lru_ref, ko_ref, vo_ref, convo_ref, ho_ref = refs[6 + nw:12 + nw]
    (xr_buf,) = refs[12 + nw:]

    nb, t, d = x_ref.shape
    n_cache = ck_ref.shape[3]
    aw = n_heads * HEAD_DIM
    rows = nb * t

    x = x_ref[...].reshape(rows, d)
    q, k, v, xr, gr = _project(x, w)
    ko_ref[...] = k.reshape(nb, t, aw)
    vo_ref[...] = v.reshape(nb, t, aw)
    qb = q.astype(BF16)
    kb_new = k.astype(BF16)
    vb_new = v.astype(BF16)

    def scores(b, h):
        rs = slice(b * t, (b + 1) * t)
        cols = slice(h * HEAD_DIM, (h + 1) * HEAD_DIM)
        qh = qb[rs, cols]
        kc_t = ck_ref[b, h].astype(BF16)
        s_c = _dot(qh, kc_t) + bias_ref[h, 0:t, PAD - n_cache:PAD]
        s_n = _dot_nt(qh, kb_new[rs, cols]) + bias_ref[h, 0:t, PAD:PAD + t]
        return s_c, s_n

    s_cur = [scores(0, h) for h in range(n_heads)]
    for b in range(nb):
        s_next = [scores(b + 1, h) for h in range(n_heads)] if b + 1 < nb else None
        outs = []
        for h in range(n_heads):
            vc_t = cv_ref[b, h].astype(BF16)
            vn = vb_new[b * t:(b + 1) * t, h * HEAD_DIM:(h + 1) * HEAD_DIM]
            outs.append(_softmax_pv(list(s_cur[h]), [vc_t, vn], [True, False]))
        attn_ref[b] = jnp.concatenate(outs, axis=-1)
        s_cur = s_next

    seg_rows = SUBLANES + t
    xcs = []
    for b in range(nb):
        base = b * seg_rows
        xr_buf[base + SUBLANES - (CONV_WIDTH - 1):base + SUBLANES, :] = sconv_ref[b]
        xr_buf[base + SUBLANES:base + seg_rows, :] = xr[b * t:(b + 1) * t, :]
        xcs.append(_conv(xr_buf, base + SUBLANES, t, w))
        convo_ref[b] = xr_buf[base + seg_rows - (CONV_WIDTH - 1):base + seg_rows, :]
    xc = jnp.concatenate(xcs, axis=0)
    h0_rows = jnp.concatenate(
        [jnp.broadcast_to(sh_ref[b], (t, sh_ref.shape[2])) for b in range(nb)], axis=0)
    hs, lru_out = _rglru(xc, gr, h0_rows, t, w)
    for b in range(nb):
        ho_ref[b] = hs[(b + 1) * t - 1:(b + 1) * t, :]
    lru_ref[...] = lru_out.reshape(nb, t, lru_out.shape[1])


def _sample_back_kernel(*refs, final_norm):
    x_ref, attn_ref, lru_ref, p_ref = refs[:4]
    nw = len(BACK_NAMES)
    w = dict(zip(BACK_NAMES, refs[4:4 + nw]))
    y_ref = refs[4 + nw]
    y_ref[...] = _dense_tail(x_ref[...], attn_ref[...], lru_ref[...], p_ref[...], w, final_norm)


def _sample_layer(x, p, cache_k, cache_v, state_conv, state_h, bias, weights, final_norm, group,
                  back_rows):
    nbatch, t, d = x.shape
    n_heads = bias.shape[0]
    aw = n_heads * HEAD_DIM
    n_cache = cache_k.shape[3]
    lw = weights["g_lru_out"].shape[1]
    assert nbatch % group == 0 and t % SUBLANES == 0 and t & (t - 1) == 0
    assert CONV_WIDTH - 1 <= t <= CHUNK and n_cache <= PAD and cache_k.shape[1:3] == (n_heads, HEAD_DIM)
    assert (nbatch * t) % back_rows == 0
    state_h3 = state_h.reshape(nbatch, 1, lw)

    def grp(shape):
        nd = len(shape)
        return pl.BlockSpec((group,) + tuple(shape[1:]), lambda i: (i,) + (0,) * (nd - 1))

    front_w = [weights[n] for n in FRONT_NAMES]
    front_out = (
        jax.ShapeDtypeStruct((nbatch, t, aw), F32),
        jax.ShapeDtypeStruct((nbatch, t, lw), F32),
        jax.ShapeDtypeStruct((nbatch, t, aw), F32),
        jax.ShapeDtypeStruct((nbatch, t, aw), F32),
        jax.ShapeDtypeStruct((nbatch, CONV_WIDTH - 1, lw), F32),
        jax.ShapeDtypeStruct((nbatch, 1, lw), F32),
    )
    ins = [x, cache_k, cache_v, state_conv, state_h3]
    attn, lru_out, k_new, v_new, conv_new, h_new = pl.pallas_call(
        functools.partial(_sample_front_kernel, n_heads=n_heads),
        out_shape=front_out,
        grid=(nbatch // group,),
        in_specs=[grp(a.shape) for a in ins] + [_const_spec(bias)]
        + [_const_spec(a) for a in front_w],
        out_specs=tuple(grp(o.shape) for o in front_out),
        scratch_shapes=[pltpu.VMEM((group * (SUBLANES + t), lw), F32)],
        compiler_params=pltpu.CompilerParams(
            dimension_semantics=("arbitrary",), vmem_limit_bytes=VMEM_LIMIT),
        name="sample_front",
    )(*ins, bias, *front_w)

    rows = nbatch * t
    back_w = [weights[n] for n in BACK_NAMES]
    flat = [x.reshape(rows, d), attn.reshape(rows, aw), lru_out.reshape(rows, lw),
            p.reshape(rows, p.shape[-1])]
    y = pl.pallas_call(
        functools.partial(_sample_back_kernel, final_norm=final_norm),
        out_shape=jax.ShapeDtypeStruct((rows, d), F32),
        grid=(rows // back_rows,),
        in_specs=[pl.BlockSpec((back_rows, a.shape[1]), lambda i: (i, 0)) for a in flat]
        + [_const_spec(a) for a in back_w],
        out_specs=pl.BlockSpec((back_rows, d), lambda i: (i, 0)),
        compiler_params=pltpu.CompilerParams(
            dimension_semantics=("arbitrary",), vmem_limit_bytes=VMEM_LIMIT),
        name="sample_back",
    )(*flat, *back_w)
    return y.reshape(nbatch, t, d), k_new, v_new, conv_new, h_new


def _block_diag(wb):
    n, kk, jj = wb.shape
    eye = jnp.eye(n, dtype=wb.dtype)
    return (wb[:, :, None, :] * eye[:, None, :, None]).reshape(n * kk, n * jj)


PROMPT_TILE = 256
SAMPLE_GROUP = 8
SAMPLE_BACK_ROWS = 256


def kernel(x_prompt, x_sample, p_prompt, p_sample, cache_k, cache_v, state_conv, state_h, g_mix, w_in, conv_w, conv_b, w_rgate, b_rgate, w_igate, b_igate, lru_lambda, rel_bias_table, g_attn_out, g_lru_out, w_out, g_ffn, w_ffn_gate, w_ffn_up, w_ffn_down, g_ple, w_ple_gate, w_ple_proj, g_final):
    depth = w_in.shape[0]
    hp, hs = x_prompt, x_sample
    outs = [[] for _ in range(8)]
    for i in range(depth):
        row = lambda a: a[i].reshape(1, -1).astype(F32)
        weights = {
            "g_mix": row(g_mix), "w_in": w_in[i].astype(BF16),
            "conv_w": conv_w[i].astype(F32), "conv_b": row(conv_b),
            "w_r": _block_diag(w_rgate[i]).astype(BF16), "b_r": row(b_rgate),
            "w_i": _block_diag(w_igate[i]).astype(BF16), "b_i": row(b_igate),
            "lam": row(lru_lambda),
            "g_attn_out": row(g_attn_out), "g_lru_out": row(g_lru_out),
            "w_out": w_out[i].astype(BF16), "g_ffn": row(g_ffn),
            "w_ffn_gate": w_ffn_gate[i].astype(BF16), "w_ffn_up": w_ffn_up[i].astype(BF16),
            "w_ffn_down": w_ffn_down[i].astype(BF16), "g_ple": row(g_ple),
            "w_ple_gate": w_ple_gate[i].astype(BF16), "w_ple_proj": w_ple_proj[i].astype(BF16),
            "g_final": g_final.reshape(1, -1).astype(F32),
        }
        final = i == depth - 1
        bias, bias2 = _bias_tables(rel_bias_table[i])
        n_heads = rel_bias_table.shape[1]
        aw = n_heads * HEAD_DIM

        hp, k1, v1, c1, r1 = _prompt_layer(
            hp, p_prompt[i], bias2, weights, final, PROMPT_TILE)
        nb, nc = cache_k.shape[1:3]
        hs, k2, v2, c2, r2 = _sample_layer(
            hs, p_sample[i], jnp.transpose(cache_k[i], (0, 2, 3, 1)),
            jnp.transpose(cache_v[i], (0, 2, 3, 1)), state_conv[i], state_h[i], bias,
            weights, final, SAMPLE_GROUP, SAMPLE_BACK_ROWS)
        bp, keep = k1.shape[:2]
        ts = k2.shape[1]
        for lst, val in zip(outs, (
                k1.reshape(bp, keep, n_heads, HEAD_DIM), v1.reshape(bp, keep, n_heads, HEAD_DIM),
                c1, r1.reshape(bp, -1),
                k2.reshape(nb, ts, n_heads, HEAD_DIM), v2.reshape(nb, ts, n_heads, HEAD_DIM),
                c2, r2.reshape(nb, -1))):
            lst.append(val)
    return (hp, hs) + tuple(jnp.stack(l) for l in outs)
```

```python
import functools

import jax
import jax.numpy as jnp
from jax import lax
from jax.experimental import pallas as pl
from jax.experimental.pallas import tpu as pltpu

CHUNK = 64
LEFT_CHUNKS = 8
PAD = LEFT_CHUNKS * CHUNK
BAND = PAD + CHUNK
HEAD_DIM = 64
CONV_WIDTH = 4
LRU_C = 8.0
REL_CLIP = 128
EPS = 1e-6
NEG = -1e30
SCALE = HEAD_DIM ** -0.5
LOG2E = 1.4426950408889634
LANES = 128
SUBLANES = 8
MXU_DIM = 256
EXT = 640
WINDOW = BAND + CHUNK
VMEM_LIMIT = 60 * 1024 * 1024
FFN_CHUNK = 6 * MXU_DIM

F32 = jnp.float32
BF16 = jnp.bfloat16


def _dot(a, b):
    return jnp.dot(a, b, preferred_element_type=F32)


def _dot_nt(a, b):
    return lax.dot_general(a, b, (((1,), (1,)), ((), ())), preferred_element_type=F32)


def _rms(x, g):
    ms = jnp.mean(x * x, axis=-1, keepdims=True)
    return (x * lax.rsqrt(ms + EPS)) * g


def _sigmoid(x):
    return 1.0 / (1.0 + jnp.exp2(x * (-LOG2E)))


def _gelu_tanh(x):
    c = -0.7978845608028654 * 2.0 * LOG2E
    return x / (1.0 + jnp.exp2(x * (c + (c * 0.044715) * (x * x))))


def _shift_rows(x, d, fill, seg):
    rows = x.shape[0]
    if seg == rows and d % SUBLANES == 0:
        return jnp.concatenate([jnp.full((d, x.shape[1]), fill, x.dtype), x[:rows - d]], axis=0)
    rolled = pltpu.roll(x, d, axis=0)
    row = lax.broadcasted_iota(jnp.int32, x.shape, 0)
    return jnp.where((row & (seg - 1)) >= d, rolled, fill)


def _scan_distances(seg):
    return [1 << b for b in range(seg.bit_length() - 1)]


def _scan_step(a, u, d, seg):
    u = u + a * _shift_rows(u, d, 0.0, seg)
    if 2 * d < seg:
        a = a * _shift_rows(a, d, 1.0, seg)
    return a, u


def _sqrt_nonneg(y):
    return jnp.where(y > 0.0, y * lax.rsqrt(y), 0.0)


def _lru_inputs(xc, r_pre, i_pre, w):
    r = _sigmoid(r_pre + w["b_r"][...])
    ig = _sigmoid(i_pre + w["b_i"][...])
    z = -w["lam"][...]
    softplus = jnp.maximum(z, 0.0) + jnp.log1p(jnp.exp(-jnp.abs(z)))
    a = jnp.exp2(r * ((-LRU_C * LOG2E) * softplus))
    u = _sqrt_nonneg(1.0 - a * a) * (ig * xc)
    return a, u


def _gate_dot(xcb, w_ref):
    n = w_ref.shape[0]
    if n % MXU_DIM:
        return _dot(xcb, w_ref[...])
    return jnp.concatenate(
        [_dot(xcb[:, c:c + MXU_DIM], w_ref[c:c + MXU_DIM, c:c + MXU_DIM])
         for c in range(0, n, MXU_DIM)], axis=1)


def _rglru(xc, gr, h0_rows, seg, w):
    xcb = xc.astype(BF16)
    a, u = _lru_inputs(xc, _gate_dot(xcb, w["w_r"]), _gate_dot(xcb, w["w_i"]), w)
    row = lax.broadcasted_iota(jnp.int32, xc.shape, 0)
    u = u + jnp.where((row & (seg - 1)) == 0, a * h0_rows, 0.0)
    for d in _scan_distances(seg):
        a, u = _scan_step(a, u, d, seg)
    return u, u * _gelu_tanh(gr)


def _scan_blocked(a, u, h0):
    shape = a.shape
    grouped = (shape[0] // SUBLANES, SUBLANES, shape[1])
    a = a.reshape(grouped)
    u = u.reshape(grouped)
    row = lax.broadcasted_iota(jnp.int32, grouped, 1)
    for d in _scan_distances(SUBLANES):
        inside = row >= d
        u = u + jnp.where(inside, a, 0.0) * pltpu.roll(u, d, axis=1)
        a = a * jnp.where(inside, pltpu.roll(a, d, axis=1), 1.0)
    a = a.reshape(shape)
    u = u.reshape(shape)
    carry = jnp.broadcast_to(h0, (SUBLANES, a.shape[1]))
    groups = []
    for g in range(a.shape[0] // SUBLANES):
        rows = slice(g * SUBLANES, (g + 1) * SUBLANES)
        h = u[rows] + a[rows] * carry
        groups.append(h)
        carry = jnp.broadcast_to(h[SUBLANES - 1:SUBLANES], h.shape)
    return jnp.concatenate(groups, axis=0)


def _mix_out(h, attn, lru_out, w):
    mixed = jnp.concatenate(
        [_rms(attn, w["g_attn_out"][...]), _rms(lru_out, w["g_lru_out"][...])], axis=-1)
    return h + _dot(mixed.astype(BF16), w["w_out"][...])


def _ffn_piece(hn, acc, w, c0, c1):
    gate = _dot(hn, w["w_ffn_gate"][:, c0:c1])
    up = _dot(hn, w["w_ffn_up"][:, c0:c1])
    act = (gate * _sigmoid(gate)) * up
    return acc + _dot(act.astype(BF16), w["w_ffn_down"][c0:c1, :])


def _ffn_bounds(d_ff, step):
    return [(c, min(c + step, d_ff)) for c in range(0, d_ff, step)]


def _ple_out(h, p, w, final_norm):
    gate = _sigmoid(_dot(_rms(h, w["g_ple"][...]).astype(BF16), w["w_ple_gate"][...]))
    h = h + _dot(p.astype(BF16), w["w_ple_proj"][...]) * gate
    if final_norm:
        h = _rms(h, w["g_final"][...])
    return h


def _dense_tail(h, attn, lru_out, p, w, final_norm):
    h = _mix_out(h, attn, lru_out, w)
    hn = _rms(h, w["g_ffn"][...]).astype(BF16)
    d_ff = w["w_ffn_gate"].shape[1]
    for c0, c1 in _ffn_bounds(d_ff, -(-d_ff // (2 * LANES)) * LANES):
        h = _ffn_piece(hn, h, w, c0, c1)
    return _ple_out(h, p, w, final_norm)


def _softmax_pv(scores, values, transposed=None):
    transposed = transposed or [False] * len(values)
    m = functools.reduce(jnp.maximum, [jnp.max(s, axis=-1, keepdims=True) for s in scores])
    num = None
    den = None
    for s, v, v_t in zip(scores, values, transposed):
        e = jnp.exp2(s - m)
        l = jnp.sum(e, axis=-1, keepdims=True)
        o = _dot_nt(e.astype(BF16), v) if v_t else _dot(e.astype(BF16), v)
        num = o if num is None else num + o
        den = l if den is None else den + l
    return num / den


WEIGHT_NAMES = (
    "g_mix", "w_in", "conv_w", "conv_b", "w_r", "b_r", "w_i", "b_i", "lam",
    "g_attn_out", "g_lru_out", "w_out", "g_ffn", "w_ffn_gate", "w_ffn_up", "w_ffn_down",
    "g_ple", "w_ple_gate", "w_ple_proj", "g_final")


def _order_after(x, *deps):
    bits = None
    for dep in deps:
        b = pltpu.bitcast(dep[:SUBLANES, :LANES], jnp.uint32)
        bits = b if bits is None else bits | b
    zero = pltpu.bitcast((bits >> 16) >> 16, F32)
    head = x[:SUBLANES] + jnp.concatenate([zero] * (x.shape[1] // LANES), axis=1)
    return jnp.concatenate([head, x[SUBLANES:]], axis=0)


def _project(x, w):
    xn = _rms(x, w["g_mix"][...]).astype(BF16)
    proj = _dot(xn, w["w_in"][...])
    aw = w["g_attn_out"].shape[1]
    lw = w["g_lru_out"].shape[1]
    q = proj[:, :aw] * (SCALE * LOG2E)
    k = proj[:, aw:2 * aw]
    v = proj[:, 2 * aw:3 * aw]
    xr = proj[:, 3 * aw:3 * aw + lw]
    gr = proj[:, 3 * aw + lw:]
    return q, k, v, xr, gr


def _conv(xr_buf, base, rows, w):
    cw = w["conv_w"]
    ext = xr_buf[base - SUBLANES:base + rows, :]
    xc = w["conv_b"][...] + ext[SUBLANES:] * cw[CONV_WIDTH - 1:CONV_WIDTH, :]
    for back in range(1, CONV_WIDTH):
        j = CONV_WIDTH - 1 - back
        xc = xc + pltpu.roll(ext, back, axis=0)[SUBLANES:] * cw[j:j + 1, :]
    return xc


def _bias_kernel(ext_ref, bias_ref, bias2_ref):
    n_heads = ext_ref.shape[0]
    col = lax.broadcasted_iota(jnp.int32, (CHUNK, WINDOW), 1)
    for h in range(n_heads):
        rows = jnp.broadcast_to(ext_ref[h], (CHUNK, EXT))
        rolled = pltpu.roll(rows, EXT - (CHUNK - 1), axis=1, stride=1, stride_axis=0) * LOG2E
        bias_ref[h] = rolled[:, :BAND]
        for e in range(2):
            shifted = rolled if e == 0 else pltpu.roll(rolled, CHUNK, axis=1)
            in_band = jnp.logical_and(col >= e * CHUNK, col < e * CHUNK + BAND)
            r0 = e * 2 * CHUNK + (h % 2) * CHUNK
            bias2_ref[h // 2, r0:r0 + CHUNK, :] = jnp.where(in_band, shifted, NEG)


def _bias_tables(rel_table):
    n_heads, rel_size = rel_table.shape
    assert rel_size == REL_CLIP + CHUNK and EXT == WINDOW and n_heads % 2 == 0
    left = (BAND - 1) - REL_CLIP
    ext = jnp.pad(rel_table.astype(F32), ((0, 0), (left, EXT - left - rel_size)), mode="edge")
    return pl.pallas_call(
        _bias_kernel,
        out_shape=(jax.ShapeDtypeStruct((n_heads, CHUNK, BAND), F32),
                   jax.ShapeDtypeStruct((n_heads // 2, 4 * CHUNK, WINDOW), F32)),
        name="rel_bias_expand",
    )(ext.reshape(n_heads, 1, EXT))


def _prompt_kernel(*refs, tile, seq, final_norm):
    x_ref, xprev_ref, pprev_ref, bias_ref = refs[:4]
    nw = len(WEIGHT_NAMES)
    w = dict(zip(WEIGHT_NAMES, refs[4:4 + nw]))
    y_ref, ko_ref, vo_ref, convo_ref, ho_ref = refs[4 + nw:9 + nw]
    kbuf, vbuf, q_scr, attn_scr, gr_scr, xr_buf, h_scr = refs[9 + nw:]

    t = pl.program_id(1)
    n_tiles = seq // tile
    n_pairs = kbuf.shape[1] // LANES
    keep_from = seq - PAD
    d_model = x_ref.shape[2]
    d_ff = w["w_ffn_gate"].shape[1]
    rows_w = tile // 16

    lane = lax.broadcasted_iota(jnp.int32, (CHUNK, LANES), 1)
    low_half = lane < HEAD_DIM
    units = [(jj, pr) for jj in range(tile // (2 * CHUNK)) for pr in range(n_pairs)]
    band_row = lax.broadcasted_iota(jnp.int32, (4 * CHUNK, WINDOW), 0)
    band_col = lax.broadcasted_iota(jnp.int32, (4 * CHUNK, WINDOW), 1)
    band_lo = jnp.where(band_row >= 2 * CHUNK, CHUNK, 0)
    masked = jnp.where(band_col >= band_lo, NEG, 2 * NEG)
    masked = jnp.where(band_col < band_lo + BAND, masked, 2 * NEG)
    kidx = lax.broadcasted_iota(jnp.int32, (1, WINDOW), 1)

    def step(front, back, mask_keys=True):
        st = {}

        if back:
            xc = _conv(xr_buf, SUBLANES, tile, w)
            xcb = xc.astype(BF16)
            r_pre = _gate_dot(xcb, w["w_r"])
            i_pre = _gate_dot(xcb, w["w_i"])
            h0 = h_scr[...]
            gr_prev = gr_scr[...]
            xr_tail = xr_buf[tile:tile + SUBLANES, :]

        if front:
            x = x_ref[0]
            if back:
                x = _order_after(x, r_pre, i_pre)
            q, k, v, xr, gr = _project(x, w)

        if back:
            a, u = _lru_inputs(xc, r_pre, i_pre, w)
            u = _scan_blocked(a, u, h0)
            lru_out = u * _gelu_tanh(gr_prev)
            h_scr[...] = u[tile - 1:tile, :]
            xr_buf[0:SUBLANES, :] = xr_tail

        if front:
            q_scr[...] = q.astype(BF16)
            kbuf[PAD:PAD + tile, :] = k.astype(BF16)
            vbuf[PAD:PAD + tile, :] = v.astype(BF16)
            xr_buf[SUBLANES:SUBLANES + tile, :] = xr
            gr_scr[...] = gr

            @pl.when(t * tile >= keep_from)
            def _():
                off = pl.multiple_of(t * tile - keep_from, tile)
                ko_ref[0, pl.ds(off, tile), :] = k
                vo_ref[0, pl.ds(off, tile), :] = v

        if not front:
            convo_ref[0] = xr_buf[SUBLANES - (CONV_WIDTH - 1):SUBLANES, :]
            ho_ref[0] = h_scr[...]

        dense = []

        def ffn_norm():
            st["hn"] = _rms(st["h"], w["g_ffn"][...]).astype(BF16)

        def ffn_gate(c0, c1):
            st["g"] = _dot(st["hn"], w["w_ffn_gate"][:, c0:c1])

        def ffn_up(c0, c1):
            g = st.pop("g")
            st["act"] = ((g * _sigmoid(g)) * _dot(st["hn"], w["w_ffn_up"][:, c0:c1])).astype(BF16)

        def ffn_down(c0, c1):
            st["h"] = st["h"] + _dot(st.pop("act"), w["w_ffn_down"][c0:c1, :])

        def ple_norm():
            st["hpn"] = _rms(st["h"], w["g_ple"][...]).astype(BF16)
            st["pg"] = []

        def ple_gate(c0, c1):
            st["pg"].append(_sigmoid(_dot(st["hpn"], w["w_ple_gate"][:, c0:c1])))

        def ple_out():
            h = st["h"] + _dot(pprev_ref[0].astype(BF16), w["w_ple_proj"][...]) * jnp.concatenate(
                st.pop("pg"), axis=-1)
            if final_norm:
                h = _rms(h, w["g_final"][...])
            y_ref[0] = h

        if back:
            st["h"] = _mix_out(xprev_ref[0], attn_scr[...], lru_out, w)
            dense += [(0, ffn_norm)]
            for c0, c1 in _ffn_bounds(d_ff, FFN_CHUNK):
                wt = rows_w * (d_model // MXU_DIM) * -(-(c1 - c0) // MXU_DIM)
                dense += [(wt, functools.partial(ffn_gate, c0, c1)),
                          (wt, functools.partial(ffn_up, c0, c1)),
                          (wt, functools.partial(ffn_down, c0, c1))]
            dense += [(0, ple_norm)]
            for c0, c1 in _ffn_bounds(d_model, MXU_DIM):
                dense += [(rows_w * (d_model // MXU_DIM), functools.partial(ple_gate, c0, c1))]
            dense += [(rows_w * (d_model // MXU_DIM), ple_out)]

        if not front:
            for _, piece in dense:
                piece()
            return

        def scores(jj, pr):
            lanes = slice(pr * LANES, (pr + 1) * LANES)
            r0 = jj * 2 * CHUNK
            parts = []
            for c in range(2):
                q2 = q_scr[r0 + c * CHUNK:r0 + (c + 1) * CHUNK, lanes]
                zero = jnp.zeros_like(q2)
                parts += [jnp.where(low_half, q2, zero), jnp.where(low_half, zero, q2)]
            s = _dot_nt(jnp.concatenate(parts, axis=0), kbuf[r0:r0 + WINDOW, lanes]) + bias_ref[pr]
            if not mask_keys:
                return s
            valid = (t * tile + r0 + kidx - PAD) >= PAD
            return jnp.where(valid, s, masked)

        def attend(jj, pr, s):
            lanes = slice(pr * LANES, (pr + 1) * LANES)
            r0 = jj * 2 * CHUNK
            o = _softmax_pv([s], [vbuf[r0:r0 + WINDOW, lanes]])
            for c in range(2):
                st["attn", 2 * jj + c, pr] = jnp.where(
                    low_half, o[2 * c * CHUNK:(2 * c + 1) * CHUNK],
                    o[(2 * c + 1) * CHUNK:(2 * c + 2) * CHUNK])

        total_w = sum(wt for wt, _ in dense)
        done_w = 0
        pending = list(dense)
        s_cur = scores(*units[0])
        for i, unit in enumerate(units):
            s_next = scores(*units[i + 1]) if i + 1 < len(units) else None
            share = total_w * (i + 1) // len(units)
            while pending and (done_w < share or i + 1 == len(units)):
                wt, piece = pending.pop(0)
                piece()
                done_w += wt
            attend(*unit, s_cur)
            s_cur = s_next

        for j in range(tile // CHUNK):
            for pr in range(n_pairs):
                attn_scr[j * CHUNK:(j + 1) * CHUNK, pr * LANES:(pr + 1) * LANES] = st.pop(
                    ("attn", j, pr))

        for r in range(0, PAD, tile):
            n = min(tile, PAD - r)
            kbuf[r:r + n, :] = kbuf[r + tile:r + tile + n, :]
            vbuf[r:r + n, :] = vbuf[r + tile:r + tile + n, :]

    @pl.when(t == 0)
    def _():
        kbuf[0:PAD, :] = jnp.zeros((PAD, kbuf.shape[1]), BF16)
        vbuf[0:PAD, :] = jnp.zeros((PAD, vbuf.shape[1]), BF16)
        xr_buf[0:SUBLANES, :] = jnp.zeros((SUBLANES, xr_buf.shape[1]), F32)
        h_scr[...] = jnp.zeros(h_scr.shape, F32)
        step(front=True, back=False)

    first_unmasked = -(-2 * PAD // tile)

    @pl.when(jnp.logical_and(t > 0, t < min(first_unmasked, n_tiles)))
    def _():
        step(front=True, back=True)

    @pl.when(jnp.logical_and(t >= first_unmasked, t < n_tiles))
    def _():
        step(front=True, back=True, mask_keys=False)

    @pl.when(t == n_tiles)
    def _():
        step(front=False, back=True)


def _const_spec(arr):
    nd = arr.ndim
    return pl.BlockSpec(arr.shape, lambda *_: (0,) * nd, pipeline_mode=pl.Buffered(1))


def _prompt_layer(x, p, bias, weights, final_norm, tile):
    b, seq, d = x.shape
    aw = weights["g_attn_out"].shape[1]
    lw = weights["g_lru_out"].shape[1]
    assert seq % tile == 0 and PAD % tile == 0 and seq >= PAD and tile % CHUNK == 0
    assert tile & (tile - 1) == 0 and aw % LANES == 0 and tile % (2 * CHUNK) == 0
    n_tiles = seq // tile
    wlist = [weights[n] for n in WEIGHT_NAMES]
    kern = functools.partial(_prompt_kernel, tile=tile, seq=seq, final_norm=final_norm)
    out_shape = (
        jax.ShapeDtypeStruct((b, seq, d), F32),
        jax.ShapeDtypeStruct((b, PAD, aw), F32),
        jax.ShapeDtypeStruct((b, PAD, aw), F32),
        jax.ShapeDtypeStruct((b, CONV_WIDTH - 1, lw), F32),
        jax.ShapeDtypeStruct((b, 1, lw), F32),
    )
    cur = lambda i, j: (i, jnp.minimum(j, n_tiles - 1), 0)
    prev = lambda i, j: (i, jnp.maximum(j - 1, 0), 0)
    in_specs = [
        pl.BlockSpec((1, tile, d), cur),
        pl.BlockSpec((1, tile, d), prev),
        pl.BlockSpec((1, tile, p.shape[-1]), prev),
        _const_spec(bias),
    ] + [_const_spec(a) for a in wlist]
    out_specs = (
        pl.BlockSpec((1, tile, d), prev),
        pl.BlockSpec((1, PAD, aw), lambda i, j: (i, 0, 0)),
        pl.BlockSpec((1, PAD, aw), lambda i, j: (i, 0, 0)),
        pl.BlockSpec((1, CONV_WIDTH - 1, lw), lambda i, j: (i, 0, 0)),
        pl.BlockSpec((1, 1, lw), lambda i, j: (i, 0, 0)),
    )
    scratch = [
        pltpu.VMEM((PAD + tile, aw), BF16),
        pltpu.VMEM((PAD + tile, aw), BF16),
        pltpu.VMEM((tile, aw), BF16),
        pltpu.VMEM((tile, aw), F32),
        pltpu.VMEM((tile, lw), F32),
        pltpu.VMEM((SUBLANES + tile, lw), F32),
        pltpu.VMEM((1, lw), F32),
    ]
    return pl.pallas_call(
        kern,
        out_shape=out_shape,
        grid=(b, n_tiles + 1),
        in_specs=in_specs,
        out_specs=out_specs,
        scratch_shapes=scratch,
        compiler_params=pltpu.CompilerParams(
            dimension_semantics=("arbitrary", "arbitrary"), vmem_limit_bytes=VMEM_LIMIT),
        name="prompt_layer",
    )(x, x, p, bias, *wlist)


FRONT_NAMES = ("g_mix", "w_in", "conv_w", "conv_b", "w_r", "b_r", "w_i", "b_i", "lam",
               "g_attn_out", "g_lru_out")
BACK_NAMES = ("g_attn_out", "g_lru_out", "w_out", "g_ffn", "w_ffn_gate", "w_ffn_up",
              "w_ffn_down", "g_ple", "w_ple_gate", "w_ple_proj", "g_final")


def _sample_front_kernel(*refs, n_heads):
    x_ref, ck_ref, cv_ref, sconv_ref, sh_ref, bias_ref = refs[:6]
    nw = len(FRONT_NAMES)
    w = dict(zip(FRONT_NAMES, refs[6:6 + nw]))
    attn_ref, lru_ref, ko_ref, vo_ref, convo_ref, ho_ref = refs[6 + nw:12 + nw]
    (xr_buf,) = refs[12 + nw:]

    nb, t, d = x_ref.shape
    n_cache = ck_ref.shape[3]
    aw = n_heads * HEAD_DIM
    rows = nb * t

    x = x_ref[...].reshape(rows, d)
    q, k, v, xr, gr = _project(x, w)
    ko_ref[...] = k.reshape(nb, t, aw)
    vo_ref[...] = v.reshape(nb, t, aw)
    qb = q.astype(BF16)
    kb_new = k.astype(BF16)
    vb_new = v.astype(BF16)

    def scores(b, h):
        rs = slice(b * t, (b + 1) * t)
        cols = slice(h * HEAD_DIM, (h + 1) * HEAD_DIM)
        qh = qb[rs, cols]
        kc_t = ck_ref[b, h].astype(BF16)
        s_c = _dot(qh, kc_t) + bias_ref[h, 0:t, PAD - n_cache:PAD]
        s_n = _dot_nt(qh, kb_new[rs, cols]) + bias_ref[h, 0:t, PAD:PAD + t]
        return s_c, s_n

    s_cur = [scores(0, h) for h in range(n_heads)]
    for b in range(nb):
        s_next = [scores(b + 1, h) for h in range(n_heads)] if b + 1 < nb else None
        outs = []
        for h in range(n_heads):
            vc_t = cv_ref[b, h].astype(BF16)
            vn = vb_new[b * t:(b + 1) * t, h * HEAD_DIM:(h + 1) * HEAD_DIM]
            outs.append(_softmax_pv(list(s_cur[h]), [vc_t, vn], [True, False]))
        attn_ref[b] = jnp.concatenate(outs, axis=-1)
        s_cur = s_next

    seg_rows = SUBLANES + t
    xcs = []
    for b in range(nb):
        base = b * seg_rows
        xr_buf[base + SUBLANES - (CONV_WIDTH - 1):base + SUBLANES, :] = sconv_ref[b]
        xr_buf[base + SUBLANES:base + seg_rows, :] = xr[b * t:(b + 1) * t, :]
        xcs.append(_conv(xr_buf, base + SUBLANES, t, w))
        convo_ref[b] = xr_buf[base + seg_rows - (CONV_WIDTH - 1):base + seg_rows, :]
    xc = jnp.concatenate(xcs, axis=0)
    h0_rows = jnp.concatenate(
        [jnp.broadcast_to(sh_ref[b], (t, sh_ref.shape[2])) for b in range(nb)], axis=0)
    hs, lru_out = _rglru(xc, gr, h0_rows, t, w)
    for b in range(nb):
        ho_ref[b] = hs[(b + 1) * t - 1:(b + 1) * t, :]
    lru_ref[...] = lru_out.reshape(nb, t, lru_out.shape[1])


def _sample_back_kernel(*refs, final_norm):
    x_ref, attn_ref, lru_ref, p_ref = refs[:4]
    nw = len(BACK_NAMES)
    w = dict(zip(BACK_NAMES, refs[4:4 + nw]))
    y_ref = refs[4 + nw]
    y_ref[...] = _dense_tail(x_ref[...], attn_ref[...], lru_ref[...], p_ref[...], w, final_norm)


def _sample_layer(x, p, cache_k, cache_v, state_conv, state_h, bias, weights, final_norm, group,
                  back_rows):
    nbatch, t, d = x.shape
    n_heads = bias.shape[0]
    aw = n_heads * HEAD_DIM
    n_cache = cache_k.shape[3]
    lw = weights["g_lru_out"].shape[1]
    assert nbatch % group == 0 and t % SUBLANES == 0 and t & (t - 1) == 0
    assert CONV_WIDTH - 1 <= t <= CHUNK and n_cache <= PAD and cache_k.shape[1:3] == (n_heads, HEAD_DIM)
    assert (nbatch * t) % back_rows == 0
    state_h3 = state_h.reshape(nbatch, 1, lw)

    def grp(shape):
        nd = len(shape)
        return pl.BlockSpec((group,) + tuple(shape[1:]), lambda i: (i,) + (0,) * (nd - 1))

    front_w = [weights[n] for n in FRONT_NAMES]
    front_out = (
        jax.ShapeDtypeStruct((nbatch, t, aw), F32),
        jax.ShapeDtypeStruct((nbatch, t, lw), F32),
        jax.ShapeDtypeStruct((nbatch, t, aw), F32),
        jax.ShapeDtypeStruct((nbatch, t, aw), F32),
        jax.ShapeDtypeStruct((nbatch, CONV_WIDTH - 1, lw), F32),
        jax.ShapeDtypeStruct((nbatch, 1, lw), F32),
    )
    ins = [x, cache_k, cache_v, state_conv, state_h3]
    attn, lru_out, k_new, v_new, conv_new, h_new = pl.pallas_call(
        functools.partial(_sample_front_kernel, n_heads=n_heads),
        out_shape=front_out,
        grid=(nbatch // group,),
        in_specs=[grp(a.shape) for a in ins] + [_const_spec(bias)]
        + [_const_spec(a) for a in front_w],
        out_specs=tuple(grp(o.shape) for o in front_out),
        scratch_shapes=[pltpu.VMEM((group * (SUBLANES + t), lw), F32)],
        compiler_params=pltpu.CompilerParams(
            dimension_semantics=("arbitrary",), vmem_limit_bytes=VMEM_LIMIT),
        name="sample_front",
    )(*ins, bias, *front_w)

    rows = nbatch * t
    back_w = [weights[n] for n in BACK_NAMES]
    flat = [x.reshape(rows, d), attn.reshape(rows, aw), lru_out.reshape(rows, lw),
            p.reshape(rows, p.shape[-1])]
    y = pl.pallas_call(
        functools.partial(_sample_back_kernel, final_norm=final_norm),
        out_shape=jax.ShapeDtypeStruct((rows, d), F32),
        grid=(rows // back_rows,),
        in_specs=[pl.BlockSpec((back_rows, a.shape[1]), lambda i: (i, 0)) for a in flat]
        + [_const_spec(a) for a in back_w],
        out_specs=pl.BlockSpec((back_rows, d), lambda i: (i, 0)),
        compiler_params=pltpu.CompilerParams(
            dimension_semantics=("arbitrary",), vmem_limit_bytes=VMEM_LIMIT),
        name="sample_back",
    )(*flat, *back_w)
    return y.reshape(nbatch, t, d), k_new, v_new, conv_new, h_new


def _block_diag(wb):
    n, kk, jj = wb.shape
    eye = jnp.eye(n, dtype=wb.dtype)
    return (wb[:, :, None, :] * eye[:, None, :, None]).reshape(n * kk, n * jj)


PROMPT_TILE = 256
SAMPLE_GROUP = 8
SAMPLE_BACK_ROWS = 256


def kernel(x_prompt, x_sample, p_prompt, p_sample, cache_k, cache_v, state_conv, state_h, g_mix, w_in, conv_w, conv_b, w_rgate, b_rgate, w_igate, b_igate, lru_lambda, rel_bias_table, g_attn_out, g_lru_out, w_out, g_ffn, w_ffn_gate, w_ffn_up, w_ffn_down, g_ple, w_ple_gate, w_ple_proj, g_final):
    depth = w_in.shape[0]
    hp, hs = x_prompt, x_sample
    outs = [[] for _ in range(8)]
    for i in range(depth):
        row = lambda a: a[i].reshape(1, -1).astype(F32)
        weights = {
            "g_mix": row(g_mix), "w_in": w_in[i].astype(BF16),
            "conv_w": conv_w[i].astype(F32), "conv_b": row(conv_b),
            "w_r": _block_diag(w_rgate[i]).astype(BF16), "b_r": row(b_rgate),
            "w_i": _block_diag(w_igate[i]).astype(BF16), "b_i": row(b_igate),
            "lam": row(lru_lambda),
            "g_attn_out": row(g_attn_out), "g_lru_out": row(g_lru_out),
            "w_out": w_out[i].astype(BF16), "g_ffn": row(g_ffn),
            "w_ffn_gate": w_ffn_gate[i].astype(BF16), "w_ffn_up": w_ffn_up[i].astype(BF16),
            "w_ffn_down": w_ffn_down[i].astype(BF16), "g_ple": row(g_ple),
            "w_ple_gate": w_ple_gate[i].astype(BF16), "w_ple_proj": w_ple_proj[i].astype(BF16),
            "g_final": g_final.reshape(1, -1).astype(F32),
        }
        final = i == depth - 1
        bias, bias2 = _bias_tables(rel_bias_table[i])
        n_heads = rel_bias_table.shape[1]
        aw = n_heads * HEAD_DIM

        hp, k1, v1, c1, r1 = _prompt_layer(
            hp, p_prompt[i], bias2, weights, final, PROMPT_TILE)
        nb, nc = cache_k.shape[1:3]
        hs, k2, v2, c2, r2 = _sample_layer(
            hs, p_sample[i], jnp.transpose(cache_k[i], (0, 2, 3, 1)),
            jnp.transpose(cache_v[i], (0, 2, 3, 1)), state_conv[i], state_h[i], bias,
            weights, final, SAMPLE_GROUP, SAMPLE_BACK_ROWS)
        bp, keep = k1.shape[:2]
        ts = k2.shape[1]
        for lst, val in zip(outs, (
                k1.reshape(bp, keep, n_heads, HEAD_DIM), v1.reshape(bp, keep, n_heads, HEAD_DIM),
                c1, r1.reshape(bp, -1),
                k2.reshape(nb, ts, n_heads, HEAD_DIM), v2.reshape(nb, ts, n_heads, HEAD_DIM),
                c2, r2.reshape(nb, -1))):
            lst.append(val)
    return (hp, hs) + tuple(jnp.stack(l) for l in outs)
```

```python
import functools

import jax
import jax.numpy as jnp
from jax import lax
from jax.experimental import pallas as pl
from jax.experimental.pallas import tpu as pltpu

CHUNK = 64
LEFT_CHUNKS = 8
PAD = LEFT_CHUNKS * CHUNK
BAND = PAD + CHUNK
HEAD_DIM = 64
CONV_WIDTH = 4
LRU_C = 8.0
REL_CLIP = 128
EPS = 1e-6
NEG = -1e30
SCALE = HEAD_DIM ** -0.5
LOG2E = 1.4426950408889634
LANES = 128
SUBLANES = 8
MXU_DIM = 256
EXT = 640
WINDOW = BAND + CHUNK
VMEM_LIMIT = 60 * 1024 * 1024
FFN_CHUNK = 4 * MXU_DIM

F32 = jnp.float32
BF16 = jnp.bfloat16


def _dot(a, b):
    return jnp.dot(a, b, preferred_element_type=F32)


def _dot_nt(a, b):
    return lax.dot_general(a, b, (((1,), (1,)), ((), ())), preferred_element_type=F32)


def _rms(x, g):
    ms = jnp.mean(x * x, axis=-1, keepdims=True)
    return (x * lax.rsqrt(ms + EPS)) * g


def _sigmoid(x):
    return 1.0 / (1.0 + jnp.exp2(x * (-LOG2E)))


def _gelu_tanh(x):
    c = -0.7978845608028654 * 2.0 * LOG2E
    return x / (1.0 + jnp.exp2(x * (c + (c * 0.044715) * (x * x))))


def _shift_rows(x, d, fill, seg):
    rows = x.shape[0]
    if seg == rows and d % SUBLANES == 0:
        return jnp.concatenate([jnp.full((d, x.shape[1]), fill, x.dtype), x[:rows - d]], axis=0)
    rolled = pltpu.roll(x, d, axis=0)
    row = lax.broadcasted_iota(jnp.int32, x.shape, 0)
    return jnp.where((row & (seg - 1)) >= d, rolled, fill)


def _scan_distances(seg):
    return [1 << b for b in range(seg.bit_length() - 1)]


def _scan_step(a, u, d, seg):
    u = u + a * _shift_rows(u, d, 0.0, seg)
    if 2 * d < seg:
        a = a * _shift_rows(a, d, 1.0, seg)
    return a, u


def _sqrt_nonneg(y):
    return jnp.where(y > 0.0, y * lax.rsqrt(y), 0.0)


def _lru_inputs(xc, r_pre, i_pre, w):
    r = _sigmoid(r_pre + w["b_r"][...])
    ig = _sigmoid(i_pre + w["b_i"][...])
    z = -w["lam"][...]
    softplus = jnp.maximum(z, 0.0) + jnp.log1p(jnp.exp(-jnp.abs(z)))
    a = jnp.exp2(r * ((-LRU_C * LOG2E) * softplus))
    u = _sqrt_nonneg(1.0 - a * a) * (ig * xc)
    return a, u


def _gate_dot(xcb, w_ref):
    n = w_ref.shape[0]
    if n % MXU_DIM:
        return _dot(xcb, w_ref[...])
    return jnp.concatenate(
        [_dot(xcb[:, c:c + MXU_DIM], w_ref[c:c + MXU_DIM, c:c + MXU_DIM])
         for c in range(0, n, MXU_DIM)], axis=1)


def _rglru(xc, gr, h0_rows, seg, w):
    xcb = xc.astype(BF16)
    a, u = _lru_inputs(xc, _gate_dot(xcb, w["w_r"]), _gate_dot(xcb, w["w_i"]), w)
    row = lax.broadcasted_iota(jnp.int32, xc.shape, 0)
    u = u + jnp.where((row & (seg - 1)) == 0, a * h0_rows, 0.0)
    for d in _scan_distances(seg):
        a, u = _scan_step(a, u, d, seg)
    return u, u * _gelu_tanh(gr)


def _scan_blocked(a, u, h0):
    shape = a.shape
    grouped = (shape[0] // SUBLANES, SUBLANES, shape[1])
    a = a.reshape(grouped)
    u = u.reshape(grouped)
    row = lax.broadcasted_iota(jnp.int32, grouped, 1)
    for d in _scan_distances(SUBLANES):
        inside = row >= d
        u = u + jnp.where(inside, a, 0.0) * pltpu.roll(u, d, axis=1)
        a = a * jnp.where(inside, pltpu.roll(a, d, axis=1), 1.0)
    a = a.reshape(shape)
    u = u.reshape(shape)
    carry = jnp.broadcast_to(h0, (SUBLANES, a.shape[1]))
    groups = []
    for g in range(a.shape[0] // SUBLANES):
        rows = slice(g * SUBLANES, (g + 1) * SUBLANES)
        h = u[rows] + a[rows] * carry
        groups.append(h)
        carry = jnp.broadcast_to(h[SUBLANES - 1:SUBLANES], h.shape)
    return jnp.concatenate(groups, axis=0)


def _mix_out(h, attn, lru_out, w):
    mixed = jnp.concatenate(
        [_rms(attn, w["g_attn_out"][...]), _rms(lru_out, w["g_lru_out"][...])], axis=-1)
    return h + _dot(mixed.astype(BF16), w["w_out"][...])


def _ffn_piece(hn, acc, w, c0, c1):
    gate = _dot(hn, w["w_ffn_gate"][:, c0:c1])
    up = _dot(hn, w["w_ffn_up"][:, c0:c1])
    act = (gate * _sigmoid(gate)) * up
    return acc + _dot(act.astype(BF16), w["w_ffn_down"][c0:c1, :])


def _ffn_bounds(d_ff, step):
    return [(c, min(c + step, d_ff)) for c in range(0, d_ff, step)]


def _ple_out(h, p, w, final_norm):
    gate = _sigmoid(_dot(_rms(h, w["g_ple"][...]).astype(BF16), w["w_ple_gate"][...]))
    h = h + _dot(p.astype(BF16), w["w_ple_proj"][...]) * gate
    if final_norm:
        h = _rms(h, w["g_final"][...])
    return h


def _dense_tail(h, attn, lru_out, p, w, final_norm):
    h = _mix_out(h, attn, lru_out, w)
    hn = _rms(h, w["g_ffn"][...]).astype(BF16)
    d_ff = w["w_ffn_gate"].shape[1]
    for c0, c1 in _ffn_bounds(d_ff, -(-d_ff // (2 * LANES)) * LANES):
        h = _ffn_piece(hn, h, w, c0, c1)
    return _ple_out(h, p, w, final_norm)


def _softmax_pv(scores, values, transposed=None):
    transposed = transposed or [False] * len(values)
    m = functools.reduce(jnp.maximum, [jnp.max(s, axis=-1, keepdims=True) for s in scores])
    num = None
    den = None
    for s, v, v_t in zip(scores, values, transposed):
        e = jnp.exp2(s - m)
        l = jnp.sum(e, axis=-1, keepdims=True)
        o = _dot_nt(e.astype(BF16), v) if v_t else _dot(e.astype(BF16), v)
        num = o if num is None else num + o
        den = l if den is None else den + l
    return num / den


WEIGHT_NAMES = (
    "g_mix", "w_in", "conv_w", "conv_b", "w_r", "b_r", "w_i", "b_i", "lam",
    "g_attn_out", "g_lru_out", "w_out", "g_ffn", "w_ffn_gate", "w_ffn_up", "w_ffn_down",
    "g_ple", "w_ple_gate", "w_ple_proj", "g_final")


def _order_after(x, *deps):
    bits = None
    for dep in deps:
        b = pltpu.bitcast(dep[:SUBLANES, :LANES], jnp.uint32)
        bits = b if bits is None else bits | b
    zero = pltpu.bitcast((bits >> 16) >> 16, F32)
    head = x[:SUBLANES] + jnp.concatenate([zero] * (x.shape[1] // LANES), axis=1)
    return jnp.concatenate([head, x[SUBLANES:]], axis=0)


def _project(x, w):
    xn = _rms(x, w["g_mix"][...]).astype(BF16)
    proj = _dot(xn, w["w_in"][...])
    aw = w["g_attn_out"].shape[1]
    lw = w["g_lru_out"].shape[1]
    q = proj[:, :aw] * (SCALE * LOG2E)
    k = proj[:, aw:2 * aw]
    v = proj[:, 2 * aw:3 * aw]
    xr = proj[:, 3 * aw:3 * aw + lw]
    gr = proj[:, 3 * aw + lw:]
    return q, k, v, xr, gr


def _conv(xr_buf, base, rows, w):
    cw = w["conv_w"]
    ext = xr_buf[base - SUBLANES:base + rows, :]
    xc = w["conv_b"][...] + ext[SUBLANES:] * cw[CONV_WIDTH - 1:CONV_WIDTH, :]
    for back in range(1, CONV_WIDTH):
        j = CONV_WIDTH - 1 - back
        xc = xc + pltpu.roll(ext, back, axis=0)[SUBLANES:] * cw[j:j + 1, :]
    return xc


def _bias_kernel(ext_ref, bias_ref, bias2_ref):
    n_heads = ext_ref.shape[0]
    col = lax.broadcasted_iota(jnp.int32, (CHUNK, WINDOW), 1)
    for h in range(n_heads):
        rows = jnp.broadcast_to(ext_ref[h], (CHUNK, EXT))
        rolled = pltpu.roll(rows, EXT - (CHUNK - 1), axis=1, stride=1, stride_axis=0) * LOG2E
        bias_ref[h] = rolled[:, :BAND]
        for e in range(2):
            shifted = rolled if e == 0 else pltpu.roll(rolled, CHUNK, axis=1)
            in_band = jnp.logical_and(col >= e * CHUNK, col < e * CHUNK + BAND)
            r0 = e * 2 * CHUNK + (h % 2) * CHUNK
            bias2_ref[h // 2, r0:r0 + CHUNK, :] = jnp.where(in_band, shifted, NEG)


def _bias_tables(rel_table):
    n_heads, rel_size = rel_table.shape
    assert rel_size == REL_CLIP + CHUNK and EXT == WINDOW and n_heads % 2 == 0
    left = (BAND - 1) - REL_CLIP
    ext = jnp.pad(rel_table.astype(F32), ((0, 0), (left, EXT - left - rel_size)), mode="edge")
    return pl.pallas_call(
        _bias_kernel,
        out_shape=(jax.ShapeDtypeStruct((n_heads, CHUNK, BAND), F32),
                   jax.ShapeDtypeStruct((n_heads // 2, 4 * CHUNK, WINDOW), F32)),
        name="rel_bias_expand",
    )(ext.reshape(n_heads, 1, EXT))


def _prompt_kernel(*refs, tile, seq, final_norm):
    x_ref, xprev_ref, pprev_ref, bias_ref = refs[:4]
    nw = len(WEIGHT_NAMES)
    w = dict(zip(WEIGHT_NAMES, refs[4:4 + nw]))
    y_ref, ko_ref, vo_ref, convo_ref, ho_ref = refs[4 + nw:9 + nw]
    kbuf, vbuf, q_scr, attn_scr, gr_scr, xr_buf, h_scr = refs[9 + nw:]

    t = pl.program_id(1)
    n_tiles = seq // tile
    n_pairs = kbuf.shape[1] // LANES
    keep_from = seq - PAD
    d_model = x_ref.shape[2]
    d_ff = w["w_ffn_gate"].shape[1]
    rows_w = tile // 16

    lane = lax.broadcasted_iota(jnp.int32, (CHUNK, LANES), 1)
    low_half = lane < HEAD_DIM
    units = [(jj, pr) for jj in range(tile // (2 * CHUNK)) for pr in range(n_pairs)]
    band_row = lax.broadcasted_iota(jnp.int32, (4 * CHUNK, WINDOW), 0)
    band_col = lax.broadcasted_iota(jnp.int32, (4 * CHUNK, WINDOW), 1)
    band_lo = jnp.where(band_row >= 2 * CHUNK, CHUNK, 0)
    masked = jnp.where(band_col >= band_lo, NEG, 2 * NEG)
    masked = jnp.where(band_col < band_lo + BAND, masked, 2 * NEG)
    kidx = lax.broadcasted_iota(jnp.int32, (1, WINDOW), 1)

    def step(front, back):
        st = {}

        if back:
            xc = _conv(xr_buf, SUBLANES, tile, w)
            xcb = xc.astype(BF16)
            r_pre = _gate_dot(xcb, w["w_r"])
            i_pre = _gate_dot(xcb, w["w_i"])
            h0 = h_scr[...]
            gr_prev = gr_scr[...]
            xr_tail = xr_buf[tile:tile + SUBLANES, :]

        if front:
            x = x_ref[0]
            if back:
                x = _order_after(x, r_pre, i_pre)
            q, k, v, xr, gr = _project(x, w)

        if back:
            a, u = _lru_inputs(xc, r_pre, i_pre, w)
            u = _scan_blocked(a, u, h0)
            lru_out = u * _gelu_tanh(gr_prev)
            h_scr[...] = u[tile - 1:tile, :]
            xr_buf[0:SUBLANES, :] = xr_tail

        if front:
            q_scr[...] = q.astype(BF16)
            kbuf[PAD:PAD + tile, :] = k.astype(BF16)
            vbuf[PAD:PAD + tile, :] = v.astype(BF16)
            xr_buf[SUBLANES:SUBLANES + tile, :] = xr
            gr_scr[...] = gr

            @pl.when(t * tile >= keep_from)
            def _():
                off = pl.multiple_of(t * tile - keep_from, tile)
                ko_ref[0, pl.ds(off, tile), :] = k
                vo_ref[0, pl.ds(off, tile), :] = v

        if not front:
            convo_ref[0] = xr_buf[SUBLANES - (CONV_WIDTH - 1):SUBLANES, :]
            ho_ref[0] = h_scr[...]

        dense = []

        def ffn_norm():
            st["hn"] = _rms(st["h"], w["g_ffn"][...]).astype(BF16)

        def ffn_gate(c0, c1):
            st["g"] = _dot(st["hn"], w["w_ffn_gate"][:, c0:c1])

        def ffn_up(c0, c1):
            g = st.pop("g")
            st["act"] = ((g * _sigmoid(g)) * _dot(st["hn"], w["w_ffn_up"][:, c0:c1])).astype(BF16)

        def ffn_down(c0, c1):
            st["h"] = st["h"] + _dot(st.pop("act"), w["w_ffn_down"][c0:c1, :])

        def ple_gate():
            hpn = _rms(st["h"], w["g_ple"][...]).astype(BF16)
            st["pg"] = _sigmoid(_dot(hpn, w["w_ple_gate"][...]))

        def ple_out():
            h = st["h"] + _dot(pprev_ref[0].astype(BF16), w["w_ple_proj"][...]) * st.pop("pg")
            if final_norm:
                h = _rms(h, w["g_final"][...])
            y_ref[0] = h

        if back:
            st["h"] = _mix_out(xprev_ref[0], attn_scr[...], lru_out, w)
            dense += [(0, ffn_norm)]
            for c0, c1 in _ffn_bounds(d_ff, FFN_CHUNK):
                wt = rows_w * (d_model // MXU_DIM) * -(-(c1 - c0) // MXU_DIM)
                dense += [(wt, functools.partial(ffn_gate, c0, c1)),
                          (wt, functools.partial(ffn_up, c0, c1)),
                          (wt, functools.partial(ffn_down, c0, c1))]
            dense += [(rows_w * (d_model // MXU_DIM) ** 2, ple_gate)]
            dense += [(rows_w * (d_model // MXU_DIM), ple_out)]

        if not front:
            for _, piece in dense:
                piece()
            return

        def scores(jj, pr):
            lanes = slice(pr * LANES, (pr + 1) * LANES)
            r0 = jj * 2 * CHUNK
            parts = []
            for c in range(2):
                q2 = q_scr[r0 + c * CHUNK:r0 + (c + 1) * CHUNK, lanes]
                zero = jnp.zeros_like(q2)
                parts += [jnp.where(low_half, q2, zero), jnp.where(low_half, zero, q2)]
            s = _dot_nt(jnp.concatenate(parts, axis=0), kbuf[r0:r0 + WINDOW, lanes]) + bias_ref[pr]
            valid = (t * tile + r0 + kidx - PAD) >= PAD
            return jnp.where(valid, s, masked)

        def attend(jj, pr, s):
            lanes = slice(pr * LANES, (pr + 1) * LANES)
            r0 = jj * 2 * CHUNK
            o = _softmax_pv([s], [vbuf[r0:r0 + WINDOW, lanes]])
            for c in range(2):
                st["attn", 2 * jj + c, pr] = jnp.where(
                    low_half, o[2 * c * CHUNK:(2 * c + 1) * CHUNK],
                    o[(2 * c + 1) * CHUNK:(2 * c + 2) * CHUNK])

        total_w = sum(wt for wt, _ in dense)
        done_w = 0
        pending = list(dense)
        s_cur = scores(*units[0])
        for i, unit in enumerate(units):
            s_next = scores(*units[i + 1]) if i + 1 < len(units) else None
            share = total_w * (i + 1) // len(units)
            while pending and (done_w < share or i + 1 == len(units)):
                wt, piece = pending.pop(0)
                piece()
                done_w += wt
            attend(*unit, s_cur)
            s_cur = s_next

        for j in range(tile // CHUNK):
            for pr in range(n_pairs):
                attn_scr[j * CHUNK:(j + 1) * CHUNK, pr * LANES:(pr + 1) * LANES] = st.pop(
                    ("attn", j, pr))

        for r in range(0, PAD, tile):
            n = min(tile, PAD - r)
            kbuf[r:r + n, :] = kbuf[r + tile:r + tile + n, :]
            vbuf[r:r + n, :] = vbuf[r + tile:r + tile + n, :]

    @pl.when(t == 0)
    def _():
        kbuf[0:PAD, :] = jnp.zeros((PAD, kbuf.shape[1]), BF16)
        vbuf[0:PAD, :] = jnp.zeros((PAD, vbuf.shape[1]), BF16)
        xr_buf[0:SUBLANES, :] = jnp.zeros((SUBLANES, xr_buf.shape[1]), F32)
        h_scr[...] = jnp.zeros(h_scr.shape, F32)
        step(front=True, back=False)

    @pl.when(jnp.logical_and(t > 0, t < n_tiles))
    def _():
        step(front=True, back=True)

    @pl.when(t == n_tiles)
    def _():
        step(front=False, back=True)


def _const_spec(arr):
    nd = arr.ndim
    return pl.BlockSpec(arr.shape, lambda *_: (0,) * nd, pipeline_mode=pl.Buffered(1))


def _prompt_layer(x, p, bias, weights, final_norm, tile):
    b, seq, d = x.shape
    aw = weights["g_attn_out"].shape[1]
    lw = weights["g_lru_out"].shape[1]
    assert seq % tile == 0 and PAD % tile == 0 and seq >= PAD and tile % CHUNK == 0
    assert tile & (tile - 1) == 0 and aw % LANES == 0 and tile % (2 * CHUNK) == 0
    n_tiles = seq // tile
    wlist = [weights[n] for n in WEIGHT_NAMES]
    kern = functools.partial(_prompt_kernel, tile=tile, seq=seq, final_norm=final_norm)
    out_shape = (
        jax.ShapeDtypeStruct((b, seq, d), F32),
        jax.ShapeDtypeStruct((b, PAD, aw), F32),
        jax.ShapeDtypeStruct((b, PAD, aw), F32),
        jax.ShapeDtypeStruct((b, CONV_WIDTH - 1, lw), F32),
        jax.ShapeDtypeStruct((b, 1, lw), F32),
    )
    cur = lambda i, j: (i, jnp.minimum(j, n_tiles - 1), 0)
    prev = lambda i, j: (i, jnp.maximum(j - 1, 0), 0)
    in_specs = [
        pl.BlockSpec((1, tile, d), cur),
        pl.BlockSpec((1, tile, d), prev),
        pl.BlockSpec((1, tile, p.shape[-1]), prev),
        _const_spec(bias),
    ] + [_const_spec(a) for a in wlist]
    out_specs = (
        pl.BlockSpec((1, tile, d), prev),
        pl.BlockSpec((1, PAD, aw), lambda i, j: (i, 0, 0)),
        pl.BlockSpec((1, PAD, aw), lambda i, j: (i, 0, 0)),
        pl.BlockSpec((1, CONV_WIDTH - 1, lw), lambda i, j: (i, 0, 0)),
        pl.BlockSpec((1, 1, lw), lambda i, j: (i, 0, 0)),
    )
    scratch = [
        pltpu.VMEM((PAD + tile, aw), BF16),
        pltpu.VMEM((PAD + tile, aw), BF16),
        pltpu.VMEM((tile, aw), BF16),
        pltpu.VMEM((tile, aw), F32),
        pltpu.VMEM((tile, lw), F32),
        pltpu.VMEM((SUBLANES + tile, lw), F32),
        pltpu.VMEM((1, lw), F32),
    ]
    return pl.pallas_call(
        kern,
        out_shape=out_shape,
        grid=(b, n_tiles + 1),
        in_specs=in_specs,
        out_specs=out_specs,
        scratch_shapes=scratch,
        compiler_params=pltpu.CompilerParams(
            dimension_semantics=("arbitrary", "arbitrary"), vmem_limit_bytes=VMEM_LIMIT),
        name="prompt_layer",
    )(x, x, p, bias, *wlist)


FRONT_NAMES = ("g_mix", "w_in", "conv_w", "conv_b", "w_r", "b_r", "w_i", "b_i", "lam",
               "g_attn_out", "g_lru_out")
BACK_NAMES = ("g_attn_out", "g_lru_out", "w_out", "g_ffn", "w_ffn_gate", "w_ffn_up",
              "w_ffn_down", "g_ple", "w_ple_gate", "w_ple_proj", "g_final")


def _sample_front_kernel(*refs, n_heads):
    x_ref, ck_ref, cv_ref, sconv_ref, sh_ref, bias_ref = refs[:6]
    nw = len(FRONT_NAMES)
    w = dict(zip(FRONT_NAMES, refs[6:6 + nw]))
    attn_ref, lru_ref, ko_ref, vo_ref, convo_ref, ho_ref = refs[6 + nw:12 + nw]
    (xr_buf,) = refs[12 + nw:]

    nb, t, d = x_ref.shape
    n_cache = ck_ref.shape[3]
    aw = n_heads * HEAD_DIM
    rows = nb * t

    x = x_ref[...].reshape(rows, d)
    q, k, v, xr, gr = _project(x, w)
    ko_ref[...] = k.reshape(nb, t, aw)
    vo_ref[...] = v.reshape(nb, t, aw)
    qb = q.astype(BF16)
    kb_new = k.astype(BF16)
    vb_new = v.astype(BF16)

    def scores(b, h):
        rs = slice(b * t, (b + 1) * t)
        cols = slice(h * HEAD_DIM, (h + 1) * HEAD_DIM)
        qh = qb[rs, cols]
        kc_t = ck_ref[b, h].astype(BF16)
        s_c = _dot(qh, kc_t) + bias_ref[h, 0:t, PAD - n_cache:PAD]
        s_n = _dot_nt(qh, kb_new[rs, cols]) + bias_ref[h, 0:t, PAD:PAD + t]
        return s_c, s_n

    s_cur = [scores(0, h) for h in range(n_heads)]
    for b in range(nb):
        s_next = [scores(b + 1, h) for h in range(n_heads)] if b + 1 < nb else None
        outs = []
        for h in range(n_heads):
            vc_t = cv_ref[b, h].astype(BF16)
            vn = vb_new[b * t:(b + 1) * t, h * HEAD_DIM:(h + 1) * HEAD_DIM]
            outs.append(_softmax_pv(list(s_cur[h]), [vc_t, vn], [True, False]))
        attn_ref[b] = jnp.concatenate(outs, axis=-1)
        s_cur = s_next

    seg_rows = SUBLANES + t
    xcs = []
    for b in range(nb):
        base = b * seg_rows
        xr_buf[base + SUBLANES - (CONV_WIDTH - 1):base + SUBLANES, :] = sconv_ref[b]
        xr_buf[base + SUBLANES:base + seg_rows, :] = xr[b * t:(b + 1) * t, :]
        xcs.append(_conv(xr_buf, base + SUBLANES, t, w))
        convo_ref[b] = xr_buf[base + seg_rows - (CONV_WIDTH - 1):base + seg_rows, :]
    xc = jnp.concatenate(xcs, axis=0)
    h0_rows = jnp.concatenate(
        [jnp.broadcast_to(sh_ref[b], (t, sh_ref.shape[2])) for b in range(nb)], axis=0)
    hs, lru_out = _rglru(xc, gr, h0_rows, t, w)
    for b in range(nb):
        ho_ref[b] = hs[(b + 1) * t - 1:(b + 1) * t, :]
    lru_ref[...] = lru_out.reshape(nb, t, lru_out.shape[1])


def _sample_back_kernel(*refs, final_norm):
    x_ref, attn_ref, lru_ref, p_ref = refs[:4]
    nw = len(BACK_NAMES)
    w = dict(zip(BACK_NAMES, refs[4:4 + nw]))
    y_ref = refs[4 + nw]
    y_ref[...] = _dense_tail(x_ref[...], attn_ref[...], lru_ref[...], p_ref[...], w, final_norm)


def _sample_layer(x, p, cache_k, cache_v, state_conv, state_h, bias, weights, final_norm, group,
                  back_rows):
    nbatch, t, d = x.shape
    n_heads = bias.shape[0]
    aw = n_heads * HEAD_DIM
    n_cache = cache_k.shape[3]
    lw = weights["g_lru_out"].shape[1]
    assert nbatch % group == 0 and t % SUBLANES == 0 and t & (t - 1) == 0
    assert CONV_WIDTH - 1 <= t <= CHUNK and n_cache <= PAD and cache_k.shape[1:3] == (n_heads, HEAD_DIM)
    assert (nbatch * t) % back_rows == 0
    state_h3 = state_h.reshape(nbatch, 1, lw)

    def grp(shape):
        nd = len(shape)
        return pl.BlockSpec((group,) + tuple(shape[1:]), lambda i: (i,) + (0,) * (nd - 1))

    front_w = [weights[n] for n in FRONT_NAMES]
    front_out = (
        jax.ShapeDtypeStruct((nbatch, t, aw), F32),
        jax.ShapeDtypeStruct((nbatch, t, lw), F32),
        jax.ShapeDtypeStruct((nbatch, t, aw), F32),
        jax.ShapeDtypeStruct((nbatch, t, aw), F32),
        jax.ShapeDtypeStruct((nbatch, CONV_WIDTH - 1, lw), F32),
        jax.ShapeDtypeStruct((nbatch, 1, lw), F32),
    )
    ins = [x, cache_k, cache_v, state_conv, state_h3]
    attn, lru_out, k_new, v_new, conv_new, h_new = pl.pallas_call(
        functools.partial(_sample_front_kernel, n_heads=n_heads),
        out_shape=front_out,
        grid=(nbatch // group,),
        in_specs=[grp(a.shape) for a in ins] + [_const_spec(bias)]
        + [_const_spec(a) for a in front_w],
        out_specs=tuple(grp(o.shape) for o in front_out),
        scratch_shapes=[pltpu.VMEM((group * (SUBLANES + t), lw), F32)],
        compiler_params=pltpu.CompilerParams(
            dimension_semantics=("arbitrary",), vmem_limit_bytes=VMEM_LIMIT),
        name="sample_front",
    )(*ins, bias, *front_w)

    rows = nbatch * t
    back_w = [weights[n] for n in BACK_NAMES]
    flat = [x.reshape(rows, d), attn.reshape(rows, aw), lru_out.reshape(rows, lw),
            p.reshape(rows, p.shape[-1])]
    y = pl.pallas_call(
        functools.partial(_sample_back_kernel, final_norm=final_norm),
        out_shape=jax.ShapeDtypeStruct((rows, d), F32),
        grid=(rows // back_rows,),
        in_specs=[pl.BlockSpec((back_rows, a.shape[1]), lambda i: (i, 0)) for a in flat]
        + [_const_spec(a) for a in back_w],
        out_specs=pl.BlockSpec((back_rows, d), lambda i: (i, 0)),
        compiler_params=pltpu.CompilerParams(
            dimension_semantics=("arbitrary",), vmem_limit_bytes=VMEM_LIMIT),
        name="sample_back",
    )(*flat, *back_w)
    return y.reshape(nbatch, t, d), k_new, v_new, conv_new, h_new


def _block_diag(wb):
    n, kk, jj = wb.shape
    eye = jnp.eye(n, dtype=wb.dtype)
    return (wb[:, :, None, :] * eye[:, None, :, None]).reshape(n * kk, n * jj)


PROMPT_TILE = 256
SAMPLE_GROUP = 8
SAMPLE_BACK_ROWS = 256


def kernel(x_prompt, x_sample, p_prompt, p_sample, cache_k, cache_v, state_conv, state_h, g_mix, w_in, conv_w, conv_b, w_rgate, b_rgate, w_igate, b_igate, lru_lambda, rel_bias_table, g_attn_out, g_lru_out, w_out, g_ffn, w_ffn_gate, w_ffn_up, w_ffn_down, g_ple, w_ple_gate, w_ple_proj, g_final):
    depth = w_in.shape[0]
    hp, hs = x_prompt, x_sample
    outs = [[] for _ in range(8)]
    for i in range(depth):
        row = lambda a: a[i].reshape(1, -1).astype(F32)
        weights = {
            "g_mix": row(g_mix), "w_in": w_in[i].astype(BF16),
            "conv_w": conv_w[i].astype(F32), "conv_b": row(conv_b),
            "w_r": _block_diag(w_rgate[i]).astype(BF16), "b_r": row(b_rgate),
            "w_i": _block_diag(w_igate[i]).astype(BF16), "b_i": row(b_igate),
            "lam": row(lru_lambda),
            "g_attn_out": row(g_attn_out), "g_lru_out": row(g_lru_out),
            "w_out": w_out[i].astype(BF16), "g_ffn": row(g_ffn),
            "w_ffn_gate": w_ffn_gate[i].astype(BF16), "w_ffn_up": w_ffn_up[i].astype(BF16),
            "w_ffn_down": w_ffn_down[i].astype(BF16), "g_ple": row(g_ple),
            "w_ple_gate": w_ple_gate[i].astype(BF16), "w_ple_proj": w_ple_proj[i].astype(BF16),
            "g_final": g_final.reshape(1, -1).astype(F32),
        }
        final = i == depth - 1
        bias, bias2 = _bias_tables(rel_bias_table[i])
        n_heads = rel_bias_table.shape[1]
        aw = n_heads * HEAD_DIM

        hp, k1, v1, c1, r1 = _prompt_layer(
            hp, p_prompt[i], bias2, weights, final, PROMPT_TILE)
        nb, nc = cache_k.shape[1:3]
        hs, k2, v2, c2, r2 = _sample_layer(
            hs, p_sample[i], jnp.transpose(cache_k[i], (0, 2, 3, 1)),
            jnp.transpose(cache_v[i], (0, 2, 3, 1)), state_conv[i], state_h[i], bias,
            weights, final, SAMPLE_GROUP, SAMPLE_BACK_ROWS)
        bp, keep = k1.shape[:2]
        ts = k2.shape[1]
        for lst, val in zip(outs, (
                k1.reshape(bp, keep, n_heads, HEAD_DIM), v1.reshape(bp, keep, n_heads, HEAD_DIM),
                c1, r1.reshape(bp, -1),
                k2.reshape(nb, ts, n_heads, HEAD_DIM), v2.reshape(nb, ts, n_heads, HEAD_DIM),
                c2, r2.reshape(nb, -1))):
            lst.append(val)
    return (hp, hs) + tuple(jnp.stack(l) for l in outs)
```

```python
import functools

import jax
import jax.numpy as jnp
from jax import lax
from jax.experimental import pallas as pl
from jax.experimental.pallas import tpu as pltpu

CHUNK = 64
LEFT_CHUNKS = 8
PAD = LEFT_CHUNKS * CHUNK
BAND = PAD + CHUNK
HEAD_DIM = 64
CONV_WIDTH = 4
LRU_C = 8.0
REL_CLIP = 128
EPS = 1e-6
NEG = -1e30
SCALE = HEAD_DIM ** -0.5
LOG2E = 1.4426950408889634
LANES = 128
SUBLANES = 8
MXU_DIM = 256
EXT = 640
WINDOW = BAND + CHUNK
VMEM_LIMIT = 60 * 1024 * 1024
FFN_CHUNK = 4 * MXU_DIM

F32 = jnp.float32
BF16 = jnp.bfloat16


def _dot(a, b):
    return jnp.dot(a, b, preferred_element_type=F32)


def _dot_nt(a, b):
    return lax.dot_general(a, b, (((1,), (1,)), ((), ())), preferred_element_type=F32)


def _rms(x, g):
    ms = jnp.mean(x * x, axis=-1, keepdims=True)
    return (x * lax.rsqrt(ms + EPS)) * g


def _sigmoid(x):
    return 1.0 / (1.0 + jnp.exp2(x * (-LOG2E)))


def _gelu_tanh(x):
    c = -0.7978845608028654 * 2.0 * LOG2E
    return x / (1.0 + jnp.exp2(x * (c + (c * 0.044715) * (x * x))))


def _shift_rows(x, d, fill, seg):
    rows = x.shape[0]
    if seg == rows and d % SUBLANES == 0:
        return jnp.concatenate([jnp.full((d, x.shape[1]), fill, x.dtype), x[:rows - d]], axis=0)
    rolled = pltpu.roll(x, d, axis=0)
    row = lax.broadcasted_iota(jnp.int32, x.shape, 0)
    return jnp.where((row & (seg - 1)) >= d, rolled, fill)


def _scan_distances(seg):
    return [1 << b for b in range(seg.bit_length() - 1)]


def _scan_step(a, u, d, seg):
    u = u + a * _shift_rows(u, d, 0.0, seg)
    if 2 * d < seg:
        a = a * _shift_rows(a, d, 1.0, seg)
    return a, u


def _sqrt_nonneg(y):
    return jnp.where(y > 0.0, y * lax.rsqrt(y), 0.0)


def _lru_inputs(xc, r_pre, i_pre, w):
    r = _sigmoid(r_pre + w["b_r"][...])
    ig = _sigmoid(i_pre + w["b_i"][...])
    z = -w["lam"][...]
    softplus = jnp.maximum(z, 0.0) + jnp.log1p(jnp.exp(-jnp.abs(z)))
    a = jnp.exp2(r * ((-LRU_C * LOG2E) * softplus))
    u = _sqrt_nonneg(1.0 - a * a) * (ig * xc)
    return a, u


def _gate_dot(xcb, w_ref):
    n = w_ref.shape[0]
    if n % MXU_DIM:
        return _dot(xcb, w_ref[...])
    return jnp.concatenate(
        [_dot(xcb[:, c:c + MXU_DIM], w_ref[c:c + MXU_DIM, c:c + MXU_DIM])
         for c in range(0, n, MXU_DIM)], axis=1)


def _rglru(xc, gr, h0_rows, seg, w):
    xcb = xc.astype(BF16)
    a, u = _lru_inputs(xc, _gate_dot(xcb, w["w_r"]), _gate_dot(xcb, w["w_i"]), w)
    row = lax.broadcasted_iota(jnp.int32, xc.shape, 0)
    u = u + jnp.where((row & (seg - 1)) == 0, a * h0_rows, 0.0)
    for d in _scan_distances(seg):
        a, u = _scan_step(a, u, d, seg)
    return u, u * _gelu_tanh(gr)


def _scan_blocked(a, u, h0):
    shape = a.shape
    grouped = (shape[0] // SUBLANES, SUBLANES, shape[1])
    a = a.reshape(grouped)
    u = u.reshape(grouped)
    row = lax.broadcasted_iota(jnp.int32, grouped, 1)
    for d in _scan_distances(SUBLANES):
        inside = row >= d
        u = u + jnp.where(inside, a, 0.0) * pltpu.roll(u, d, axis=1)
        a = a * jnp.where(inside, pltpu.roll(a, d, axis=1), 1.0)
    a = a.reshape(shape)
    u = u.reshape(shape)
    carry = jnp.broadcast_to(h0, (SUBLANES, a.shape[1]))
    groups = []
    for g in range(a.shape[0] // SUBLANES):
        rows = slice(g * SUBLANES, (g + 1) * SUBLANES)
        h = u[rows] + a[rows] * carry
        groups.append(h)
        carry = jnp.broadcast_to(h[SUBLANES - 1:SUBLANES], h.shape)
    return jnp.concatenate(groups, axis=0)


def _mix_out(h, attn, lru_out, w):
    mixed = jnp.concatenate(
        [_rms(attn, w["g_attn_out"][...]), _rms(lru_out, w["g_lru_out"][...])], axis=-1)
    return h + _dot(mixed.astype(BF16), w["w_out"][...])


def _ffn_piece(hn, acc, w, c0, c1):
    gate = _dot(hn, w["w_ffn_gate"][:, c0:c1])
    up = _dot(hn, w["w_ffn_up"][:, c0:c1])
    act = (gate * _sigmoid(gate)) * up
    return acc + _dot(act.astype(BF16), w["w_ffn_down"][c0:c1, :])


def _ffn_bounds(d_ff, step):
    return [(c, min(c + step, d_ff)) for c in range(0, d_ff, step)]


def _ple_out(h, p, w, final_norm):
    gate = _sigmoid(_dot(_rms(h, w["g_ple"][...]).astype(BF16), w["w_ple_gate"][...]))
    h = h + _dot(p.astype(BF16), w["w_ple_proj"][...]) * gate
    if final_norm:
        h = _rms(h, w["g_final"][...])
    return h


def _dense_tail(h, attn, lru_out, p, w, final_norm):
    h = _mix_out(h, attn, lru_out, w)
    hn = _rms(h, w["g_ffn"][...]).astype(BF16)
    d_ff = w["w_ffn_gate"].shape[1]
    for c0, c1 in _ffn_bounds(d_ff, -(-d_ff // (2 * LANES)) * LANES):
        h = _ffn_piece(hn, h, w, c0, c1)
    return _ple_out(h, p, w, final_norm)


def _softmax_pv(scores, values, transposed=None):
    transposed = transposed or [False] * len(values)
    m = functools.reduce(jnp.maximum, [jnp.max(s, axis=-1, keepdims=True) for s in scores])
    num = None
    den = None
    for s, v, v_t in zip(scores, values, transposed):
        e = jnp.exp2(s - m)
        l = jnp.sum(e, axis=-1, keepdims=True)
        o = _dot_nt(e.astype(BF16), v) if v_t else _dot(e.astype(BF16), v)
        num = o if num is None else num + o
        den = l if den is None else den + l
    return num / den


WEIGHT_NAMES = (
    "g_mix", "w_in", "conv_w", "conv_b", "w_r", "b_r", "w_i", "b_i", "lam",
    "g_attn_out", "g_lru_out", "w_out", "g_ffn", "w_ffn_gate", "w_ffn_up", "w_ffn_down",
    "g_ple", "w_ple_gate", "w_ple_proj", "g_final")
STAGED_NAMES = ("w_out", "w_ffn_gate", "w_ffn_up", "w_ffn_down", "w_ple_gate", "w_ple_proj")


def _order_after(x, *deps):
    bits = None
    for dep in deps:
        b = pltpu.bitcast(dep[:SUBLANES, :LANES], jnp.uint32)
        bits = b if bits is None else bits | b
    zero = pltpu.bitcast((bits >> 16) >> 16, F32)
    head = x[:SUBLANES] + jnp.concatenate([zero] * (x.shape[1] // LANES), axis=1)
    return jnp.concatenate([head, x[SUBLANES:]], axis=0)


def _project(x, w):
    xn = _rms(x, w["g_mix"][...]).astype(BF16)
    proj = _dot(xn, w["w_in"][...])
    aw = w["g_attn_out"].shape[1]
    lw = w["g_lru_out"].shape[1]
    q = proj[:, :aw] * (SCALE * LOG2E)
    k = proj[:, aw:2 * aw]
    v = proj[:, 2 * aw:3 * aw]
    xr = proj[:, 3 * aw:3 * aw + lw]
    gr = proj[:, 3 * aw + lw:]
    return q, k, v, xr, gr


def _conv(xr_buf, base, rows, w):
    cw = w["conv_w"]
    ext = xr_buf[base - SUBLANES:base + rows, :]
    xc = w["conv_b"][...] + ext[SUBLANES:] * cw[CONV_WIDTH - 1:CONV_WIDTH, :]
    for back in range(1, CONV_WIDTH):
        j = CONV_WIDTH - 1 - back
        xc = xc + pltpu.roll(ext, back, axis=0)[SUBLANES:] * cw[j:j + 1, :]
    return xc


def _bias_kernel(ext_ref, bias_ref, bias2_ref):
    n_heads = ext_ref.shape[0]
    col = lax.broadcasted_iota(jnp.int32, (CHUNK, WINDOW), 1)
    for h in range(n_heads):
        rows = jnp.broadcast_to(ext_ref[h], (CHUNK, EXT))
        rolled = pltpu.roll(rows, EXT - (CHUNK - 1), axis=1, stride=1, stride_axis=0) * LOG2E
        bias_ref[h] = rolled[:, :BAND]
        for e in range(2):
            shifted = rolled if e == 0 else pltpu.roll(rolled, CHUNK, axis=1)
            in_band = jnp.logical_and(col >= e * CHUNK, col < e * CHUNK + BAND)
            r0 = e * 2 * CHUNK + (h % 2) * CHUNK
            bias2_ref[h // 2, r0:r0 + CHUNK, :] = jnp.where(in_band, shifted, NEG)


def _bias_tables(rel_table):
    n_heads, rel_size = rel_table.shape
    assert rel_size == REL_CLIP + CHUNK and EXT == WINDOW and n_heads % 2 == 0
    left = (BAND - 1) - REL_CLIP
    ext = jnp.pad(rel_table.astype(F32), ((0, 0), (left, EXT - left - rel_size)), mode="edge")
    return pl.pallas_call(
        _bias_kernel,
        out_shape=(jax.ShapeDtypeStruct((n_heads, CHUNK, BAND), F32),
                   jax.ShapeDtypeStruct((n_heads // 2, 4 * CHUNK, WINDOW), F32)),
        name="rel_bias_expand",
    )(ext.reshape(n_heads, 1, EXT))


def _prompt_kernel(*refs, tile, seq, final_norm):
    x_ref, xprev_ref, pprev_ref, bias_ref = refs[:4]
    nw = len(WEIGHT_NAMES)
    w = dict(zip(WEIGHT_NAMES, refs[4:4 + nw]))
    y_ref, ko_ref, vo_ref, convo_ref, ho_ref = refs[4 + nw:9 + nw]
    kbuf, vbuf, q_scr, attn_scr, gr_scr, xr_buf, h_scr = refs[9 + nw:16 + nw]
    staged = refs[16 + nw:16 + nw + len(STAGED_NAMES)]
    stage_sem = refs[16 + nw + len(STAGED_NAMES)]

    staged_hbm = [w[n] for n in STAGED_NAMES]

    def stage_copy(k):
        return pltpu.make_async_copy(staged_hbm[k], staged[k], stage_sem.at[k])

    very_first = jnp.logical_and(pl.program_id(0) == 0, pl.program_id(1) == 0)
    second = jnp.logical_and(pl.program_id(0) == 0, pl.program_id(1) == 1)

    @pl.when(very_first)
    def _():
        for k in range(len(STAGED_NAMES)):
            stage_copy(k).start()

    @pl.when(second)
    def _():
        for k in range(len(STAGED_NAMES)):
            stage_copy(k).wait()

    w = dict(w)
    w.update(zip(STAGED_NAMES, staged))

    t = pl.program_id(1)
    n_tiles = seq // tile
    n_pairs = kbuf.shape[1] // LANES
    keep_from = seq - PAD
    d_model = x_ref.shape[2]
    d_ff = w["w_ffn_gate"].shape[1]
    rows_w = tile // 16

    lane = lax.broadcasted_iota(jnp.int32, (CHUNK, LANES), 1)
    low_half = lane < HEAD_DIM
    units = [(jj, pr) for jj in range(tile // (2 * CHUNK)) for pr in range(n_pairs)]
    band_row = lax.broadcasted_iota(jnp.int32, (4 * CHUNK, WINDOW), 0)
    band_col = lax.broadcasted_iota(jnp.int32, (4 * CHUNK, WINDOW), 1)
    band_lo = jnp.where(band_row >= 2 * CHUNK, CHUNK, 0)
    masked = jnp.where(band_col >= band_lo, NEG, 2 * NEG)
    masked = jnp.where(band_col < band_lo + BAND, masked, 2 * NEG)
    kidx = lax.broadcasted_iota(jnp.int32, (1, WINDOW), 1)

    def step(front, back):
        st = {}

        if back:
            xc = _conv(xr_buf, SUBLANES, tile, w)
            xcb = xc.astype(BF16)
            r_pre = _gate_dot(xcb, w["w_r"])
            i_pre = _gate_dot(xcb, w["w_i"])
            h0 = h_scr[...]
            gr_prev = gr_scr[...]
            xr_tail = xr_buf[tile:tile + SUBLANES, :]

        if front:
            x = x_ref[0]
            if back:
                x = _order_after(x, r_pre, i_pre)
            q, k, v, xr, gr = _project(x, w)

        if back:
            a, u = _lru_inputs(xc, r_pre, i_pre, w)
            u = _scan_blocked(a, u, h0)
            lru_out = u * _gelu_tanh(gr_prev)
            h_scr[...] = u[tile - 1:tile, :]
            xr_buf[0:SUBLANES, :] = xr_tail

        if front:
            q_scr[...] = q.astype(BF16)
            kbuf[PAD:PAD + tile, :] = k.astype(BF16)
            vbuf[PAD:PAD + tile, :] = v.astype(BF16)
            xr_buf[SUBLANES:SUBLANES + tile, :] = xr
            gr_scr[...] = gr

            @pl.when(t * tile >= keep_from)
            def _():
                off = pl.multiple_of(t * tile - keep_from, tile)
                ko_ref[0, pl.ds(off, tile), :] = k
                vo_ref[0, pl.ds(off, tile), :] = v

        if not front:
            convo_ref[0] = xr_buf[SUBLANES - (CONV_WIDTH - 1):SUBLANES, :]
            ho_ref[0] = h_scr[...]

        dense = []

        def ffn_norm():
            st["hn"] = _rms(st["h"], w["g_ffn"][...]).astype(BF16)

        def ffn_gate(c0, c1):
            st["g"] = _dot(st["hn"], w["w_ffn_gate"][:, c0:c1])

        def ffn_up(c0, c1):
            g = st.pop("g")
            st["act"] = ((g * _sigmoid(g)) * _dot(st["hn"], w["w_ffn_up"][:, c0:c1])).astype(BF16)

        def ffn_down(c0, c1):
            st["h"] = st["h"] + _dot(st.pop("act"), w["w_ffn_down"][c0:c1, :])

        def ple_gate():
            hpn = _rms(st["h"], w["g_ple"][...]).astype(BF16)
            st["pg"] = _sigmoid(_dot(hpn, w["w_ple_gate"][...]))

        def ple_out():
            h = st["h"] + _dot(pprev_ref[0].astype(BF16), w["w_ple_proj"][...]) * st.pop("pg")
            if final_norm:
                h = _rms(h, w["g_final"][...])
            y_ref[0] = h

        if back:
            st["h"] = _mix_out(xprev_ref[0], attn_scr[...], lru_out, w)
            dense += [(0, ffn_norm)]
            for c0, c1 in _ffn_bounds(d_ff, FFN_CHUNK):
                wt = rows_w * (d_model // MXU_DIM) * -(-(c1 - c0) // MXU_DIM)
                dense += [(wt, functools.partial(ffn_gate, c0, c1)),
                          (wt, functools.partial(ffn_up, c0, c1)),
                          (wt, functools.partial(ffn_down, c0, c1))]
            dense += [(rows_w * (d_model // MXU_DIM) ** 2, ple_gate)]
            dense += [(rows_w * (d_model // MXU_DIM), ple_out)]

        if not front:
            for _, piece in dense:
                piece()
            return

        def scores(jj, pr):
            lanes = slice(pr * LANES, (pr + 1) * LANES)
            r0 = jj * 2 * CHUNK
            parts = []
            for c in range(2):
                q2 = q_scr[r0 + c * CHUNK:r0 + (c + 1) * CHUNK, lanes]
                zero = jnp.zeros_like(q2)
                parts += [jnp.where(low_half, q2, zero), jnp.where(low_half, zero, q2)]
            s = _dot_nt(jnp.concatenate(parts, axis=0), kbuf[r0:r0 + WINDOW, lanes]) + bias_ref[pr]
            valid = (t * tile + r0 + kidx - PAD) >= PAD
            return jnp.where(valid, s, masked)

        def attend(jj, pr, s):
            lanes = slice(pr * LANES, (pr + 1) * LANES)
            r0 = jj * 2 * CHUNK
            o = _softmax_pv([s], [vbuf[r0:r0 + WINDOW, lanes]])
            for c in range(2):
                st["attn", 2 * jj + c, pr] = jnp.where(
                    low_half, o[2 * c * CHUNK:(2 * c + 1) * CHUNK],
                    o[(2 * c + 1) * CHUNK:(2 * c + 2) * CHUNK])

        total_w = sum(wt for wt, _ in dense)
        done_w = 0
        pending = list(dense)
        s_cur = scores(*units[0])
        for i, unit in enumerate(units):
            s_next = scores(*units[i + 1]) if i + 1 < len(units) else None
            share = total_w * (i + 1) // len(units)
            while pending and (done_w < share or i + 1 == len(units)):
                wt, piece = pending.pop(0)
                piece()
                done_w += wt
            attend(*unit, s_cur)
            s_cur = s_next

        for j in range(tile // CHUNK):
            for pr in range(n_pairs):
                attn_scr[j * CHUNK:(j + 1) * CHUNK, pr * LANES:(pr + 1) * LANES] = st.pop(
                    ("attn", j, pr))

        for r in range(0, PAD, tile):
            n = min(tile, PAD - r)
            kbuf[r:r + n, :] = kbuf[r + tile:r + tile + n, :]
            vbuf[r:r + n, :] = vbuf[r + tile:r + tile + n, :]

    @pl.when(t == 0)
    def _():
        kbuf[0:PAD, :] = jnp.zeros((PAD, kbuf.shape[1]), BF16)
        vbuf[0:PAD, :] = jnp.zeros((PAD, vbuf.shape[1]), BF16)
        xr_buf[0:SUBLANES, :] = jnp.zeros((SUBLANES, xr_buf.shape[1]), F32)
        h_scr[...] = jnp.zeros(h_scr.shape, F32)
        step(front=True, back=False)

    @pl.when(jnp.logical_and(t > 0, t < n_tiles))
    def _():
        step(front=True, back=True)

    @pl.when(t == n_tiles)
    def _():
        step(front=False, back=True)


def _const_spec(arr):
    nd = arr.ndim
    return pl.BlockSpec(arr.shape, lambda *_: (0,) * nd, pipeline_mode=pl.Buffered(1))


def _prompt_layer(x, p, bias, weights, final_norm, tile):
    b, seq, d = x.shape
    aw = weights["g_attn_out"].shape[1]
    lw = weights["g_lru_out"].shape[1]
    assert seq % tile == 0 and PAD % tile == 0 and seq >= PAD and tile % CHUNK == 0
    assert tile & (tile - 1) == 0 and aw % LANES == 0 and tile % (2 * CHUNK) == 0
    n_tiles = seq // tile
    wlist = [weights[n] for n in WEIGHT_NAMES]
    kern = functools.partial(_prompt_kernel, tile=tile, seq=seq, final_norm=final_norm)
    out_shape = (
        jax.ShapeDtypeStruct((b, seq, d), F32),
        jax.ShapeDtypeStruct((b, PAD, aw), F32),
        jax.ShapeDtypeStruct((b, PAD, aw), F32),
        jax.ShapeDtypeStruct((b, CONV_WIDTH - 1, lw), F32),
        jax.ShapeDtypeStruct((b, 1, lw), F32),
    )
    cur = lambda i, j: (i, jnp.minimum(j, n_tiles - 1), 0)
    prev = lambda i, j: (i, jnp.maximum(j - 1, 0), 0)
    in_specs = [
        pl.BlockSpec((1, tile, d), cur),
        pl.BlockSpec((1, tile, d), prev),
        pl.BlockSpec((1, tile, p.shape[-1]), prev),
        _const_spec(bias),
    ] + [pl.BlockSpec(memory_space=pl.ANY) if n in STAGED_NAMES else _const_spec(weights[n])
         for n in WEIGHT_NAMES]
    out_specs = (
        pl.BlockSpec((1, tile, d), prev),
        pl.BlockSpec((1, PAD, aw), lambda i, j: (i, 0, 0)),
        pl.BlockSpec((1, PAD, aw), lambda i, j: (i, 0, 0)),
        pl.BlockSpec((1, CONV_WIDTH - 1, lw), lambda i, j: (i, 0, 0)),
        pl.BlockSpec((1, 1, lw), lambda i, j: (i, 0, 0)),
    )
    scratch = [
        pltpu.VMEM((PAD + tile, aw), BF16),
        pltpu.VMEM((PAD + tile, aw), BF16),
        pltpu.VMEM((tile, aw), BF16),
        pltpu.VMEM((tile, aw), F32),
        pltpu.VMEM((tile, lw), F32),
        pltpu.VMEM((SUBLANES + tile, lw), F32),
        pltpu.VMEM((1, lw), F32),
    ]
    scratch += [pltpu.VMEM(weights[n].shape, weights[n].dtype) for n in STAGED_NAMES]
    scratch += [pltpu.SemaphoreType.DMA((len(STAGED_NAMES),))]
    return pl.pallas_call(
        kern,
        out_shape=out_shape,
        grid=(b, n_tiles + 1),
        in_specs=in_specs,
        out_specs=out_specs,
        scratch_shapes=scratch,
        compiler_params=pltpu.CompilerParams(
            dimension_semantics=("arbitrary", "arbitrary"), vmem_limit_bytes=VMEM_LIMIT),
        name="prompt_layer",
    )(x, x, p, bias, *wlist)


FRONT_NAMES = ("g_mix", "w_in", "conv_w", "conv_b", "w_r", "b_r", "w_i", "b_i", "lam",
               "g_attn_out", "g_lru_out")
BACK_NAMES = ("g_attn_out", "g_lru_out", "w_out", "g_ffn", "w_ffn_gate", "w_ffn_up",
              "w_ffn_down", "g_ple", "w_ple_gate", "w_ple_proj", "g_final")


def _sample_front_kernel(*refs, n_heads):
    x_ref, ck_ref, cv_ref, sconv_ref, sh_ref, bias_ref = refs[:6]
    nw = len(FRONT_NAMES)
    w = dict(zip(FRONT_NAMES, refs[6:6 + nw]))
    attn_ref, lru_ref, ko_ref, vo_ref, convo_ref, ho_ref = refs[6 + nw:12 + nw]
    (xr_buf,) = refs[12 + nw:]

    nb, t, d = x_ref.shape
    n_cache = ck_ref.shape[3]
    aw = n_heads * HEAD_DIM
    rows = nb * t

    x = x_ref[...].reshape(rows, d)
    q, k, v, xr, gr = _project(x, w)
    ko_ref[...] = k.reshape(nb, t, aw)
    vo_ref[...] = v.reshape(nb, t, aw)
    qb = q.astype(BF16)
    kb_new = k.astype(BF16)
    vb_new = v.astype(BF16)

    def scores(b, h):
        rs = slice(b * t, (b + 1) * t)
        cols = slice(h * HEAD_DIM, (h + 1) * HEAD_DIM)
        qh = qb[rs, cols]
        kc_t = ck_ref[b, h].astype(BF16)
        s_c = _dot(qh, kc_t) + bias_ref[h, 0:t, PAD - n_cache:PAD]
        s_n = _dot_nt(qh, kb_new[rs, cols]) + bias_ref[h, 0:t, PAD:PAD + t]
        return s_c, s_n

    s_cur = [scores(0, h) for h in range(n_heads)]
    for b in range(nb):
        s_next = [scores(b + 1, h) for h in range(n_heads)] if b + 1 < nb else None
        outs = []
        for h in range(n_heads):
            vc_t = cv_ref[b, h].astype(BF16)
            vn = vb_new[b * t:(b + 1) * t, h * HEAD_DIM:(h + 1) * HEAD_DIM]
            outs.append(_softmax_pv(list(s_cur[h]), [vc_t, vn], [True, False]))
        attn_ref[b] = jnp.concatenate(outs, axis=-1)
        s_cur = s_next

    seg_rows = SUBLANES + t
    xcs = []
    for b in range(nb):
        base = b * seg_rows
        xr_buf[base + SUBLANES - (CONV_WIDTH - 1):base + SUBLANES, :] = sconv_ref[b]
        xr_buf[base + SUBLANES:base + seg_rows, :] = xr[b * t:(b + 1) * t, :]
        xcs.append(_conv(xr_buf, base + SUBLANES, t, w))
        convo_ref[b] = xr_buf[base + seg_rows - (CONV_WIDTH - 1):base + seg_rows, :]
    xc = jnp.concatenate(xcs, axis=0)
    h0_rows = jnp.concatenate(
        [jnp.broadcast_to(sh_ref[b], (t, sh_ref.shape[2])) for b in range(nb)], axis=0)
    hs, lru_out = _rglru(xc, gr, h0_rows, t, w)
    for b in range(nb):
        ho_ref[b] = hs[(b + 1) * t - 1:(b + 1) * t, :]
    lru_ref[...] = lru_out.reshape(nb, t, lru_out.shape[1])


def _sample_back_kernel(*refs, final_norm):
    x_ref, attn_ref, lru_ref, p_ref = refs[:4]
    nw = len(BACK_NAMES)
    w = dict(zip(BACK_NAMES, refs[4:4 + nw]))
    y_ref = refs[4 + nw]
    y_ref[...] = _dense_tail(x_ref[...], attn_ref[...], lru_ref[...], p_ref[...], w, final_norm)


def _sample_layer(x, p, cache_k, cache_v, state_conv, state_h, bias, weights, final_norm, group,
                  back_rows):
    nbatch, t, d = x.shape
    n_heads = bias.shape[0]
    aw = n_heads * HEAD_DIM
    n_cache = cache_k.shape[3]
    lw = weights["g_lru_out"].shape[1]
    assert nbatch % group == 0 and t % SUBLANES == 0 and t & (t - 1) == 0
    assert CONV_WIDTH - 1 <= t <= CHUNK and n_cache <= PAD and cache_k.shape[1:3] == (n_heads, HEAD_DIM)
    assert (nbatch * t) % back_rows == 0
    state_h3 = state_h.reshape(nbatch, 1, lw)

    def grp(shape):
        nd = len(shape)
        return pl.BlockSpec((group,) + tuple(shape[1:]), lambda i: (i,) + (0,) * (nd - 1))

    front_w = [weights[n] for n in FRONT_NAMES]
    front_out = (
        jax.ShapeDtypeStruct((nbatch, t, aw), F32),
        jax.ShapeDtypeStruct((nbatch, t, lw), F32),
        jax.ShapeDtypeStruct((nbatch, t, aw), F32),
        jax.ShapeDtypeStruct((nbatch, t, aw), F32),
        jax.ShapeDtypeStruct((nbatch, CONV_WIDTH - 1, lw), F32),
        jax.ShapeDtypeStruct((nbatch, 1, lw), F32),
    )
    ins = [x, cache_k, cache_v, state_conv, state_h3]
    attn, lru_out, k_new, v_new, conv_new, h_new = pl.pallas_call(
        functools.partial(_sample_front_kernel, n_heads=n_heads),
        out_shape=front_out,
        grid=(nbatch // group,),
        in_specs=[grp(a.shape) for a in ins] + [_const_spec(bias)]
        + [_const_spec(a) for a in front_w],
        out_specs=tuple(grp(o.shape) for o in front_out),
        scratch_shapes=[pltpu.VMEM((group * (SUBLANES + t), lw), F32)],
        compiler_params=pltpu.CompilerParams(
            dimension_semantics=("arbitrary",), vmem_limit_bytes=VMEM_LIMIT),
        name="sample_front",
    )(*ins, bias, *front_w)

    rows = nbatch * t
    back_w = [weights[n] for n in BACK_NAMES]
    flat = [x.reshape(rows, d), attn.reshape(rows, aw), lru_out.reshape(rows, lw),
            p.reshape(rows, p.shape[-1])]
    y = pl.pallas_call(
        functools.partial(_sample_back_kernel, final_norm=final_norm),
        out_shape=jax.ShapeDtypeStruct((rows, d), F32),
        grid=(rows // back_rows,),
        in_specs=[pl.BlockSpec((back_rows, a.shape[1]), lambda i: (i, 0)) for a in flat]
        + [_const_spec(a) for a in back_w],
        out_specs=pl.BlockSpec((back_rows, d), lambda i: (i, 0)),
        compiler_params=pltpu.CompilerParams(
            dimension_semantics=("arbitrary",), vmem_limit_bytes=VMEM_LIMIT),
        name="sample_back",
    )(*flat, *back_w)
    return y.reshape(nbatch, t, d), k_new, v_new, conv_new, h_new


def _block_diag(wb):
    n, kk, jj = wb.shape
    eye = jnp.eye(n, dtype=wb.dtype)
    return (wb[:, :, None, :] * eye[:, None, :, None]).reshape(n * kk, n * jj)


PROMPT_TILE = 256
SAMPLE_GROUP = 8
SAMPLE_BACK_ROWS = 256


def kernel(x_prompt, x_sample, p_prompt, p_sample, cache_k, cache_v, state_conv, state_h, g_mix, w_in, conv_w, conv_b, w_rgate, b_rgate, w_igate, b_igate, lru_lambda, rel_bias_table, g_attn_out, g_lru_out, w_out, g_ffn, w_ffn_gate, w_ffn_up, w_ffn_down, g_ple, w_ple_gate, w_ple_proj, g_final):
    depth = w_in.shape[0]
    hp, hs = x_prompt, x_sample
    outs = [[] for _ in range(8)]
    for i in range(depth):
        row = lambda a: a[i].reshape(1, -1).astype(F32)
        weights = {
            "g_mix": row(g_mix), "w_in": w_in[i].astype(BF16),
            "conv_w": conv_w[i].astype(F32), "conv_b": row(conv_b),
            "w_r": _block_diag(w_rgate[i]).astype(BF16), "b_r": row(b_rgate),
            "w_i": _block_diag(w_igate[i]).astype(BF16), "b_i": row(b_igate),
            "lam": row(lru_lambda),
            "g_attn_out": row(g_attn_out), "g_lru_out": row(g_lru_out),
            "w_out": w_out[i].astype(BF16), "g_ffn": row(g_ffn),
            "w_ffn_gate": w_ffn_gate[i].astype(BF16), "w_ffn_up": w_ffn_up[i].astype(BF16),
            "w_ffn_down": w_ffn_down[i].astype(BF16), "g_ple": row(g_ple),
            "w_ple_gate": w_ple_gate[i].astype(BF16), "w_ple_proj": w_ple_proj[i].astype(BF16),
            "g_final": g_final.reshape(1, -1).astype(F32),
        }
        final = i == depth - 1
        bias, bias2 = _bias_tables(rel_bias_table[i])
        n_heads = rel_bias_table.shape[1]
        aw = n_heads * HEAD_DIM

        hp, k1, v1, c1, r1 = _prompt_layer(
            hp, p_prompt[i], bias2, weights, final, PROMPT_TILE)
        nb, nc = cache_k.shape[1:3]
        hs, k2, v2, c2, r2 = _sample_layer(
            hs, p_sample[i], jnp.transpose(cache_k[i], (0, 2, 3, 1)),
            jnp.transpose(cache_v[i], (0, 2, 3, 1)), state_conv[i], state_h[i], bias,
            weights, final, SAMPLE_GROUP, SAMPLE_BACK_ROWS)
        bp, keep = k1.shape[:2]
        ts = k2.shape[1]
        for lst, val in zip(outs, (
                k1.reshape(bp, keep, n_heads, HEAD_DIM), v1.reshape(bp, keep, n_heads, HEAD_DIM),
                c1, r1.reshape(bp, -1),
                k2.reshape(nb, ts, n_heads, HEAD_DIM), v2.reshape(nb, ts, n_heads, HEAD_DIM),
                c2, r2.reshape(nb, -1))):
            lst.append(val)
    return (hp, hs) + tuple(jnp.stack(l) for l in outs)
```

```python
import functools

import jax
import jax.numpy as jnp
from jax import lax
from jax.experimental import pallas as pl
from jax.experimental.pallas import tpu as pltpu

CHUNK = 64
LEFT_CHUNKS = 8
PAD = LEFT_CHUNKS * CHUNK
BAND = PAD + CHUNK
HEAD_DIM = 64
CONV_WIDTH = 4
LRU_C = 8.0
REL_CLIP = 128
EPS = 1e-6
NEG = -1e30
SCALE = HEAD_DIM ** -0.5
LOG2E = 1.4426950408889634
LANES = 128
SUBLANES = 8
MXU_DIM = 256
EXT = 640
WINDOW = BAND + CHUNK
VMEM_LIMIT = 60 * 1024 * 1024
FFN_CHUNK = 4 * MXU_DIM

F32 = jnp.float32
BF16 = jnp.bfloat16


def _dot(a, b):
    return jnp.dot(a, b, preferred_element_type=F32)


def _dot_nt(a, b):
    return lax.dot_general(a, b, (((1,), (1,)), ((), ())), preferred_element_type=F32)


def _rms(x, g):
    ms = jnp.mean(x * x, axis=-1, keepdims=True)
    return (x * lax.rsqrt(ms + EPS)) * g


def _sigmoid(x):
    return 1.0 / (1.0 + jnp.exp2(x * (-LOG2E)))


def _gelu_tanh(x):
    c = -0.7978845608028654 * 2.0 * LOG2E
    return x / (1.0 + jnp.exp2(x * (c + (c * 0.044715) * (x * x))))


def _shift_rows(x, d, fill, seg):
    rows = x.shape[0]
    if seg == rows and d % SUBLANES == 0:
        return jnp.concatenate([jnp.full((d, x.shape[1]), fill, x.dtype), x[:rows - d]], axis=0)
    rolled = pltpu.roll(x, d, axis=0)
    row = lax.broadcasted_iota(jnp.int32, x.shape, 0)
    return jnp.where((row & (seg - 1)) >= d, rolled, fill)


def _scan_distances(seg):
    return [1 << b for b in range(seg.bit_length() - 1)]


def _scan_step(a, u, d, seg):
    u = u + a * _shift_rows(u, d, 0.0, seg)
    if 2 * d < seg:
        a = a * _shift_rows(a, d, 1.0, seg)
    return a, u


def _sqrt_nonneg(y):
    return jnp.where(y > 0.0, y * lax.rsqrt(y), 0.0)


def _lru_inputs(xc, r_pre, i_pre, w):
    r = _sigmoid(r_pre + w["b_r"][...])
    ig = _sigmoid(i_pre + w["b_i"][...])
    z = -w["lam"][...]
    softplus = jnp.maximum(z, 0.0) + jnp.log1p(jnp.exp(-jnp.abs(z)))
    a = jnp.exp2(r * ((-LRU_C * LOG2E) * softplus))
    u = _sqrt_nonneg(1.0 - a * a) * (ig * xc)
    return a, u


def _gate_dot(xcb, w_ref):
    n = w_ref.shape[0]
    if n % MXU_DIM:
        return _dot(xcb, w_ref[...])
    return jnp.concatenate(
        [_dot(xcb[:, c:c + MXU_DIM], w_ref[c:c + MXU_DIM, c:c + MXU_DIM])
         for c in range(0, n, MXU_DIM)], axis=1)


def _rglru(xc, gr, h0_rows, seg, w):
    xcb = xc.astype(BF16)
    a, u = _lru_inputs(xc, _gate_dot(xcb, w["w_r"]), _gate_dot(xcb, w["w_i"]), w)
    row = lax.broadcasted_iota(jnp.int32, xc.shape, 0)
    u = u + jnp.where((row & (seg - 1)) == 0, a * h0_rows, 0.0)
    for d in _scan_distances(seg):
        a, u = _scan_step(a, u, d, seg)
    return u, u * _gelu_tanh(gr)


def _scan_blocked(a, u, h0):
    shape = a.shape
    grouped = (shape[0] // SUBLANES, SUBLANES, shape[1])
    a = a.reshape(grouped)
    u = u.reshape(grouped)
    row = lax.broadcasted_iota(jnp.int32, grouped, 1)
    for d in _scan_distances(SUBLANES):
        inside = row >= d
        u = u + jnp.where(inside, a, 0.0) * pltpu.roll(u, d, axis=1)
        a = a * jnp.where(inside, pltpu.roll(a, d, axis=1), 1.0)
    a = a.reshape(shape)
    u = u.reshape(shape)
    carry = jnp.broadcast_to(h0, (SUBLANES, a.shape[1]))
    groups = []
    for g in range(a.shape[0] // SUBLANES):
        rows = slice(g * SUBLANES, (g + 1) * SUBLANES)
        h = u[rows] + a[rows] * carry
        groups.append(h)
        carry = jnp.broadcast_to(h[SUBLANES - 1:SUBLANES], h.shape)
    return jnp.concatenate(groups, axis=0)


def _mix_out(h, attn, lru_out, w):
    mixed = jnp.concatenate(
        [_rms(attn, w["g_attn_out"][...]), _rms(lru_out, w["g_lru_out"][...])], axis=-1)
    return h + _dot(mixed.astype(BF16), w["w_out"][...])


def _ffn_piece(hn, acc, w, c0, c1):
    gate = _dot(hn, w["w_ffn_gate"][:, c0:c1])
    up = _dot(hn, w["w_ffn_up"][:, c0:c1])
    act = (gate * _sigmoid(gate)) * up
    return acc + _dot(act.astype(BF16), w["w_ffn_down"][c0:c1, :])


def _ffn_bounds(d_ff, step):
    return [(c, min(c + step, d_ff)) for c in range(0, d_ff, step)]


def _ple_out(h, p, w, final_norm):
    gate = _sigmoid(_dot(_rms(h, w["g_ple"][...]).astype(BF16), w["w_ple_gate"][...]))
    h = h + _dot(p.astype(BF16), w["w_ple_proj"][...]) * gate
    if final_norm:
        h = _rms(h, w["g_final"][...])
    return h


def _dense_tail(h, attn, lru_out, p, w, final_norm):
    h = _mix_out(h, attn, lru_out, w)
    hn = _rms(h, w["g_ffn"][...]).astype(BF16)
    d_ff = w["w_ffn_gate"].shape[1]
    for c0, c1 in _ffn_bounds(d_ff, -(-d_ff // (2 * LANES)) * LANES):
        h = _ffn_piece(hn, h, w, c0, c1)
    return _ple_out(h, p, w, final_norm)


def _softmax_pv(scores, values, transposed=None):
    transposed = transposed or [False] * len(values)
    m = functools.reduce(jnp.maximum, [jnp.max(s, axis=-1, keepdims=True) for s in scores])
    num = None
    den = None
    for s, v, v_t in zip(scores, values, transposed):
        e = jnp.exp2(s - m)
        l = jnp.sum(e, axis=-1, keepdims=True)
        o = _dot_nt(e.astype(BF16), v) if v_t else _dot(e.astype(BF16), v)
        num = o if num is None else num + o
        den = l if den is None else den + l
    return num / den


WEIGHT_NAMES = (
    "g_mix", "w_in", "conv_w", "conv_b", "w_r", "b_r", "w_i", "b_i", "lam",
    "g_attn_out", "g_lru_out", "w_out", "g_ffn", "w_ffn_gate", "w_ffn_up", "w_ffn_down",
    "g_ple", "w_ple_gate", "w_ple_proj", "g_final")


def _order_after(x, *deps):
    bits = None
    for dep in deps:
        b = pltpu.bitcast(dep[:SUBLANES, :LANES], jnp.uint32)
        bits = b if bits is None else bits | b
    zero = pltpu.bitcast((bits >> 16) >> 16, F32)
    head = x[:SUBLANES] + jnp.concatenate([zero] * (x.shape[1] // LANES), axis=1)
    return jnp.concatenate([head, x[SUBLANES:]], axis=0)


def _project(x, w):
    xn = _rms(x, w["g_mix"][...]).astype(BF16)
    proj = _dot(xn, w["w_in"][...])
    aw = w["g_attn_out"].shape[1]
    lw = w["g_lru_out"].shape[1]
    q = proj[:, :aw] * (SCALE * LOG2E)
    k = proj[:, aw:2 * aw]
    v = proj[:, 2 * aw:3 * aw]
    xr = proj[:, 3 * aw:3 * aw + lw]
    gr = proj[:, 3 * aw + lw:]
    return q, k, v, xr, gr


def _conv(xr_buf, base, rows, w):
    cw = w["conv_w"]
    ext = xr_buf[base - SUBLANES:base + rows, :]
    xc = w["conv_b"][...] + ext[SUBLANES:] * cw[CONV_WIDTH - 1:CONV_WIDTH, :]
    for back in range(1, CONV_WIDTH):
        j = CONV_WIDTH - 1 - back
        xc = xc + pltpu.roll(ext, back, axis=0)[SUBLANES:] * cw[j:j + 1, :]
    return xc


def _bias_kernel(ext_ref, bias_ref, bias2_ref):
    n_heads = ext_ref.shape[0]
    col = lax.broadcasted_iota(jnp.int32, (CHUNK, WINDOW), 1)
    for h in range(n_heads):
        rows = jnp.broadcast_to(ext_ref[h], (CHUNK, EXT))
        rolled = pltpu.roll(rows, EXT - (CHUNK - 1), axis=1, stride=1, stride_axis=0) * LOG2E
        bias_ref[h] = rolled[:, :BAND]
        for e in range(2):
            shifted = rolled if e == 0 else pltpu.roll(rolled, CHUNK, axis=1)
            in_band = jnp.logical_and(col >= e * CHUNK, col < e * CHUNK + BAND)
            r0 = e * 2 * CHUNK + (h % 2) * CHUNK
            bias2_ref[h // 2, r0:r0 + CHUNK, :] = jnp.where(in_band, shifted, NEG)


def _bias_tables(rel_table):
    n_heads, rel_size = rel_table.shape
    assert rel_size == REL_CLIP + CHUNK and EXT == WINDOW and n_heads % 2 == 0
    left = (BAND - 1) - REL_CLIP
    ext = jnp.pad(rel_table.astype(F32), ((0, 0), (left, EXT - left - rel_size)), mode="edge")
    return pl.pallas_call(
        _bias_kernel,
        out_shape=(jax.ShapeDtypeStruct((n_heads, CHUNK, BAND), F32),
                   jax.ShapeDtypeStruct((n_heads // 2, 4 * CHUNK, WINDOW), F32)),
        name="rel_bias_expand",
    )(ext.reshape(n_heads, 1, EXT))


def _prompt_kernel(*refs, tile, seq, final_norm):
    x_ref, xprev_ref, pprev_ref, bias_ref = refs[:4]
    nw = len(WEIGHT_NAMES)
    w = dict(zip(WEIGHT_NAMES, refs[4:4 + nw]))
    y_ref, ko_ref, vo_ref, convo_ref, ho_ref = refs[4 + nw:9 + nw]
    kbuf, vbuf, q_scr, attn_scr, gr_scr, xr_buf, h_scr = refs[9 + nw:]

    t = pl.program_id(1)
    n_tiles = seq // tile
    n_pairs = kbuf.shape[1] // LANES
    keep_from = seq - PAD
    d_model = x_ref.shape[2]
    d_ff = w["w_ffn_gate"].shape[1]
    rows_w = tile // 16

    lane = lax.broadcasted_iota(jnp.int32, (CHUNK, LANES), 1)
    low_half = lane < HEAD_DIM
    units = [(jj, pr) for jj in range(tile // (2 * CHUNK)) for pr in range(n_pairs)]
    band_row = lax.broadcasted_iota(jnp.int32, (4 * CHUNK, WINDOW), 0)
    band_col = lax.broadcasted_iota(jnp.int32, (4 * CHUNK, WINDOW), 1)
    band_lo = jnp.where(band_row >= 2 * CHUNK, CHUNK, 0)
    masked = jnp.where(band_col >= band_lo, NEG, 2 * NEG)
    masked = jnp.where(band_col < band_lo + BAND, masked, 2 * NEG)
    kidx = lax.broadcasted_iota(jnp.int32, (1, WINDOW), 1)

    def step(front, back):
        st = {}

        if back:
            xc = _conv(xr_buf, SUBLANES, tile, w)
            xcb = xc.astype(BF16)
            r_pre = _gate_dot(xcb, w["w_r"])
            i_pre = _gate_dot(xcb, w["w_i"])
            h0 = h_scr[...]
            gr_prev = gr_scr[...]
            xr_tail = xr_buf[tile:tile + SUBLANES, :]

        if front:
            x = x_ref[0]
            if back:
                x = _order_after(x, r_pre, i_pre)
            q, k, v, xr, gr = _project(x, w)

        if back:
            a, u = _lru_inputs(xc, r_pre, i_pre, w)
            u = _scan_blocked(a, u, h0)
            lru_out = u * _gelu_tanh(gr_prev)
            h_scr[...] = u[tile - 1:tile, :]
            xr_buf[0:SUBLANES, :] = xr_tail

        if front:
            q_scr[...] = q.astype(BF16)
            kbuf[PAD:PAD + tile, :] = k.astype(BF16)
            vbuf[PAD:PAD + tile, :] = v.astype(BF16)
            xr_buf[SUBLANES:SUBLANES + tile, :] = xr
            gr_scr[...] = gr

            @pl.when(t * tile >= keep_from)
            def _():
                off = pl.multiple_of(t * tile - keep_from, tile)
                ko_ref[0, pl.ds(off, tile), :] = k
                vo_ref[0, pl.ds(off, tile), :] = v

        if not front:
            convo_ref[0] = xr_buf[SUBLANES - (CONV_WIDTH - 1):SUBLANES, :]
            ho_ref[0] = h_scr[...]

        dense = []

        def ffn_norm():
            st["hn"] = _rms(st["h"], w["g_ffn"][...]).astype(BF16)

        def ffn_gate(c0, c1):
            st["g"] = _dot(st["hn"], w["w_ffn_gate"][:, c0:c1])

        def ffn_up(c0, c1):
            g = st.pop("g")
            st["act"] = ((g * _sigmoid(g)) * _dot(st["hn"], w["w_ffn_up"][:, c0:c1])).astype(BF16)

        def ffn_down(c0, c1):
            st["h"] = st["h"] + _dot(st.pop("act"), w["w_ffn_down"][c0:c1, :])

        def ple_gate():
            hpn = _rms(st["h"], w["g_ple"][...]).astype(BF16)
            st["pg"] = _sigmoid(_dot(hpn, w["w_ple_gate"][...]))

        def ple_out():
            h = st["h"] + _dot(pprev_ref[0].astype(BF16), w["w_ple_proj"][...]) * st.pop("pg")
            if final_norm:
                h = _rms(h, w["g_final"][...])
            y_ref[0] = h

        if back:
            st["h"] = _mix_out(xprev_ref[0], attn_scr[...], lru_out, w)
            dense += [(0, ffn_norm)]
            for c0, c1 in _ffn_bounds(d_ff, FFN_CHUNK):
                wt = rows_w * (d_model // MXU_DIM) * -(-(c1 - c0) // MXU_DIM)
                dense += [(wt, functools.partial(ffn_gate, c0, c1)),
                          (wt, functools.partial(ffn_up, c0, c1)),
                          (wt, functools.partial(ffn_down, c0, c1))]
            dense += [(rows_w * (d_model // MXU_DIM) ** 2, ple_gate)]
            dense += [(rows_w * (d_model // MXU_DIM), ple_out)]

        if not front:
            for _, piece in dense:
                piece()
            return

        def scores(jj, pr):
            lanes = slice(pr * LANES, (pr + 1) * LANES)
            r0 = jj * 2 * CHUNK
            parts = []
            for c in range(2):
                q2 = q_scr[r0 + c * CHUNK:r0 + (c + 1) * CHUNK, lanes]
                zero = jnp.zeros_like(q2)
                parts += [jnp.where(low_half, q2, zero), jnp.where(low_half, zero, q2)]
            s = _dot_nt(jnp.concatenate(parts, axis=0), kbuf[r0:r0 + WINDOW, lanes]) + bias_ref[pr]
            valid = (t * tile + r0 + kidx - PAD) >= PAD
            return jnp.where(valid, s, masked)

        def attend(jj, pr, s):
            lanes = slice(pr * LANES, (pr + 1) * LANES)
            r0 = jj * 2 * CHUNK
            o = _softmax_pv([s], [vbuf[r0:r0 + WINDOW, lanes]])
            for c in range(2):
                st["attn", 2 * jj + c, pr] = jnp.where(
                    low_half, o[2 * c * CHUNK:(2 * c + 1) * CHUNK],
                    o[(2 * c + 1) * CHUNK:(2 * c + 2) * CHUNK])

        total_w = sum(wt for wt, _ in dense)
        done_w = 0
        pending = list(dense)
        s_cur = scores(*units[0])
        for i, unit in enumerate(units):
            s_next = scores(*units[i + 1]) if i + 1 < len(units) else None
            share = total_w * (i + 1) // len(units)
            while pending and (done_w < share or i + 1 == len(units)):
                wt, piece = pending.pop(0)
                piece()
                done_w += wt
            attend(*unit, s_cur)
            s_cur = s_next

        for j in range(tile // CHUNK):
            for pr in range(n_pairs):
                attn_scr[j * CHUNK:(j + 1) * CHUNK, pr * LANES:(pr + 1) * LANES] = st.pop(
                    ("attn", j, pr))

        for r in range(0, PAD, tile):
            n = min(tile, PAD - r)
            kbuf[r:r + n, :] = kbuf[r + tile:r + tile + n, :]
            vbuf[r:r + n, :] = vbuf[r + tile:r + tile + n, :]

    @pl.when(t == 0)
    def _():
        kbuf[0:PAD, :] = jnp.zeros((PAD, kbuf.shape[1]), BF16)
        vbuf[0:PAD, :] = jnp.zeros((PAD, vbuf.shape[1]), BF16)
        xr_buf[0:SUBLANES, :] = jnp.zeros((SUBLANES, xr_buf.shape[1]), F32)
        h_scr[...] = jnp.zeros(h_scr.shape, F32)
        step(front=True, back=False)

    @pl.when(jnp.logical_and(t > 0, t < n_tiles))
    def _():
        step(front=True, back=True)

    @pl.when(t == n_tiles)
    def _():
        step(front=False, back=True)


def _const_spec(arr):
    nd = arr.ndim
    return pl.BlockSpec(arr.shape, lambda *_: (0,) * nd, pipeline_mode=pl.Buffered(1))


def _prompt_layer(x, p, bias, weights, final_norm, tile):
    b, seq, d = x.shape
    aw = weights["g_attn_out"].shape[1]
    lw = weights["g_lru_out"].shape[1]
    assert seq % tile == 0 and PAD % tile == 0 and seq >= PAD and tile % CHUNK == 0
    assert tile & (tile - 1) == 0 and aw % LANES == 0 and tile % (2 * CHUNK) == 0
    n_tiles = seq // tile
    wlist = [weights[n] for n in WEIGHT_NAMES]
    kern = functools.partial(_prompt_kernel, tile=tile, seq=seq, final_norm=final_norm)
    out_shape = (
        jax.ShapeDtypeStruct((b, seq, d), F32),
        jax.ShapeDtypeStruct((b, PAD, aw), F32),
        jax.ShapeDtypeStruct((b, PAD, aw), F32),
        jax.ShapeDtypeStruct((b, CONV_WIDTH - 1, lw), F32),
        jax.ShapeDtypeStruct((b, 1, lw), F32),
    )
    cur = lambda i, j: (i, jnp.minimum(j, n_tiles - 1), 0)
    prev = lambda i, j: (i, jnp.maximum(j - 1, 0), 0)
    in_specs = [
        pl.BlockSpec((1, tile, d), cur),
        pl.BlockSpec((1, tile, d), prev),
        pl.BlockSpec((1, tile, p.shape[-1]), prev),
        _const_spec(bias),
    ] + [_const_spec(a) for a in wlist]
    out_specs = (
        pl.BlockSpec((1, tile, d), prev),
        pl.BlockSpec((1, PAD, aw), lambda i, j: (i, 0, 0)),
        pl.BlockSpec((1, PAD, aw), lambda i, j: (i, 0, 0)),
        pl.BlockSpec((1, CONV_WIDTH - 1, lw), lambda i, j: (i, 0, 0)),
        pl.BlockSpec((1, 1, lw), lambda i, j: (i, 0, 0)),
    )
    scratch = [
        pltpu.VMEM((PAD + tile, aw), BF16),
        pltpu.VMEM((PAD + tile, aw), BF16),
        pltpu.VMEM((tile, aw), BF16),
        pltpu.VMEM((tile, aw), F32),
        pltpu.VMEM((tile, lw), F32),
        pltpu.VMEM((SUBLANES + tile, lw), F32),
        pltpu.VMEM((1, lw), F32),
    ]
    return pl.pallas_call(
        kern,
        out_shape=out_shape,
        grid=(b, n_tiles + 1),
        in_specs=in_specs,
        out_specs=out_specs,
        scratch_shapes=scratch,
        compiler_params=pltpu.CompilerParams(
            dimension_semantics=("parallel", "arbitrary"), vmem_limit_bytes=VMEM_LIMIT),
        name="prompt_layer",
    )(x, x, p, bias, *wlist)


FRONT_NAMES = ("g_mix", "w_in", "conv_w", "conv_b", "w_r", "b_r", "w_i", "b_i", "lam",
               "g_attn_out", "g_lru_out")
BACK_NAMES = ("g_attn_out", "g_lru_out", "w_out", "g_ffn", "w_ffn_gate", "w_ffn_up",
              "w_ffn_down", "g_ple", "w_ple_gate", "w_ple_proj", "g_final")


def _sample_front_kernel(*refs, n_heads):
    x_ref, ck_ref, cv_ref, sconv_ref, sh_ref, bias_ref = refs[:6]
    nw = len(FRONT_NAMES)
    w = dict(zip(FRONT_NAMES, refs[6:6 + nw]))
    attn_ref, lru_ref, ko_ref, vo_ref, convo_ref, ho_ref = refs[6 + nw:12 + nw]
    (xr_buf,) = refs[12 + nw:]

    nb, t, d = x_ref.shape
    n_cache = ck_ref.shape[3]
    aw = n_heads * HEAD_DIM
    rows = nb * t

    x = x_ref[...].reshape(rows, d)
    q, k, v, xr, gr = _project(x, w)
    ko_ref[...] = k.reshape(nb, t, aw)
    vo_ref[...] = v.reshape(nb, t, aw)
    qb = q.astype(BF16)
    kb_new = k.astype(BF16)
    vb_new = v.astype(BF16)

    def scores(b, h):
        rs = slice(b * t, (b + 1) * t)
        cols = slice(h * HEAD_DIM, (h + 1) * HEAD_DIM)
        qh = qb[rs, cols]
        kc_t = ck_ref[b, h].astype(BF16)
        s_c = _dot(qh, kc_t) + bias_ref[h, 0:t, PAD - n_cache:PAD]
        s_n = _dot_nt(qh, kb_new[rs, cols]) + bias_ref[h, 0:t, PAD:PAD + t]
        return s_c, s_n

    s_cur = [scores(0, h) for h in range(n_heads)]
    for b in range(nb):
        s_next = [scores(b + 1, h) for h in range(n_heads)] if b + 1 < nb else None
        outs = []
        for h in range(n_heads):
            vc_t = cv_ref[b, h].astype(BF16)
            vn = vb_new[b * t:(b + 1) * t, h * HEAD_DIM:(h + 1) * HEAD_DIM]
            outs.append(_softmax_pv(list(s_cur[h]), [vc_t, vn], [True, False]))
        attn_ref[b] = jnp.concatenate(outs, axis=-1)
        s_cur = s_next

    seg_rows = SUBLANES + t
    xcs = []
    for b in range(nb):
        base = b * seg_rows
        xr_buf[base + SUBLANES - (CONV_WIDTH - 1):base + SUBLANES, :] = sconv_ref[b]
        xr_buf[base + SUBLANES:base + seg_rows, :] = xr[b * t:(b + 1) * t, :]
        xcs.append(_conv(xr_buf, base + SUBLANES, t, w))
        convo_ref[b] = xr_buf[base + seg_rows - (CONV_WIDTH - 1):base + seg_rows, :]
    xc = jnp.concatenate(xcs, axis=0)
    h0_rows = jnp.concatenate(
        [jnp.broadcast_to(sh_ref[b], (t, sh_ref.shape[2])) for b in range(nb)], axis=0)
    hs, lru_out = _rglru(xc, gr, h0_rows, t, w)
    for b in range(nb):
        ho_ref[b] = hs[(b + 1) * t - 1:(b + 1) * t, :]
    lru_ref[...] = lru_out.reshape(nb, t, lru_out.shape[1])


def _sample_back_kernel(*refs, final_norm):
    x_ref, attn_ref, lru_ref, p_ref = refs[:4]
    nw = len(BACK_NAMES)
    w = dict(zip(BACK_NAMES, refs[4:4 + nw]))
    y_ref = refs[4 + nw]
    y_ref[...] = _dense_tail(x_ref[...], attn_ref[...], lru_ref[...], p_ref[...], w, final_norm)


def _sample_layer(x, p, cache_k, cache_v, state_conv, state_h, bias, weights, final_norm, group,
                  back_rows):
    nbatch, t, d = x.shape
    n_heads = bias.shape[0]
    aw = n_heads * HEAD_DIM
    n_cache = cache_k.shape[3]
    lw = weights["g_lru_out"].shape[1]
    assert nbatch % group == 0 and t % SUBLANES == 0 and t & (t - 1) == 0
    assert CONV_WIDTH - 1 <= t <= CHUNK and n_cache <= PAD and cache_k.shape[1:3] == (n_heads, HEAD_DIM)
    assert (nbatch * t) % back_rows == 0
    state_h3 = state_h.reshape(nbatch, 1, lw)

    def grp(shape):
        nd = len(shape)
        return pl.BlockSpec((group,) + tuple(shape[1:]), lambda i: (i,) + (0,) * (nd - 1))

    front_w = [weights[n] for n in FRONT_NAMES]
    front_out = (
        jax.ShapeDtypeStruct((nbatch, t, aw), F32),
        jax.ShapeDtypeStruct((nbatch, t, lw), F32),
        jax.ShapeDtypeStruct((nbatch, t, aw), F32),
        jax.ShapeDtypeStruct((nbatch, t, aw), F32),
        jax.ShapeDtypeStruct((nbatch, CONV_WIDTH - 1, lw), F32),
        jax.ShapeDtypeStruct((nbatch, 1, lw), F32),
    )
    ins = [x, cache_k, cache_v, state_conv, state_h3]
    attn, lru_out, k_new, v_new, conv_new, h_new = pl.pallas_call(
        functools.partial(_sample_front_kernel, n_heads=n_heads),
        out_shape=front_out,
        grid=(nbatch // group,),
        in_specs=[grp(a.shape) for a in ins] + [_const_spec(bias)]
        + [_const_spec(a) for a in front_w],
        out_specs=tuple(grp(o.shape) for o in front_out),
        scratch_shapes=[pltpu.VMEM((group * (SUBLANES + t), lw), F32)],
        compiler_params=pltpu.CompilerParams(
            dimension_semantics=("arbitrary",), vmem_limit_bytes=VMEM_LIMIT),
        name="sample_front",
    )(*ins, bias, *front_w)

    rows = nbatch * t
    back_w = [weights[n] for n in BACK_NAMES]
    flat = [x.reshape(rows, d), attn.reshape(rows, aw), lru_out.reshape(rows, lw),
            p.reshape(rows, p.shape[-1])]
    y = pl.pallas_call(
        functools.partial(_sample_back_kernel, final_norm=final_norm),
        out_shape=jax.ShapeDtypeStruct((rows, d), F32),
        grid=(rows // back_rows,),
        in_specs=[pl.BlockSpec((back_rows, a.shape[1]), lambda i: (i, 0)) for a in flat]
        + [_const_spec(a) for a in back_w],
        out_specs=pl.BlockSpec((back_rows, d), lambda i: (i, 0)),
        compiler_params=pltpu.CompilerParams(
            dimension_semantics=("arbitrary",), vmem_limit_bytes=VMEM_LIMIT),
        name="sample_back",
    )(*flat, *back_w)
    return y.reshape(nbatch, t, d), k_new, v_new, conv_new, h_new


def _block_diag(wb):
    n, kk, jj = wb.shape
    eye = jnp.eye(n, dtype=wb.dtype)
    return (wb[:, :, None, :] * eye[:, None, :, None]).reshape(n * kk, n * jj)


PROMPT_TILE = 256
SAMPLE_GROUP = 8
SAMPLE_BACK_ROWS = 256


def kernel(x_prompt, x_sample, p_prompt, p_sample, cache_k, cache_v, state_conv, state_h, g_mix, w_in, conv_w, conv_b, w_rgate, b_rgate, w_igate, b_igate, lru_lambda, rel_bias_table, g_attn_out, g_lru_out, w_out, g_ffn, w_ffn_gate, w_ffn_up, w_ffn_down, g_ple, w_ple_gate, w_ple_proj, g_final):
    depth = w_in.shape[0]
    hp, hs = x_prompt, x_sample
    outs = [[] for _ in range(8)]
    for i in range(depth):
        row = lambda a: a[i].reshape(1, -1).astype(F32)
        weights = {
            "g_mix": row(g_mix), "w_in": w_in[i].astype(BF16),
            "conv_w": conv_w[i].astype(F32), "conv_b": row(conv_b),
            "w_r": _block_diag(w_rgate[i]).astype(BF16), "b_r": row(b_rgate),
            "w_i": _block_diag(w_igate[i]).astype(BF16), "b_i": row(b_igate),
            "lam": row(lru_lambda),
            "g_attn_out": row(g_attn_out), "g_lru_out": row(g_lru_out),
            "w_out": w_out[i].astype(BF16), "g_ffn": row(g_ffn),
            "w_ffn_gate": w_ffn_gate[i].astype(BF16), "w_ffn_up": w_ffn_up[i].astype(BF16),
            "w_ffn_down": w_ffn_down[i].astype(BF16), "g_ple": row(g_ple),
            "w_ple_gate": w_ple_gate[i].astype(BF16), "w_ple_proj": w_ple_proj[i].astype(BF16),
            "g_final": g_final.reshape(1, -1).astype(F32),
        }
        final = i == depth - 1
        bias, bias2 = _bias_tables(rel_bias_table[i])
        n_heads = rel_bias_table.shape[1]
        aw = n_heads * HEAD_DIM

        hp, k1, v1, c1, r1 = _prompt_layer(
            hp, p_prompt[i], bias2, weights, final, PROMPT_TILE)
        nb, nc = cache_k.shape[1:3]
        hs, k2, v2, c2, r2 = _sample_layer(
            hs, p_sample[i], jnp.transpose(cache_k[i], (0, 2, 3, 1)),
            jnp.transpose(cache_v[i], (0, 2, 3, 1)), state_conv[i], state_h[i], bias,
            weights, final, SAMPLE_GROUP, SAMPLE_BACK_ROWS)
        bp, keep = k1.shape[:2]
        ts = k2.shape[1]
        for lst, val in zip(outs, (
                k1.reshape(bp, keep, n_heads, HEAD_DIM), v1.reshape(bp, keep, n_heads, HEAD_DIM),
                c1, r1.reshape(bp, -1),
                k2.reshape(nb, ts, n_heads, HEAD_DIM), v2.reshape(nb, ts, n_heads, HEAD_DIM),
                c2, r2.reshape(nb, -1))):
            lst.append(val)
    return (hp, hs) + tuple(jnp.stack(l) for l in outs)
```
